```python
import math
import jax
import jax.numpy as jnp
from jax import lax
import numpy as np

D_MODEL = 4096
BATCH = 2
SEQ = 4096
DEPTH = 2

HEAD_DIM = 128
N_MIXERS = 4
MIX_WIDTH = D_MODEL
GROUP_WIDTH = MIX_WIDTH // N_MIXERS
GROUP_HEADS = GROUP_WIDTH // HEAD_DIM
SWA_HEADS = GROUP_HEADS
SWA_KV_HEADS = 2
WINDOW = 128
Q_BLOCK = 128
CONV_CHANNELS = GROUP_WIDTH
CONV_WIDTH = 31
NSA_HEADS = GROUP_HEADS
NSA_KV_HEADS = 2
CMP_LEN = 32
CMP_STRIDE = 16
SLC_LEN = 64
N_SELECT = 16
SLC_Q_CHUNK = 64
N_BRANCH = 3
SB_HEADS = GROUP_HEADS
N_BUCKETS = 32
MAX_DISTANCE = 128
N_BIAS_HEADS = SWA_HEADS + NSA_HEADS
D_FF = 11008
N_EXPERTS = 8
TOP_K = 2
D_EXPERT = D_FF // 2
NORM_EPS = 1e-6

IN_SIZES = (
    SWA_HEADS * HEAD_DIM, SWA_KV_HEADS * HEAD_DIM, SWA_KV_HEADS * HEAD_DIM,
    CONV_CHANNELS, CONV_CHANNELS,
    NSA_HEADS * HEAD_DIM, *([NSA_KV_HEADS * HEAD_DIM] * 6), NSA_HEADS * N_BRANCH,
    SB_HEADS * HEAD_DIM, SB_HEADS * HEAD_DIM, SB_HEADS * HEAD_DIM,
)
IN_COLS = sum(IN_SIZES)
IN_SPLITS = tuple(int(v) for v in np.cumsum(IN_SIZES)[:-1])

kernel_name = 'hybrid_parallel_mixer_trunk'


def _rmsnorm(x, g):
    xf = x.astype(jnp.float32)
    y = xf * lax.rsqrt(jnp.mean(xf * xf, axis=-1, keepdims=True) + NORM_EPS)
    return (y * g.astype(jnp.float32)).astype(x.dtype)


def _layernorm(x, g, b):
    xf = x.astype(jnp.float32)
    mu = jnp.mean(xf, axis=-1, keepdims=True)
    var = jnp.mean(jnp.square(xf - mu), axis=-1, keepdims=True)
    y = (xf - mu) * lax.rsqrt(var + NORM_EPS)
    return (y * g.astype(jnp.float32) + b.astype(jnp.float32)).astype(x.dtype)


def _t5_bucket(dist):
    n = jnp.maximum(dist, 0)
    max_exact = N_BUCKETS // 2
    nf = jnp.maximum(n, 1).astype(jnp.float32)
    large = max_exact + (jnp.log(nf / max_exact) / math.log(MAX_DISTANCE / max_exact)
                         * (N_BUCKETS - max_exact)).astype(jnp.int32)
    large = jnp.minimum(large, N_BUCKETS - 1)
    return jnp.where(n < max_exact, n, large)


def _masked_softmax(logits, mask, sink=None):
    s = jnp.where(mask, logits, -jnp.inf)
    m = jnp.max(s, axis=-1, keepdims=True)
    if sink is not None:
        m = jnp.maximum(m, sink)
    m = jnp.where(jnp.isfinite(m), m, 0.0)
    p = jnp.where(mask, jnp.exp(s - m), 0.0)
    den = jnp.sum(p, axis=-1, keepdims=True)
    if sink is not None:
        den = den + jnp.exp(sink - m)
    return p / jnp.maximum(den, jnp.finfo(jnp.float32).tiny)


def _banded_gqa(q, k, v, bias_heads, sink=None):
    b, s, h, d = q.shape
    g = k.shape[2]
    r = h // g
    nb = s // Q_BLOCK
    qb = q.reshape(b, nb, Q_BLOCK, g, r, d)

    def band(t):
        tb = t.reshape(b, nb, Q_BLOCK, g, d)
        prev = jnp.pad(tb, ((0, 0), (1, 0), (0, 0), (0, 0), (0, 0)))[:, :-1]
        return jnp.concatenate([prev, tb], axis=2)

    kb, vb = band(k), band(v)
    logits = jnp.einsum('bnqgrd,bnkgd->bngrqk', qb, kb,
                        preferred_element_type=jnp.float32) * (HEAD_DIM ** -0.5)
    qi = jnp.arange(Q_BLOCK)[:, None]
    kj = jnp.arange(2 * Q_BLOCK)[None, :]
    dist = qi + Q_BLOCK - kj
    blk = jnp.arange(nb)[:, None, None]
    mask = (dist >= 0) & (dist < WINDOW) & ((blk > 0) | (kj >= Q_BLOCK))
    bias = bias_heads[_t5_bucket(dist)].astype(jnp.float32)
    bias = jnp.transpose(bias, (2, 0, 1)).reshape(g, r, Q_BLOCK, 2 * Q_BLOCK)
    sink_b = None if sink is None else sink.astype(jnp.float32).reshape(g, r)[None, None, :, :, None, None]
    p = _masked_softmax(logits + bias, mask[None, :, None, None], sink_b)
    out = jnp.einsum('bngrqk,bnkgd->bnqgrd', p.astype(v.dtype), vb)
    return out.reshape(b, s, h, d)


def _conformer_conv(val, gate, dw, dw_b, ln_g, ln_b, pw):
    hid = val * jax.nn.sigmoid(gate)
    c = hid.shape[-1]
    hid = lax.conv_general_dilated(
        hid, dw[:, None, :].astype(hid.dtype), window_strides=(1,),
        padding=[(CONV_WIDTH - 1, 0)], dimension_numbers=('NWC', 'WIO', 'NWC'),
        feature_group_count=c) + dw_b.astype(hid.dtype)
    hid = jax.nn.silu(_layernorm(hid, ln_g, ln_b))
    return hid @ pw


def _compress(t, pe, w1, w2):
    n_cmp = (t.shape[1] - CMP_LEN) // CMP_STRIDE + 1
    idx = jnp.arange(n_cmp)[:, None] * CMP_STRIDE + jnp.arange(CMP_LEN)[None, :]
    blocks = t[:, idx] + pe[:, None, :]
    b, nc, L, g, d = blocks.shape
    flat = jnp.transpose(blocks, (0, 1, 3, 2, 4)).reshape(b, nc, g, L * d)
    return jax.nn.silu(flat @ w1) @ w2


def _nsa(q, kc, vc, ks, vs, kw, vw, gate_logits, cmp_pe, cmp_w1, cmp_w2, bias_heads):
    b, s, h, d = q.shape
    g = kc.shape[2]
    r = h // g
    scale = HEAD_DIM ** -0.5
    q5 = q.reshape(b, s, g, r, d)
    pos = jnp.arange(s)

    kcmp = _compress(kc, cmp_pe[0], cmp_w1[0], cmp_w2[0])
    vcmp = _compress(vc, cmp_pe[1], cmp_w1[1], cmp_w2[1])
    n_cmp = kcmp.shape[1]
    cmp_start = jnp.arange(n_cmp) * CMP_STRIDE
    dist_c = pos[:, None] - (cmp_start + CMP_LEN - 1)[None, :]
    bias_c = jnp.transpose(bias_heads[_t5_bucket(dist_c)], (2, 0, 1)).reshape(g, r, s, n_cmp)
    logits_c = jnp.einsum('bsgrd,bcgd->bgrsc', q5, kcmp,
                          preferred_element_type=jnp.float32) * scale + bias_c.astype(jnp.float32)
    p_cmp = _masked_softmax(logits_c, dist_c >= 0)
    o_cmp = jnp.einsum('bgrsc,bcgd->bsgrd', p_cmp.astype(vcmp.dtype), vcmp)

    n_slc = s // SLC_LEN
    n_sel = min(N_SELECT, n_slc)
    slc_start = jnp.arange(n_slc) * SLC_LEN
    overlap = ((cmp_start[:, None] < slc_start[None, :] + SLC_LEN)
               & (cmp_start[:, None] + CMP_LEN > slc_start[None, :])).astype(jnp.float32)
    importance = jnp.einsum('bgrsc,cj->bgsj', p_cmp, overlap)
    blk = jnp.arange(n_slc)[None, :]
    cur = (pos // SLC_LEN)[:, None]
    forced = (blk == 0) | (blk == cur) | (blk == cur - 1)
    score = jnp.where(forced, jnp.inf, jnp.where(blk <= cur, importance, -jnp.inf))
    _, sel = lax.top_k(score, n_sel)

    ksb = jnp.swapaxes(ks, 1, 2).reshape(b, g, n_slc, SLC_LEN, d)
    vsb = jnp.swapaxes(vs, 1, 2).reshape(b, g, n_slc, SLC_LEN, d)
    bias_g = jnp.transpose(bias_heads.reshape(N_BUCKETS, g, r), (1, 0, 2))
    bi = jnp.arange(b)[:, None, None, None]
    gi = jnp.arange(g)[None, :, None, None]
    offs = jnp.arange(SLC_LEN)

    def chunk(args):
        qc, selc, posc = args
        qcn = qc.shape[1]
        tlen = n_sel * SLC_LEN
        kg = ksb[bi, gi, selc]
        vg = vsb[bi, gi, selc]
        tok = selc[..., None] * SLC_LEN + offs
        dist = posc[None, None, :, None, None] - tok
        bias = jnp.moveaxis(bias_g[gi[..., None], _t5_bucket(dist)], -1, 2)
        logits = jnp.einsum('bqgrd,bgqnld->bgrqnl', qc, kg,
                            preferred_element_type=jnp.float32) * scale + bias.astype(jnp.float32)
        p = _masked_softmax(logits.reshape(b, g, r, qcn, tlen),
                            (dist >= 0).reshape(b, g, 1, qcn, tlen))
        return jnp.einsum('bgrqt,bgqtd->bqgrd', p.astype(vg.dtype), vg.reshape(b, g, qcn, tlen, d))

    nc = s // SLC_Q_CHUNK
    q_chunks = jnp.moveaxis(q5.reshape(b, nc, SLC_Q_CHUNK, g, r, d), 1, 0)
    sel_chunks = jnp.moveaxis(sel.reshape(b, g, nc, SLC_Q_CHUNK, n_sel), 2, 0)
    pos_chunks = pos.reshape(nc, SLC_Q_CHUNK)
    o_slc = lax.map(chunk, (q_chunks, sel_chunks, pos_chunks))
    o_slc = jnp.moveaxis(o_slc, 0, 1).reshape(b, s, g, r, d)

    o_win = _banded_gqa(q, kw, vw, bias_heads).reshape(b, s, g, r, d)

    gates = jax.nn.sigmoid(gate_logits).reshape(b, s, g, r, N_BRANCH)
    out = gates[..., 0:1] * o_cmp + gates[..., 1:2] * o_slc + gates[..., 2:3] * o_win
    return out.reshape(b, s, h * d)


def _stick_breaking(q, k, v):
    b, s, h, d = q.shape
    nb = s // Q_BLOCK
    kpos = jnp.arange(s)
    scale = HEAD_DIM ** -0.5

    def block(args):
        qb, qpos = args
        z = jnp.einsum('bqhd,bkhd->bhqk', qb, k, preferred_element_type=jnp.float32) * scale
        before = kpos[None, :] < qpos[:, None]
        log_keep = jnp.where(before, jax.nn.log_sigmoid(-z), 0.0)
        shifted = jnp.pad(log_keep[..., 1:], ((0, 0), (0, 0), (0, 0), (0, 1)))
        suffix = lax.cumsum(shifted, axis=3, reverse=True)
        a = jnp.where(before, jnp.exp(jax.nn.log_sigmoid(z) + suffix), 0.0)
        return jnp.einsum('bhqk,bkhd->bqhd', a.astype(v.dtype), v)

    q_blocks = jnp.swapaxes(q.reshape(b, nb, Q_BLOCK, h, d), 0, 1)
    out = lax.map(block, (q_blocks, kpos.reshape(nb, Q_BLOCK)))
    return jnp.swapaxes(out, 0, 1).reshape(b, s, h * d)


def _swiglu(h, w_gate, w_up, w_down):
    return (jax.nn.silu(h @ w_gate) * (h @ w_up)) @ w_down


def _moe_swiglu(h, router, w_gate, w_up, w_down):
    b, s, dm = h.shape
    t = h.reshape(b * s, dm)
    probs = jax.nn.softmax((t @ router).astype(jnp.float32), axis=-1)
    top_p, top_i = lax.top_k(probs, TOP_K)
    top_p = top_p / jnp.sum(top_p, axis=-1, keepdims=True)
    gate = jnp.sum(jax.nn.one_hot(top_i, N_EXPERTS, dtype=jnp.float32) * top_p[..., None], axis=1)
    out = jnp.zeros_like(t)
    for e in range(N_EXPERTS):
        y = _swiglu(t, w_gate[e], w_up[e], w_down[e])
        out = out + gate[:, e:e + 1].astype(t.dtype) * y
    return out.reshape(b, s, dm)


def setup_inputs(seed: int = 0) -> dict:
    key = jax.random.key(seed)
    ks = jax.random.split(key, 24)
    n_dense = (DEPTH + 1) // 2
    n_moe = DEPTH // 2

    def nrm(k, shape, scale):
        return jax.random.normal(k, shape, jnp.float32) * scale

    return {
        'x': nrm(ks[0], (BATCH, SEQ, D_MODEL), 1.0),
        'attn_norm_g': 1.0 + nrm(ks[1], (DEPTH, D_MODEL), 0.01),
        'ffn_norm_g': 1.0 + nrm(ks[2], (DEPTH, D_MODEL), 0.01),
        'final_norm_g': 1.0 + nrm(ks[3], (D_MODEL,), 0.01),
        'w_in': nrm(ks[4], (DEPTH, D_MODEL, IN_COLS), D_MODEL ** -0.5),
        'w_out': nrm(ks[5], (DEPTH, MIX_WIDTH, D_MODEL), MIX_WIDTH ** -0.5),
        'mix_norm_g': 1.0 + nrm(ks[6], (DEPTH, MIX_WIDTH), 0.01),
        'rel_bias': nrm(ks[7], (N_BUCKETS, N_BIAS_HEADS), 0.2),
        'swa_sinks': nrm(ks[8], (DEPTH, SWA_HEADS), 0.5),
        'conv_dw': nrm(ks[9], (DEPTH, CONV_WIDTH, CONV_CHANNELS), CONV_WIDTH ** -0.5),
        'conv_dw_b': nrm(ks[10], (DEPTH, CONV_CHANNELS), 0.01),
        'conv_ln_g': 1.0 + nrm(ks[11], (DEPTH, CONV_CHANNELS), 0.01),
        'conv_ln_b': nrm(ks[12], (DEPTH, CONV_CHANNELS), 0.01),
        'conv_pw': nrm(ks[13], (DEPTH, CONV_CHANNELS, CONV_CHANNELS), CONV_CHANNELS ** -0.5),
        'nsa_cmp_pe': nrm(ks[14], (DEPTH, 2, CMP_LEN, HEAD_DIM), 0.02),
        'nsa_cmp_w1': nrm(ks[15], (DEPTH, 2, CMP_LEN * HEAD_DIM, HEAD_DIM), (CMP_LEN * HEAD_DIM) ** -0.5),
        'nsa_cmp_w2': nrm(ks[16], (DEPTH, 2, HEAD_DIM, HEAD_DIM), HEAD_DIM ** -0.5),
        'ffn_w_gate': nrm(ks[17], (n_dense, D_MODEL, D_FF), D_MODEL ** -0.5),
        'ffn_w_up': nrm(ks[18], (n_dense, D_MODEL, D_FF), D_MODEL ** -0.5),
        'ffn_w_down': nrm(ks[19], (n_dense, D_FF, D_MODEL), D_FF ** -0.5),
        'moe_router': nrm(ks[20], (n_moe, D_MODEL, N_EXPERTS), D_MODEL ** -0.5),
        'moe_w_gate': nrm(ks[21], (n_moe, N_EXPERTS, D_MODEL, D_EXPERT), D_MODEL ** -0.5),
        'moe_w_up': nrm(ks[22], (n_moe, N_EXPERTS, D_MODEL, D_EXPERT), D_MODEL ** -0.5),
        'moe_w_down': nrm(ks[23], (n_moe, N_EXPERTS, D_EXPERT, D_MODEL), D_EXPERT ** -0.5),
    }


def reference(x, attn_norm_g, ffn_norm_g, final_norm_g, w_in, w_out, mix_norm_g, rel_bias,
              swa_sinks, conv_dw, conv_dw_b, conv_ln_g, conv_ln_b, conv_pw, nsa_cmp_pe,
              nsa_cmp_w1, nsa_cmp_w2, ffn_w_gate, ffn_w_up, ffn_w_down, moe_router,
              moe_w_gate, moe_w_up, moe_w_down):
    b, s, _ = x.shape

    def heads(t):
        return t.reshape(b, s, -1, HEAD_DIM)

    bias_a = rel_bias[:, :SWA_HEADS]
    bias_c = rel_bias[:, SWA_HEADS:]
    for layer in range(DEPTH):
        h = _rmsnorm(x, attn_norm_g[layer])
        (a_q, a_k, a_v, b_val, b_gate, c_q, c_kc, c_vc, c_ks, c_vs, c_kw, c_vw, c_g,
         d_q, d_k, d_v) = jnp.split(h @ w_in[layer], IN_SPLITS, axis=-1)
        o_a = _banded_gqa(heads(a_q), heads(a_k), heads(a_v), bias_a,
                          swa_sinks[layer]).reshape(b, s, GROUP_WIDTH)
        o_b = _conformer_conv(b_val, b_gate, conv_dw[layer], conv_dw_b[layer],
                              conv_ln_g[layer], conv_ln_b[layer], conv_pw[layer])
        o_c = _nsa(heads(c_q), heads(c_kc), heads(c_vc), heads(c_ks), heads(c_vs),
                   heads(c_kw), heads(c_vw), c_g, nsa_cmp_pe[layer], nsa_cmp_w1[layer],
                   nsa_cmp_w2[layer], bias_c)
        o_d = _stick_breaking(heads(d_q), heads(d_k), heads(d_v))
        mixed = jnp.stack([o_a, o_b, o_c, o_d], axis=2)
        mixed = _rmsnorm(mixed, mix_norm_g[layer].reshape(N_MIXERS, GROUP_WIDTH)).reshape(b, s, MIX_WIDTH)
        x = x + mixed @ w_out[layer]
        h = _rmsnorm(x, ffn_norm_g[layer])
        if layer % 2 == 0:
            i = layer // 2
            x = x + _swiglu(h, ffn_w_gate[i], ffn_w_up[i], ffn_w_down[i])
        else:
            i = layer // 2
            x = x + _moe_swiglu(h, moe_router[i], moe_w_gate[i], moe_w_up[i], moe_w_down[i])
    return _rmsnorm(x, final_norm_g)
```

```python
import functools
import math

import jax
import jax.numpy as jnp
import numpy as np
from jax import lax
from jax.experimental import pallas as pl
from jax.experimental.pallas import tpu as pltpu

D_MODEL = 4096
DEPTH = 2
HEAD_DIM = 128
N_MIXERS = 4
GROUP_WIDTH = D_MODEL // N_MIXERS
GROUP_HEADS = GROUP_WIDTH // HEAD_DIM
SWA_HEADS = GROUP_HEADS
SWA_KV_HEADS = 2
WINDOW = 128
Q_BLOCK = 128
CONV_WIDTH = 31
NSA_HEADS = GROUP_HEADS
NSA_KV_HEADS = 2
CMP_LEN = 32
CMP_STRIDE = 16
SLC_LEN = 64
N_SELECT = 16
SLC_Q_CHUNK = 64
N_BRANCH = 3
SB_HEADS = GROUP_HEADS
N_BUCKETS = 32
MAX_DISTANCE = 128
D_FF = 11008
N_EXPERTS = 8
TOP_K = 2
D_EXPERT = D_FF // 2
NORM_EPS = 1e-6

IN_SIZES = (
    SWA_HEADS * HEAD_DIM, SWA_KV_HEADS * HEAD_DIM, SWA_KV_HEADS * HEAD_DIM,
    GROUP_WIDTH, GROUP_WIDTH,
    NSA_HEADS * HEAD_DIM, *([NSA_KV_HEADS * HEAD_DIM] * 6), NSA_HEADS * N_BRANCH,
    SB_HEADS * HEAD_DIM, SB_HEADS * HEAD_DIM, SB_HEADS * HEAD_DIM,
)
IN_SPLITS = tuple(int(v) for v in np.cumsum(IN_SIZES)[:-1])

VMEM_LIMIT_BYTES = 56 * 1024 * 1024


def _mm_kernel(x_ref, w_ref, o_ref):
    o_ref[...] = jnp.dot(x_ref[...], w_ref[...],
                         preferred_element_type=jnp.float32).astype(o_ref.dtype)


def _mm_res_kernel(x_ref, w_ref, r_ref, o_ref):
    acc = jnp.dot(x_ref[...], w_ref[...], preferred_element_type=jnp.float32)
    o_ref[...] = (r_ref[...] + acc).astype(o_ref.dtype)


def _mm_res_ksplit_kernel(x_ref, w_ref, r_ref, o_ref):
    k = pl.program_id(2)

    @pl.when(k == 0)
    def _():
        o_ref[...] = r_ref[...]

    o_ref[...] += jnp.dot(x_ref[...], w_ref[...], preferred_element_type=jnp.float32)


def _matmul(x, w, *, bm, bn, out_dtype, residual=None, name="matmul"):
    m, k = x.shape
    _, n = w.shape
    assert m % bm == 0 and n % bn == 0
    in_specs = [pl.BlockSpec((bm, k), lambda i, j: (i, 0)),
                pl.BlockSpec((k, bn), lambda i, j: (0, j))]
    args = [x, w]
    body = _mm_kernel
    if residual is not None:
        in_specs.append(pl.BlockSpec((bm, bn), lambda i, j: (i, j)))
        args.append(residual)
        body = _mm_res_kernel
    return pl.pallas_call(
        body,
        out_shape=jax.ShapeDtypeStruct((m, n), out_dtype),
        grid=(m // bm, n // bn),
        in_specs=in_specs,
        out_specs=pl.BlockSpec((bm, bn), lambda i, j: (i, j)),
        compiler_params=pltpu.CompilerParams(
            dimension_semantics=("parallel", "arbitrary"),
            vmem_limit_bytes=VMEM_LIMIT_BYTES),
        name=name,
    )(*args)


def _matmul_ksplit_res(x, w, residual, *, bm, bn, bk, name="matmul_ksplit"):
    m, k = x.shape
    _, n = w.shape
    assert m % bm == 0 and n % bn == 0 and k % bk == 0
    return pl.pallas_call(
        _mm_res_ksplit_kernel,
        out_shape=jax.ShapeDtypeStruct((m, n), jnp.float32),
        grid=(m // bm, n // bn, k // bk),
        in_specs=[pl.BlockSpec((bm, bk), lambda i, j, kk: (i, kk)),
                  pl.BlockSpec((bk, bn), lambda i, j, kk: (kk, j)),
                  pl.BlockSpec((bm, bn), lambda i, j, kk: (i, j))],
        out_specs=pl.BlockSpec((bm, bn), lambda i, j, kk: (i, j)),
        compiler_params=pltpu.CompilerParams(
            dimension_semantics=("parallel", "arbitrary", "arbitrary"),
            vmem_limit_bytes=VMEM_LIMIT_BYTES),
        name=name,
    )(x, w, residual)


def _swiglu_kernel(x_ref, wg_ref, wu_ref, o_ref):
    x = x_ref[...]
    g = jnp.dot(x, wg_ref[...], preferred_element_type=jnp.float32)
    u = jnp.dot(x, wu_ref[...], preferred_element_type=jnp.float32)
    o_ref[...] = (g * jax.nn.sigmoid(g) * u).astype(o_ref.dtype)


def _swiglu_up(x, wg, wu, *, bm, bn, name="swiglu_up"):
    m, k = x.shape
    _, n = wg.shape
    assert m % bm == 0 and n % bn == 0
    return pl.pallas_call(
        _swiglu_kernel,
        out_shape=jax.ShapeDtypeStruct((m, n), jnp.bfloat16),
        grid=(m // bm, n // bn),
        in_specs=[pl.BlockSpec((bm, k), lambda i, j: (i, 0)),
                  pl.BlockSpec((k, bn), lambda i, j: (0, j)),
                  pl.BlockSpec((k, bn), lambda i, j: (0, j))],
        out_specs=pl.BlockSpec((bm, bn), lambda i, j: (i, j)),
        compiler_params=pltpu.CompilerParams(
            dimension_semantics=("parallel", "arbitrary"),
            vmem_limit_bytes=VMEM_LIMIT_BYTES),
        name=name,
    )(x, wg, wu)


def _rmsnorm(x, g):
    xf = x.astype(jnp.float32)
    y = xf * lax.rsqrt(jnp.mean(xf * xf, axis=-1, keepdims=True) + NORM_EPS)
    return (y * g.astype(jnp.float32)).astype(x.dtype)


def _layernorm(x, g, b):
    xf = x.astype(jnp.float32)
    mu = jnp.mean(xf, axis=-1, keepdims=True)
    var = jnp.mean(jnp.square(xf - mu), axis=-1, keepdims=True)
    y = (xf - mu) * lax.rsqrt(var + NORM_EPS)
    return (y * g.astype(jnp.float32) + b.astype(jnp.float32)).astype(x.dtype)


def _t5_bucket(dist):
    n = jnp.maximum(dist, 0)
    max_exact = N_BUCKETS // 2
    nf = jnp.maximum(n, 1).astype(jnp.float32)
    large = max_exact + (jnp.log(nf / max_exact) / math.log(MAX_DISTANCE / max_exact)
                         * (N_BUCKETS - max_exact)).astype(jnp.int32)
    large = jnp.minimum(large, N_BUCKETS - 1)
    return jnp.where(n < max_exact, n, large)


def _masked_softmax(logits, mask, sink=None):
    s = jnp.where(mask, logits, -jnp.inf)
    m = jnp.max(s, axis=-1, keepdims=True)
    if sink is not None:
        m = jnp.maximum(m, sink)
    m = jnp.where(jnp.isfinite(m), m, 0.0)
    p = jnp.where(mask, jnp.exp(s - m), 0.0)
    den = jnp.sum(p, axis=-1, keepdims=True)
    if sink is not None:
        den = den + jnp.exp(sink - m)
    return p / jnp.maximum(den, jnp.finfo(jnp.float32).tiny)


def _banded_gqa(q, k, v, bias_heads, sink=None):
    b, s, h, d = q.shape
    g = k.shape[2]
    r = h // g
    nb = s // Q_BLOCK
    qb = q.reshape(b, nb, Q_BLOCK, g, r, d)

    def band(t):
        tb = t.reshape(b, nb, Q_BLOCK, g, d)
        prev = jnp.pad(tb, ((0, 0), (1, 0), (0, 0), (0, 0), (0, 0)))[:, :-1]
        return jnp.concatenate([prev, tb], axis=2)

    kb, vb = band(k), band(v)
    logits = jnp.einsum('bnqgrd,bnkgd->bngrqk', qb, kb,
                        preferred_element_type=jnp.float32) * (HEAD_DIM ** -0.5)
    qi = jnp.arange(Q_BLOCK)[:, None]
    kj = jnp.arange(2 * Q_BLOCK)[None, :]
    dist = qi + Q_BLOCK - kj
    blk = jnp.arange(nb)[:, None, None]
    mask = (dist >= 0) & (dist < WINDOW) & ((blk > 0) | (kj >= Q_BLOCK))
    bias = bias_heads[_t5_bucket(dist)].astype(jnp.float32)
    bias = jnp.transpose(bias, (2, 0, 1)).reshape(g, r, Q_BLOCK, 2 * Q_BLOCK)
    sink_b = None if sink is None else sink.astype(jnp.float32).reshape(g, r)[None, None, :, :, None, None]
    p = _masked_softmax(logits + bias, mask[None, :, None, None], sink_b)
    out = jnp.einsum('bngrqk,bnkgd->bnqgrd', p.astype(v.dtype), vb)
    return out.reshape(b, s, h, d)


def _conformer_conv(val, gate, dw, dw_b, ln_g, ln_b, pw):
    hid = val * jax.nn.sigmoid(gate)
    c = hid.shape[-1]
    hid = lax.conv_general_dilated(
        hid, dw[:, None, :].astype(hid.dtype), window_strides=(1,),
        padding=[(CONV_WIDTH - 1, 0)], dimension_numbers=('NWC', 'WIO', 'NWC'),
        feature_group_count=c) + dw_b.astype(hid.dtype)
    hid = jax.nn.silu(_layernorm(hid, ln_g, ln_b))
    return hid @ pw


def _compress(t, pe, w1, w2):
    n_cmp = (t.shape[1] - CMP_LEN) // CMP_STRIDE + 1
    idx = jnp.arange(n_cmp)[:, None] * CMP_STRIDE + jnp.arange(CMP_LEN)[None, :]
    blocks = t[:, idx] + pe[:, None, :]
    b, nc, L, g, d = blocks.shape
    flat = jnp.transpose(blocks, (0, 1, 3, 2, 4)).reshape(b, nc, g, L * d)
    return jax.nn.silu(flat @ w1) @ w2


def _nsa(q, kc, vc, ks, vs, kw, vw, gate_logits, cmp_pe, cmp_w1, cmp_w2, bias_heads):
    b, s, h, d = q.shape
    g = kc.shape[2]
    r = h // g
    scale = HEAD_DIM ** -0.5
    q5 = q.reshape(b, s, g, r, d)
    pos = jnp.arange(s)

    kcmp = _compress(kc, cmp_pe[0], cmp_w1[0], cmp_w2[0])
    vcmp = _compress(vc, cmp_pe[1], cmp_w1[1], cmp_w2[1])
    n_cmp = kcmp.shape[1]
    cmp_start = jnp.arange(n_cmp) * CMP_STRIDE
    dist_c = pos[:, None] - (cmp_start + CMP_LEN - 1)[None, :]
    bias_c = jnp.transpose(bias_heads[_t5_bucket(dist_c)], (2, 0, 1)).reshape(g, r, s, n_cmp)
    logits_c = jnp.einsum('bsgrd,bcgd->bgrsc', q5, kcmp,
                          preferred_element_type=jnp.float32) * scale + bias_c.astype(jnp.float32)
    p_cmp = _masked_softmax(logits_c, dist_c >= 0)
    o_cmp = jnp.einsum('bgrsc,bcgd->bsgrd', p_cmp.astype(vcmp.dtype), vcmp)

    n_slc = s // SLC_LEN
    n_sel = min(N_SELECT, n_slc)
    slc_start = jnp.arange(n_slc) * SLC_LEN
    overlap = ((cmp_start[:, None] < slc_start[None, :] + SLC_LEN)
               & (cmp_start[:, None] + CMP_LEN > slc_start[None, :])).astype(jnp.float32)
    importance = jnp.einsum('bgrsc,cj->bgsj', p_cmp, overlap)
    blk = jnp.arange(n_slc)[None, :]
    cur = (pos // SLC_LEN)[:, None]
    forced = (blk == 0) | (blk == cur) | (blk == cur - 1)
    score = jnp.where(forced, jnp.inf, jnp.where(blk <= cur, importance, -jnp.inf))
    _, sel = lax.top_k(score, n_sel)

    ksb = jnp.swapaxes(ks, 1, 2).reshape(b, g, n_slc, SLC_LEN, d)
    vsb = jnp.swapaxes(vs, 1, 2).reshape(b, g, n_slc, SLC_LEN, d)
    bias_g = jnp.transpose(bias_heads.reshape(N_BUCKETS, g, r), (1, 0, 2))
    bi = jnp.arange(b)[:, None, None, None]
    gi = jnp.arange(g)[None, :, None, None]
    offs = jnp.arange(SLC_LEN)

    def chunk(args):
        qc, selc, posc = args
        qcn = qc.shape[1]
        tlen = n_sel * SLC_LEN
        kg = ksb[bi, gi, selc]
        vg = vsb[bi, gi, selc]
        tok = selc[..., None] * SLC_LEN + offs
        dist = posc[None, None, :, None, None] - tok
        bias = jnp.moveaxis(bias_g[gi[..., None], _t5_bucket(dist)], -1, 2)
        logits = jnp.einsum('bqgrd,bgqnld->bgrqnl', qc, kg,
                            preferred_element_type=jnp.float32) * scale + bias.astype(jnp.float32)
        p = _masked_softmax(logits.reshape(b, g, r, qcn, tlen),
                            (dist >= 0).reshape(b, g, 1, qcn, tlen))
        return jnp.einsum('bgrqt,bgqtd->bqgrd', p.astype(vg.dtype), vg.reshape(b, g, qcn, tlen, d))

    nc = s // SLC_Q_CHUNK
    q_chunks = jnp.moveaxis(q5.reshape(b, nc, SLC_Q_CHUNK, g, r, d), 1, 0)
    sel_chunks = jnp.moveaxis(sel.reshape(b, g, nc, SLC_Q_CHUNK, n_sel), 2, 0)
    pos_chunks = pos.reshape(nc, SLC_Q_CHUNK)
    o_slc = lax.map(chunk, (q_chunks, sel_chunks, pos_chunks))
    o_slc = jnp.moveaxis(o_slc, 0, 1).reshape(b, s, g, r, d)

    o_win = _banded_gqa(q, kw, vw, bias_heads).reshape(b, s, g, r, d)

    gates = jax.nn.sigmoid(gate_logits).reshape(b, s, g, r, N_BRANCH)
    out = gates[..., 0:1] * o_cmp + gates[..., 1:2] * o_slc + gates[..., 2:3] * o_win
    return out.reshape(b, s, h * d)


def _stick_breaking(q, k, v):
    b, s, h, d = q.shape
    nb = s // Q_BLOCK
    kpos = jnp.arange(s)
    scale = HEAD_DIM ** -0.5

    def block(args):
        qb, qpos = args
        z = jnp.einsum('bqhd,bkhd->bhqk', qb, k, preferred_element_type=jnp.float32) * scale
        before = kpos[None, :] < qpos[:, None]
        log_keep = jnp.where(before, jax.nn.log_sigmoid(-z), 0.0)
        shifted = jnp.pad(log_keep[..., 1:], ((0, 0), (0, 0), (0, 0), (0, 1)))
        suffix = lax.cumsum(shifted, axis=3, reverse=True)
        a = jnp.where(before, jnp.exp(jax.nn.log_sigmoid(z) + suffix), 0.0)
        return jnp.einsum('bhqk,bkhd->bqhd', a.astype(v.dtype), v)

    q_blocks = jnp.swapaxes(q.reshape(b, nb, Q_BLOCK, h, d), 0, 1)
    out = lax.map(block, (q_blocks, kpos.reshape(nb, Q_BLOCK)))
    return jnp.swapaxes(out, 0, 1).reshape(b, s, h * d)


def _moe_swiglu(h2, xres, router, w_gate, w_up, w_down):
    t = h2
    probs = jax.nn.softmax((t @ router).astype(jnp.float32), axis=-1)
    top_p, top_i = lax.top_k(probs, TOP_K)
    top_p = top_p / jnp.sum(top_p, axis=-1, keepdims=True)
    gate = jnp.sum(jax.nn.one_hot(top_i, N_EXPERTS, dtype=jnp.float32) * top_p[..., None], axis=1)
    tb = t.astype(jnp.bfloat16)
    out = xres
    for e in range(N_EXPERTS):
        hid = _swiglu_up(tb, w_gate[e].astype(jnp.bfloat16), w_up[e].astype(jnp.bfloat16),
                         bm=1024, bn=128, name="moe_up")
        y = _matmul(hid, w_down[e].astype(jnp.bfloat16), bm=512, bn=512,
                    out_dtype=jnp.float32, name="moe_down")
        out = out + gate[:, e:e + 1] * y
    return out


def kernel(x, attn_norm_g, ffn_norm_g, final_norm_g, w_in, w_out, mix_norm_g, rel_bias,
           swa_sinks, conv_dw, conv_dw_b, conv_ln_g, conv_ln_b, conv_pw, nsa_cmp_pe,
           nsa_cmp_w1, nsa_cmp_w2, ffn_w_gate, ffn_w_up, ffn_w_down, moe_router,
           moe_w_gate, moe_w_up, moe_w_down):
    b, s, _ = x.shape
    t = b * s
    bf = jnp.bfloat16

    def heads(a):
        return a.reshape(b, s, -1, HEAD_DIM)

    bias_a = rel_bias[:, :SWA_HEADS]
    bias_c = rel_bias[:, SWA_HEADS:]
    x2 = x.reshape(t, D_MODEL)
    for layer in range(DEPTH):
        h = _rmsnorm(x2, attn_norm_g[layer]).astype(bf)
        n_in = w_in.shape[-1]
        n_pad = (-n_in) % 512
        w_in_l = jnp.pad(w_in[layer].astype(bf), ((0, 0), (0, n_pad)))
        proj = _matmul(h, w_in_l, bm=1024, bn=512, out_dtype=jnp.float32, name="in_proj")
        proj = proj[:, :n_in].reshape(b, s, n_in)
        (a_q, a_k, a_v, b_val, b_gate, c_q, c_kc, c_vc, c_ks, c_vs, c_kw, c_vw, c_g,
         d_q, d_k, d_v) = jnp.split(proj, IN_SPLITS, axis=-1)
        o_a = _banded_gqa(heads(a_q), heads(a_k), heads(a_v), bias_a,
                          swa_sinks[layer]).reshape(b, s, GROUP_WIDTH)
        o_b = _conformer_conv(b_val, b_gate, conv_dw[layer], conv_dw_b[layer],
                              conv_ln_g[layer], conv_ln_b[layer], conv_pw[layer])
        o_c = _nsa(heads(c_q), heads(c_kc), heads(c_vc), heads(c_ks), heads(c_vs),
                   heads(c_kw), heads(c_vw), c_g, nsa_cmp_pe[layer], nsa_cmp_w1[layer],
                   nsa_cmp_w2[layer], bias_c)
        o_d = _stick_breaking(heads(d_q), heads(d_k), heads(d_v))
        mixed = jnp.stack([o_a, o_b, o_c, o_d], axis=2)
        mixed = _rmsnorm(mixed, mix_norm_g[layer].reshape(N_MIXERS, GROUP_WIDTH))
        mixed = mixed.reshape(t, D_MODEL).astype(bf)
        x2 = _matmul(mixed, w_out[layer].astype(bf), bm=1024, bn=512,
                     out_dtype=jnp.float32, residual=x2, name="out_proj")
        hf = _rmsnorm(x2, ffn_norm_g[layer])
        i = layer // 2
        if layer % 2 == 0:
            hid = _swiglu_up(hf.astype(bf), ffn_w_gate[i].astype(bf), ffn_w_up[i].astype(bf),
                             bm=1024, bn=256, name="ffn_up")
            x2 = _matmul_ksplit_res(hid, ffn_w_down[i].astype(bf), x2,
                                    bm=1024, bn=512, bk=D_FF // 2, name="ffn_down")
        else:
            x2 = _moe_swiglu(hf, x2, moe_router[i], moe_w_gate[i], moe_w_up[i], moe_w_down[i])
    return _rmsnorm(x2, final_norm_g).reshape(b, s, D_MODEL)
```

```python
import functools
import math

import jax
import jax.numpy as jnp
import numpy as np
from jax import lax
from jax.experimental import pallas as pl
from jax.experimental.pallas import tpu as pltpu

D_MODEL = 4096
DEPTH = 2
HEAD_DIM = 128
N_MIXERS = 4
GROUP_WIDTH = D_MODEL // N_MIXERS
GROUP_HEADS = GROUP_WIDTH // HEAD_DIM
KV_HEADS = 2
REP = GROUP_HEADS // KV_HEADS
WINDOW = 128
Q_BLOCK = 128
CONV_WIDTH = 31
CMP_LEN = 32
CMP_STRIDE = 16
SLC_LEN = 64
N_SELECT = 16
N_BRANCH = 3
N_BUCKETS = 32
MAX_DISTANCE = 128
D_FF = 11008
N_EXPERTS = 8
TOP_K = 2
D_EXPERT = D_FF // 2
NORM_EPS = 1e-6
SCALE = HEAD_DIM ** -0.5

VMEM_LIMIT_BYTES = 56 * 1024 * 1024
LANES = 128

F32 = jnp.float32
BF16 = jnp.bfloat16
NEG_INF = float("-inf")
F32_TINY = float(np.finfo(np.float32).tiny)

PA_AQ, PA_AK, PA_AV = 0, 1024, 1280
PA_CQ, PA_CKS, PA_CVS, PA_CKW, PA_CVW = 1536, 2560, 2816, 3072, 3328
PA_DQ, PA_DK, PA_DV = 3584, 4608, 5632
PA_COLS = 6656
PF_VAL, PF_GATE, PF_KC, PF_VC, PF_G = 0, 1024, 2048, 2304, 2560
PF_COLS = 2688


def _params(*sem):
    return pltpu.CompilerParams(dimension_semantics=sem, vmem_limit_bytes=VMEM_LIMIT_BYTES)


def _bucket_thresholds():
    n = np.arange(0, 4 * MAX_DISTANCE)
    max_exact = N_BUCKETS // 2
    nf = np.maximum(n, 1).astype(np.float32)
    large = max_exact + (np.log(nf / max_exact) / math.log(MAX_DISTANCE / max_exact)
                         * (N_BUCKETS - max_exact)).astype(np.int32)
    large = np.minimum(large, N_BUCKETS - 1)
    bucket = np.where(n < max_exact, n, large)
    return [int(np.argmax(bucket >= k)) for k in range(N_BUCKETS)]


BUCKET_THRESHOLDS = _bucket_thresholds()


def _mm_kernel(x_ref, w_ref, o_ref):
    o_ref[...] = jnp.dot(x_ref[...], w_ref[...], preferred_element_type=F32).astype(o_ref.dtype)


def _mm_res_kernel(x_ref, w_ref, r_ref, o_ref):
    acc = jnp.dot(x_ref[...], w_ref[...], preferred_element_type=F32)
    o_ref[...] = (r_ref[...] + acc).astype(o_ref.dtype)


def _mm_res_ksplit_kernel(x_ref, w_ref, r_ref, o_ref):
    @pl.when(pl.program_id(2) == 0)
    def _():
        o_ref[...] = r_ref[...]

    o_ref[...] += jnp.dot(x_ref[...], w_ref[...], preferred_element_type=F32)


def _matmul(x, w, *, bm, bn, out_dtype, residual=None, name="matmul"):
    m, k = x.shape
    _, n = w.shape
    assert m % bm == 0 and n % bn == 0
    in_specs = [pl.BlockSpec((bm, k), lambda i, j: (i, 0)),
                pl.BlockSpec((k, bn), lambda i, j: (0, j))]
    args = [x, w]
    body = _mm_kernel
    if residual is not None:
        in_specs.append(pl.BlockSpec((bm, bn), lambda i, j: (i, j)))
        args.append(residual)
        body = _mm_res_kernel
    return pl.pallas_call(
        body,
        out_shape=jax.ShapeDtypeStruct((m, n), out_dtype),
        grid=(m // bm, n // bn),
        in_specs=in_specs,
        out_specs=pl.BlockSpec((bm, bn), lambda i, j: (i, j)),
        compiler_params=_params("parallel", "arbitrary"),
        name=name,
    )(*args)


def _matmul_ksplit_res(x, w, residual, *, bm, bn, bk, name="matmul_ksplit"):
    m, k = x.shape
    _, n = w.shape
    assert m % bm == 0 and n % bn == 0 and k % bk == 0
    return pl.pallas_call(
        _mm_res_ksplit_kernel,
        out_shape=jax.ShapeDtypeStruct((m, n), F32),
        grid=(m // bm, n // bn, k // bk),
        in_specs=[pl.BlockSpec((bm, bk), lambda i, j, kk: (i, kk)),
                  pl.BlockSpec((bk, bn), lambda i, j, kk: (kk, j)),
                  pl.BlockSpec((bm, bn), lambda i, j, kk: (i, j))],
        out_specs=pl.BlockSpec((bm, bn), lambda i, j, kk: (i, j)),
        compiler_params=_params("parallel", "arbitrary", "arbitrary"),
        name=name,
    )(x, w, residual)


def _swiglu_kernel(x_ref, wg_ref, wu_ref, o_ref):
    x = x_ref[...]
    g = jnp.dot(x, wg_ref[...], preferred_element_type=F32)
    u = jnp.dot(x, wu_ref[...], preferred_element_type=F32)
    o_ref[...] = (g * jax.nn.sigmoid(g) * u).astype(o_ref.dtype)


def _swiglu_up(x, wg, wu, *, bm, bn, name="swiglu_up"):
    m, k = x.shape
    _, n = wg.shape
    assert m % bm == 0 and n % bn == 0
    return pl.pallas_call(
        _swiglu_kernel,
        out_shape=jax.ShapeDtypeStruct((m, n), BF16),
        grid=(m // bm, n // bn),
        in_specs=[pl.BlockSpec((bm, k), lambda i, j: (i, 0)),
                  pl.BlockSpec((k, bn), lambda i, j: (0, j)),
                  pl.BlockSpec((k, bn), lambda i, j: (0, j))],
        out_specs=pl.BlockSpec((bm, bn), lambda i, j: (i, j)),
        compiler_params=_params("parallel", "arbitrary"),
        name=name,
    )(x, wg, wu)


def _rmsnorm_kernel(x_ref, g_ref, o_ref):
    x = x_ref[...]
    y = x * lax.rsqrt(jnp.mean(x * x, axis=-1, keepdims=True) + NORM_EPS)
    o_ref[...] = (y * g_ref[...]).astype(o_ref.dtype)


def _rmsnorm(x, g, *, out_dtype, bm=256, name="rmsnorm"):
    m, d = x.shape
    return pl.pallas_call(
        _rmsnorm_kernel,
        out_shape=jax.ShapeDtypeStruct((m, d), out_dtype),
        grid=(m // bm,),
        in_specs=[pl.BlockSpec((bm, d), lambda i: (i, 0)),
                  pl.BlockSpec((1, d), lambda i: (0, 0))],
        out_specs=pl.BlockSpec((bm, d), lambda i: (i, 0)),
        compiler_params=_params("parallel"),
        name=name,
    )(x, g.reshape(1, d))


def _bias_of_dist(dist, tab_ref, head):
    out = jnp.full(dist.shape, tab_ref[0, head], F32)
    for k in range(1, N_BUCKETS):
        out = jnp.where(dist >= BUCKET_THRESHOLDS[k], tab_ref[k, head], out)
    return out


def _win_bias_kernel(tab_ref, o_ref):
    h = pl.program_id(0)
    qi = lax.broadcasted_iota(jnp.int32, (Q_BLOCK, 2 * Q_BLOCK), 0)
    kj = lax.broadcasted_iota(jnp.int32, (Q_BLOCK, 2 * Q_BLOCK), 1)
    o_ref[0] = _bias_of_dist(qi + Q_BLOCK - kj, tab_ref, h)


def _win_bias(rel_bias):
    nh = rel_bias.shape[1]
    return pl.pallas_call(
        _win_bias_kernel,
        out_shape=jax.ShapeDtypeStruct((nh, Q_BLOCK, 2 * Q_BLOCK), F32),
        grid=(nh,),
        in_specs=[pl.BlockSpec(memory_space=pltpu.SMEM)],
        out_specs=pl.BlockSpec((1, Q_BLOCK, 2 * Q_BLOCK), lambda h: (h, 0, 0)),
        compiler_params=_params("arbitrary"),
        name="win_bias",
    )(rel_bias)


def _cmp_bias_kernel(tab_ref, o_ref, *, head0, rows):
    h = pl.program_id(0) + head0
    n = pl.program_id(1)
    ncmp = o_ref.shape[2]
    t = n * rows + lax.broadcasted_iota(jnp.int32, (rows, ncmp), 0)
    c = lax.broadcasted_iota(jnp.int32, (rows, ncmp), 1)
    o_ref[0] = _bias_of_dist(t - c * CMP_STRIDE - (CMP_LEN - 1), tab_ref, h)


def _cmp_bias(rel_bias, seq, head0, rows=512):
    ncmp = seq // CMP_STRIDE
    rows = min(rows, seq)
    return pl.pallas_call(
        functools.partial(_cmp_bias_kernel, head0=head0, rows=rows),
        out_shape=jax.ShapeDtypeStruct((GROUP_HEADS, seq, ncmp), F32),
        grid=(GROUP_HEADS, seq // rows),
        in_specs=[pl.BlockSpec(memory_space=pltpu.SMEM)],
        out_specs=pl.BlockSpec((1, rows, ncmp), lambda h, n: (h, n, 0)),
        compiler_params=_params("arbitrary", "arbitrary"),
        name="cmp_bias",
    )(rel_bias)


def _stack_heads(q):
    return jnp.concatenate([q[:, r * HEAD_DIM:(r + 1) * HEAD_DIM] for r in range(REP)], axis=0)


def _store_heads(o_ref, o, rows):
    for r in range(REP):
        o_ref[:, r * HEAD_DIM:(r + 1) * HEAD_DIM] = o[r * rows:(r + 1) * rows].astype(o_ref.dtype)


def _qk(q, k):
    return lax.dot_general(q, k, (((1,), (1,)), ((), ())), preferred_element_type=F32)


def _window_kernel(q_ref, kp_ref, kc_ref, vp_ref, vc_ref, bias_ref, sink_ref, o_ref, *, has_sink):
    n = pl.program_id(2)
    q4 = _stack_heads(q_ref[...])
    kcat = jnp.concatenate([kp_ref[...], kc_ref[...]], axis=0)
    vcat = jnp.concatenate([vp_ref[...], vc_ref[...]], axis=0)
    s = _qk(q4, kcat).reshape(REP, Q_BLOCK, 2 * Q_BLOCK) * SCALE + bias_ref[...]
    qi = lax.broadcasted_iota(jnp.int32, (Q_BLOCK, 2 * Q_BLOCK), 0)
    kj = lax.broadcasted_iota(jnp.int32, (Q_BLOCK, 2 * Q_BLOCK), 1)
    dist = qi + Q_BLOCK - kj
    mask = ((dist >= 0) & (dist < WINDOW) & ((n > 0) | (kj >= Q_BLOCK)))[None]
    s = jnp.where(mask, s, NEG_INF)
    m = jnp.max(s, axis=-1, keepdims=True)
    if has_sink:
        sink = sink_ref[...]
        m = jnp.maximum(m, sink)
    p = jnp.where(mask, jnp.exp(s - m), 0.0)
    den = jnp.sum(p, axis=-1, keepdims=True)
    if has_sink:
        den = den + jnp.exp(sink - m)
    p = p / jnp.maximum(den, F32_TINY)
    o = jnp.dot(p.reshape(REP * Q_BLOCK, 2 * Q_BLOCK).astype(BF16), vcat, preferred_element_type=F32)
    _store_heads(o_ref, o, Q_BLOCK)


def _window_attention(pa, batch, seq, q_col, k_col, v_col, bias, bias_head0, sinks, name):
    nb = seq // Q_BLOCK
    qw = REP * HEAD_DIM
    has_sink = sinks is not None
    sink_arr = (sinks if has_sink else jnp.zeros((GROUP_HEADS,), F32)).reshape(GROUP_HEADS, 1, 1)

    def cur(col):
        return lambda b, g, n: (b * nb + n, col // HEAD_DIM + g)

    def prev(col):
        return lambda b, g, n: (b * nb + jnp.maximum(n - 1, 0), col // HEAD_DIM + g)

    kv_block = (Q_BLOCK, HEAD_DIM)
    return pl.pallas_call(
        functools.partial(_window_kernel, has_sink=has_sink),
        out_shape=jax.ShapeDtypeStruct((batch * seq, GROUP_WIDTH), F32),
        grid=(batch, KV_HEADS, nb),
        in_specs=[pl.BlockSpec((Q_BLOCK, qw), lambda b, g, n: (b * nb + n, q_col // qw + g)),
                  pl.BlockSpec(kv_block, prev(k_col)), pl.BlockSpec(kv_block, cur(k_col)),
                  pl.BlockSpec(kv_block, prev(v_col)), pl.BlockSpec(kv_block, cur(v_col)),
                  pl.BlockSpec((REP, Q_BLOCK, 2 * Q_BLOCK),
                               lambda b, g, n: (bias_head0 // REP + g, 0, 0)),
                  pl.BlockSpec((REP, 1, 1), lambda b, g, n: (g, 0, 0))],
        out_specs=pl.BlockSpec((Q_BLOCK, qw), lambda b, g, n: (b * nb + n, g)),
        compiler_params=_params("parallel", "parallel", "arbitrary"),
        name=name,
    )(pa, pa, pa, pa, pa, bias, sink_arr)


def _compress_kernel(t_ref, pe_ref, w1_ref, w2_ref, o_ref):
    ncmp = o_ref.shape[3]
    half = CMP_LEN // 2
    pe = pe_ref[0]
    rows = [t_ref[pl.ds(r, ncmp, stride=CMP_STRIDE), :] for r in range(CMP_STRIDE)]
    xa = jnp.concatenate([rows[r] + pe[r:r + 1] for r in range(half)], axis=1).astype(BF16)
    xb = jnp.concatenate([rows[r] + pe[half + r:half + r + 1] for r in range(half)], axis=1).astype(BF16)
    kw = half * HEAD_DIM
    p0 = jnp.dot(xa, w1_ref[0, :kw, :], preferred_element_type=F32)
    p1 = jnp.dot(xb, w1_ref[0, kw:, :], preferred_element_type=F32)
    pre = p0 + pltpu.roll(p1, ncmp - 1, 0)
    hid = pre * jax.nn.sigmoid(pre)
    o_ref[0, 0, 0] = jnp.dot(hid.astype(BF16), w2_ref[0], preferred_element_type=F32).astype(o_ref.dtype)


def _compress(pf, batch, seq, pe, w1, w2):
    ncmp = seq // CMP_STRIDE
    return pl.pallas_call(
        _compress_kernel,
        out_shape=jax.ShapeDtypeStruct((2, batch, KV_HEADS, ncmp, HEAD_DIM), BF16),
        grid=(2, batch, KV_HEADS),
        in_specs=[pl.BlockSpec((seq, HEAD_DIM), lambda kv, b, g: (b, PF_KC // HEAD_DIM + KV_HEADS * kv + g)),
                  pl.BlockSpec((1, CMP_LEN, HEAD_DIM), lambda kv, b, g: (kv, 0, 0)),
                  pl.BlockSpec((1, CMP_LEN * HEAD_DIM, HEAD_DIM), lambda kv, b, g: (kv, 0, 0)),
                  pl.BlockSpec((1, HEAD_DIM, HEAD_DIM), lambda kv, b, g: (kv, 0, 0))],
        out_specs=pl.BlockSpec((1, 1, 1, ncmp, HEAD_DIM), lambda kv, b, g: (kv, b, g, 0, 0)),
        compiler_params=_params("arbitrary", "arbitrary", "arbitrary"),
        name="nsa_compress",
    )(pf, pe, w1.astype(BF16), w2.astype(BF16))


def _cmp_attn_kernel(q_ref, kc_ref, vc_ref, bias_ref, ov_ref, o_ref, sel_ref, *, n_sel):
    n = pl.program_id(2)
    ncmp = kc_ref.shape[3]
    nslc = ov_ref.shape[1]
    q4 = _stack_heads(q_ref[...])
    s = _qk(q4, kc_ref[0, 0, 0]).reshape(REP, Q_BLOCK, ncmp) * SCALE + bias_ref[...]
    t = n * Q_BLOCK + lax.broadcasted_iota(jnp.int32, (Q_BLOCK, ncmp), 0)
    c = lax.broadcasted_iota(jnp.int32, (Q_BLOCK, ncmp), 1)
    vis = (t - c * CMP_STRIDE - (CMP_LEN - 1) >= 0)[None]
    s = jnp.where(vis, s, NEG_INF)
    m = jnp.max(s, axis=-1, keepdims=True)
    m = jnp.where(m == NEG_INF, 0.0, m)
    p = jnp.where(vis, jnp.exp(s - m), 0.0)
    den = jnp.sum(p, axis=-1, keepdims=True)
    p = p / jnp.maximum(den, F32_TINY)
    o = jnp.dot(p.reshape(REP * Q_BLOCK, ncmp).astype(BF16), vc_ref[0, 0, 0], preferred_element_type=F32)
    _store_heads(o_ref, o, Q_BLOCK)

    psum = p[0]
    for r in range(1, REP):
        psum = psum + p[r]
    hi = psum.astype(BF16)
    lo = (psum - hi.astype(F32)).astype(BF16)
    ov = ov_ref[...]
    imp = jnp.dot(hi, ov, preferred_element_type=F32) + jnp.dot(lo, ov, preferred_element_type=F32)
    tq = n * Q_BLOCK + lax.broadcasted_iota(jnp.int32, (Q_BLOCK, nslc), 0)
    blk = lax.broadcasted_iota(jnp.int32, (Q_BLOCK, nslc), 1)
    cur = tq // SLC_LEN
    forced = (blk == 0) | (blk == cur) | (blk == cur - 1)
    score = jnp.where(forced, jnp.inf, jnp.where(blk <= cur, imp, NEG_INF))
    rank = jnp.zeros((Q_BLOCK, nslc), F32)
    for k in range(nslc):
        col = score[:, k:k + 1]
        ahead = (col > score) | ((col == score) & (blk > k))
        rank = rank + jnp.where(ahead, 1.0, 0.0)
    sel_ref[0, 0] = jnp.where(rank < n_sel, 1.0, 0.0)


def _overlap_matrix(ncmp, nslc):
    cs = np.arange(ncmp)[:, None] * CMP_STRIDE
    ss = np.arange(nslc)[None, :] * SLC_LEN
    return ((cs < ss + SLC_LEN) & (cs + CMP_LEN > ss)).astype(np.float32)


def _cmp_attention(pa, cmp_kv, cbias, batch, seq):
    nb = seq // Q_BLOCK
    ncmp = seq // CMP_STRIDE
    nslc = seq // SLC_LEN
    qw = REP * HEAD_DIM
    ov = jnp.asarray(_overlap_matrix(ncmp, nslc), BF16)
    return pl.pallas_call(
        functools.partial(_cmp_attn_kernel, n_sel=min(N_SELECT, nslc)),
        out_shape=(jax.ShapeDtypeStruct((batch * seq, GROUP_WIDTH), F32),
                   jax.ShapeDtypeStruct((batch, KV_HEADS, seq, nslc), F32)),
        grid=(batch, KV_HEADS, nb),
        in_specs=[pl.BlockSpec((Q_BLOCK, qw), lambda b, g, n: (b * nb + n, PA_CQ // qw + g)),
                  pl.BlockSpec((1, 1, 1, ncmp, HEAD_DIM), lambda b, g, n: (0, b, g, 0, 0)),
                  pl.BlockSpec((1, 1, 1, ncmp, HEAD_DIM), lambda b, g, n: (1, b, g, 0, 0)),
                  pl.BlockSpec((REP, Q_BLOCK, ncmp), lambda b, g, n: (g, n, 0)),
                  pl.BlockSpec((ncmp, nslc), lambda b, g, n: (0, 0))],
        out_specs=(pl.BlockSpec((Q_BLOCK, qw), lambda b, g, n: (b * nb + n, g)),
                   pl.BlockSpec((1, 1, Q_BLOCK, nslc), lambda b, g, n: (b, g, n, 0))),
        compiler_params=_params("parallel", "parallel", "arbitrary"),
        name="nsa_cmp_attn",
    )(pa, cmp_kv, cmp_kv, cbias, ov)


FAR_TILE = 2 * Q_BLOCK


def _slc_attn_kernel(q_ref, k_ref, v_ref, sel_ref, bias_ref, far_bias_ref, efar_ref, enear_ref, o_ref):
    n = pl.program_id(2)
    q4 = _stack_heads(q_ref[...])
    selb = sel_ref[0, 0].astype(BF16)
    far_bias = far_bias_ref[...]
    rows = REP * Q_BLOCK

    def online(carry, s, mask, v):
        m, l, acc = carry
        s = jnp.where(mask[None], s, NEG_INF)
        m_new = jnp.maximum(m, jnp.max(s, axis=-1, keepdims=True))
        m_safe = jnp.where(m_new == NEG_INF, 0.0, m_new)
        alpha = jnp.exp(m - m_safe)
        p = jnp.exp(s - m_safe)
        l = alpha * l + jnp.sum(p, axis=-1, keepdims=True)
        pv = jnp.dot(p.reshape(rows, p.shape[-1]).astype(BF16), v, preferred_element_type=F32)
        acc = alpha * acc + pv.reshape(REP, Q_BLOCK, HEAD_DIM)
        return m_new, l, acc

    far_end = (n - 1) * Q_BLOCK

    def far_step(j, carry):
        k0 = pl.multiple_of(j * FAR_TILE, FAR_TILE)
        k = k_ref[pl.ds(k0, FAR_TILE), :]
        v = v_ref[pl.ds(k0, FAR_TILE), :]
        s = _qk(q4, k).reshape(REP, Q_BLOCK, FAR_TILE) * SCALE + far_bias
        picked = jnp.dot(selb, efar_ref[j], preferred_element_type=F32) > 0.5
        kidx = k0 + lax.broadcasted_iota(jnp.int32, (Q_BLOCK, FAR_TILE), 1)
        return online(carry, s, picked & (kidx < far_end), v)

    init = (jnp.full((REP, Q_BLOCK, 1), NEG_INF, F32), jnp.zeros((REP, Q_BLOCK, 1), F32),
            jnp.zeros((REP, Q_BLOCK, HEAD_DIM), F32))
    carry = lax.fori_loop(0, n // 2, far_step, init)

    pb = jnp.maximum(n - 1, 0)
    p0 = pl.multiple_of(pb * Q_BLOCK, Q_BLOCK)
    c0 = pl.multiple_of(n * Q_BLOCK, Q_BLOCK)
    kcat = jnp.concatenate([k_ref[pl.ds(p0, Q_BLOCK), :], k_ref[pl.ds(c0, Q_BLOCK), :]], axis=0)
    vcat = jnp.concatenate([v_ref[pl.ds(p0, Q_BLOCK), :], v_ref[pl.ds(c0, Q_BLOCK), :]], axis=0)
    s = _qk(q4, kcat).reshape(REP, Q_BLOCK, 2 * Q_BLOCK) * SCALE + bias_ref[...]
    picked = jnp.concatenate(
        [jnp.dot(selb, enear_ref[pb], preferred_element_type=F32),
         jnp.dot(selb, enear_ref[n], preferred_element_type=F32)], axis=1) > 0.5
    qi = lax.broadcasted_iota(jnp.int32, (Q_BLOCK, 2 * Q_BLOCK), 0)
    kj = lax.broadcasted_iota(jnp.int32, (Q_BLOCK, 2 * Q_BLOCK), 1)
    causal = (qi + Q_BLOCK - kj >= 0) & ((n > 0) | (kj >= Q_BLOCK))
    m, l, acc = online(carry, s, picked & causal, vcat)
    o = acc / jnp.maximum(l, F32_TINY)
    for r in range(REP):
        o_ref[:, r * HEAD_DIM:(r + 1) * HEAD_DIM] = o[r].astype(o_ref.dtype)


def _expand_matrix(seq, tile):
    nslc = seq // SLC_LEN
    key = np.arange(seq).reshape(seq // tile, 1, tile)
    return (key // SLC_LEN == np.arange(nslc)[None, :, None]).astype(np.float32)


def _slc_attention(pa, sel, wbias, rel_bias_c, batch, seq):
    nb = seq // Q_BLOCK
    nslc = seq // SLC_LEN
    qw = REP * HEAD_DIM
    nfar = max(seq // FAR_TILE, 1)
    efar = jnp.asarray(_expand_matrix(max(seq, FAR_TILE), FAR_TILE)[:, :nslc], BF16)
    enear = jnp.asarray(_expand_matrix(seq, Q_BLOCK), BF16)
    far_bias = rel_bias_c[N_BUCKETS - 1].reshape(GROUP_HEADS, 1, 1)
    return pl.pallas_call(
        _slc_attn_kernel,
        out_shape=jax.ShapeDtypeStruct((batch * seq, GROUP_WIDTH), F32),
        grid=(batch, KV_HEADS, nb),
        in_specs=[pl.BlockSpec((Q_BLOCK, qw), lambda b, g, n: (b * nb + n, PA_CQ // qw + g)),
                  pl.BlockSpec((seq, HEAD_DIM), lambda b, g, n: (b, PA_CKS // HEAD_DIM + g)),
                  pl.BlockSpec((seq, HEAD_DIM), lambda b, g, n: (b, PA_CVS // HEAD_DIM + g)),
                  pl.BlockSpec((1, 1, Q_BLOCK, nslc), lambda b, g, n: (b, g, n, 0)),
                  pl.BlockSpec((REP, Q_BLOCK, 2 * Q_BLOCK),
                               lambda b, g, n: (GROUP_HEADS // REP + g, 0, 0)),
                  pl.BlockSpec((REP, 1, 1), lambda b, g, n: (g, 0, 0)),
                  pl.BlockSpec((nfar, nslc, FAR_TILE), lambda b, g, n: (0, 0, 0)),
                  pl.BlockSpec((nb, nslc, Q_BLOCK), lambda b, g, n: (0, 0, 0))],
        out_specs=pl.BlockSpec((Q_BLOCK, qw), lambda b, g, n: (b * nb + n, g)),
        compiler_params=_params("parallel", "parallel", "arbitrary"),
        name="nsa_slc_attn",
    )(pa, pa, pa, sel, wbias, far_bias, efar, enear)


SB_TILE = 256


def _sb_kernel(q_ref, k_ref, v_ref, u_ref, o_ref):
    n = pl.program_id(2)
    q = q_ref[...]
    u = u_ref[...]
    qi = lax.broadcasted_iota(jnp.int32, (SB_TILE, SB_TILE), 0)
    kj = lax.broadcasted_iota(jnp.int32, (SB_TILE, SB_TILE), 1)

    def tile(j, carry, diagonal):
        later, acc = carry
        k0 = pl.multiple_of(j * SB_TILE, SB_TILE)
        k = k_ref[pl.ds(k0, SB_TILE), :]
        v = v_ref[pl.ds(k0, SB_TILE), :]
        z = _qk(q, k) * SCALE
        t = jnp.log1p(jnp.exp(-jnp.abs(z)))
        log_beta = jnp.minimum(z, 0.0) - t
        log_keep = jnp.minimum(-z, 0.0) - t
        if diagonal:
            before = kj < qi
            log_keep = jnp.where(before, log_keep, 0.0)
        hi = log_keep.astype(BF16)
        lo = (log_keep - hi.astype(F32)).astype(BF16)
        suffix = jnp.dot(hi, u, preferred_element_type=F32) + jnp.dot(lo, u, preferred_element_type=F32)
        a = jnp.exp(log_beta + suffix + later)
        if diagonal:
            a = jnp.where(before, a, 0.0)
        acc = acc + jnp.dot(a.astype(BF16), v, preferred_element_type=F32)
        later = later + suffix[:, 0:1] + log_keep[:, 0:1]
        return later, acc

    carry = (jnp.zeros((SB_TILE, 1), F32), jnp.zeros((SB_TILE, HEAD_DIM), F32))
    carry = tile(n, carry, True)
    carry = lax.fori_loop(0, n, lambda i, c: tile(n - 1 - i, c, False), carry)
    o_ref[...] = carry[1].astype(o_ref.dtype)


def _stick_breaking(pa, batch, seq):
    tile = min(SB_TILE, seq)
    assert tile == SB_TILE
    nb = seq // SB_TILE
    u = jnp.asarray(np.tril(np.ones((SB_TILE, SB_TILE), np.float32), -1), BF16)
    return pl.pallas_call(
        _sb_kernel,
        out_shape=jax.ShapeDtypeStruct((batch * seq, GROUP_WIDTH), F32),
        grid=(batch, GROUP_HEADS, nb),
        in_specs=[pl.BlockSpec((SB_TILE, HEAD_DIM), lambda b, h, n: (b * nb + n, PA_DQ // HEAD_DIM + h)),
                  pl.BlockSpec((seq, HEAD_DIM), lambda b, h, n: (b, PA_DK // HEAD_DIM + h)),
                  pl.BlockSpec((seq, HEAD_DIM), lambda b, h, n: (b, PA_DV // HEAD_DIM + h)),
                  pl.BlockSpec((SB_TILE, SB_TILE), lambda b, h, n: (0, 0))],
        out_specs=pl.BlockSpec((SB_TILE, HEAD_DIM), lambda b, h, n: (b * nb + n, h)),
        compiler_params=_params("parallel", "parallel", "arbitrary"),
        name="stick_breaking",
    )(pa, pa, pa, u)


CONV_TILE = 256
CONV_HALO = 32


def _conv_kernel(val_ref, gate_ref, hval_ref, hgate_ref, dw_ref, dwb_ref, lng_ref, lnb_ref, pw_ref,
                 o_ref, ext_ref):
    n = pl.program_id(1)
    halo = hval_ref[...] * jax.nn.sigmoid(hgate_ref[...])
    ext_ref[0:CONV_HALO, :] = jnp.where(n > 0, halo, 0.0)
    ext_ref[CONV_HALO:, :] = val_ref[...] * jax.nn.sigmoid(gate_ref[...])
    first = CONV_HALO - (CONV_WIDTH - 1)
    acc = jnp.zeros((CONV_TILE, GROUP_WIDTH), F32) + dwb_ref[...]
    for w in range(CONV_WIDTH):
        acc = acc + ext_ref[first + w:first + w + CONV_TILE, :] * dw_ref[w:w + 1, :]
    mu = jnp.mean(acc, axis=-1, keepdims=True)
    cen = acc - mu
    var = jnp.mean(cen * cen, axis=-1, keepdims=True)
    y = cen * lax.rsqrt(var + NORM_EPS) * lng_ref[...] + lnb_ref[...]
    y = y * jax.nn.sigmoid(y)
    o_ref[...] = jnp.dot(y.astype(BF16), pw_ref[...], preferred_element_type=F32).astype(o_ref.dtype)


def _conformer_conv(pf, batch, seq, dw, dw_b, ln_g, ln_b, pw):
    nt = seq // CONV_TILE
    hpt = CONV_TILE // CONV_HALO
    c = GROUP_WIDTH
    vec = pl.BlockSpec((1, c), lambda b, n: (0, 0))

    def halo(col):
        return lambda b, n: (jnp.maximum((b * nt + n) * hpt - 1, 0), col // c)

    return pl.pallas_call(
        _conv_kernel,
        out_shape=jax.ShapeDtypeStruct((batch * seq, c), F32),
        grid=(batch, nt),
        in_specs=[pl.BlockSpec((CONV_TILE, c), lambda b, n: (b * nt + n, PF_VAL // c)),
                  pl.BlockSpec((CONV_TILE, c), lambda b, n: (b * nt + n, PF_GATE // c)),
                  pl.BlockSpec((CONV_HALO, c), halo(PF_VAL)),
                  pl.BlockSpec((CONV_HALO, c), halo(PF_GATE)),
                  pl.BlockSpec((CONV_WIDTH, c), lambda b, n: (0, 0)),
                  vec, vec, vec,
                  pl.BlockSpec((c, c), lambda b, n: (0, 0))],
        out_specs=pl.BlockSpec((CONV_TILE, c), lambda b, n: (b * nt + n, 0)),
        scratch_shapes=[pltpu.VMEM((CONV_HALO + CONV_TILE, c), F32)],
        compiler_params=_params("parallel", "arbitrary"),
        name="conformer_conv",
    )(pf, pf, pf, pf, dw, dw_b.reshape(1, c), ln_g.reshape(1, c), ln_b.reshape(1, c), pw.astype(BF16))


MIX_TILE = 256


def _mix_kernel(oa_ref, ob_ref, ocmp_ref, oslc_ref, owin_ref, gl_ref, od_ref, g_ref, o_ref):
    gates = jax.nn.sigmoid(gl_ref[...])

    def norm_store(x, grp):
        y = x * lax.rsqrt(jnp.mean(x * x, axis=-1, keepdims=True) + NORM_EPS)
        sl = slice(grp * GROUP_WIDTH, (grp + 1) * GROUP_WIDTH)
        o_ref[:, sl] = (y * g_ref[:, sl]).astype(o_ref.dtype)

    norm_store(oa_ref[...], 0)
    norm_store(ob_ref[...], 1)
    heads = []
    for h in range(GROUP_HEADS):
        sl = slice(h * HEAD_DIM, (h + 1) * HEAD_DIM)
        c0 = h * N_BRANCH
        heads.append(gates[:, c0:c0 + 1] * ocmp_ref[:, sl] + gates[:, c0 + 1:c0 + 2] * oslc_ref[:, sl]
                     + gates[:, c0 + 2:c0 + 3] * owin_ref[:, sl])
    norm_store(jnp.concatenate(heads, axis=1), 2)
    norm_store(od_ref[...], 3)


def _mix(o_a, o_b, o_cmp, o_slc, o_win, pf, o_d, g):
    t = o_a.shape[0]
    grp = pl.BlockSpec((MIX_TILE, GROUP_WIDTH), lambda i: (i, 0))
    return pl.pallas_call(
        _mix_kernel,
        out_shape=jax.ShapeDtypeStruct((t, D_MODEL), BF16),
        grid=(t // MIX_TILE,),
        in_specs=[grp, grp, grp, grp, grp,
                  pl.BlockSpec((MIX_TILE, LANES), lambda i: (i, PF_G // LANES)),
                  grp,
                  pl.BlockSpec((1, D_MODEL), lambda i: (0, 0))],
        out_specs=pl.BlockSpec((MIX_TILE, D_MODEL), lambda i: (i, 0)),
        compiler_params=_params("parallel"),
        name="mix_norm",
    )(o_a, o_b, o_cmp, o_slc, o_win, pf, o_d, g.reshape(1, D_MODEL))


def _moe_swiglu(h2, xres, router, w_gate, w_up, w_down):
    probs = jax.nn.softmax((h2 @ router).astype(F32), axis=-1)
    top_p, top_i = lax.top_k(probs, TOP_K)
    top_p = top_p / jnp.sum(top_p, axis=-1, keepdims=True)
    gate = jnp.sum(jax.nn.one_hot(top_i, N_EXPERTS, dtype=F32) * top_p[..., None], axis=1)
    tb = h2.astype(BF16)
    out = xres
    for e in range(N_EXPERTS):
        hid = _swiglu_up(tb, w_gate[e].astype(BF16), w_up[e].astype(BF16), bm=1024, bn=128, name="moe_up")
        y = _matmul(hid, w_down[e].astype(BF16), bm=512, bn=512, out_dtype=F32, name="moe_down")
        out = out + gate[:, e:e + 1] * y
    return out


def _split_in_weights(w):
    sizes = (1024, 256, 256, 1024, 1024, 1024, 256, 256, 256, 256, 256, 256, 24, 1024, 1024, 1024)
    offs = np.concatenate([[0], np.cumsum(sizes)])
    (a_q, a_k, a_v, b_val, b_gate, c_q, c_kc, c_vc, c_ks, c_vs, c_kw, c_vw, c_g,
     d_q, d_k, d_v) = [w[:, int(offs[i]):int(offs[i + 1])] for i in range(len(sizes))]
    wa = jnp.concatenate([a_q, a_k, a_v, c_q, c_ks, c_vs, c_kw, c_vw, d_q, d_k, d_v], axis=1)
    pad = jnp.zeros((w.shape[0], PF_COLS - PF_G - c_g.shape[1]), w.dtype)
    wf = jnp.concatenate([b_val, b_gate, c_kc, c_vc, c_g, pad], axis=1)
    return wa.astype(BF16), wf.astype(BF16)


def _mixer_layer(x2, batch, seq, layer, p, wbias, cbias):
    h = _rmsnorm(x2, p["attn_norm_g"][layer], out_dtype=BF16, name="attn_norm")
    wa, wf = _split_in_weights(p["w_in"][layer])
    pa = _matmul(h, wa, bm=1024, bn=512, out_dtype=BF16, name="in_proj_attn")
    pf = _matmul(h, wf, bm=1024, bn=PF_COLS // 3, out_dtype=F32, name="in_proj_f32")

    o_a = _window_attention(pa, batch, seq, PA_AQ, PA_AK, PA_AV, wbias, 0,
                            p["swa_sinks"][layer], "swa_attn")
    o_b = _conformer_conv(pf, batch, seq, p["conv_dw"][layer], p["conv_dw_b"][layer],
                          p["conv_ln_g"][layer], p["conv_ln_b"][layer], p["conv_pw"][layer])
    cmp_kv = _compress(pf, batch, seq, p["nsa_cmp_pe"][layer], p["nsa_cmp_w1"][layer],
                       p["nsa_cmp_w2"][layer])
    o_cmp, sel = _cmp_attention(pa, cmp_kv, cbias, batch, seq)
    o_slc = _slc_attention(pa, sel, wbias, p["rel_bias"][:, GROUP_HEADS:], batch, seq)
    o_win = _window_attention(pa, batch, seq, PA_CQ, PA_CKW, PA_CVW, wbias, GROUP_HEADS, None,
                              "nsa_win_attn")
    o_d = _stick_breaking(pa, batch, seq)
    mixed = _mix(o_a, o_b, o_cmp, o_slc, o_win, pf, o_d, p["mix_norm_g"][layer])
    return _matmul(mixed, p["w_out"][layer].astype(BF16), bm=1024, bn=512, out_dtype=F32,
                   residual=x2, name="out_proj")


def kernel(x, attn_norm_g, ffn_norm_g, final_norm_g, w_in, w_out, mix_norm_g, rel_bias,
           swa_sinks, conv_dw, conv_dw_b, conv_ln_g, conv_ln_b, conv_pw, nsa_cmp_pe,
           nsa_cmp_w1, nsa_cmp_w2, ffn_w_gate, ffn_w_up, ffn_w_down, moe_router,
           moe_w_gate, moe_w_up, moe_w_down):
    batch, seq, _ = x.shape
    t = batch * seq
    p = dict(attn_norm_g=attn_norm_g, w_in=w_in, w_out=w_out, mix_norm_g=mix_norm_g,
             rel_bias=rel_bias, swa_sinks=swa_sinks, conv_dw=conv_dw, conv_dw_b=conv_dw_b,
             conv_ln_g=conv_ln_g, conv_ln_b=conv_ln_b, conv_pw=conv_pw, nsa_cmp_pe=nsa_cmp_pe,
             nsa_cmp_w1=nsa_cmp_w1, nsa_cmp_w2=nsa_cmp_w2)
    wbias = _win_bias(rel_bias)
    cbias = _cmp_bias(rel_bias, seq, GROUP_HEADS)
    x2 = x.reshape(t, D_MODEL)
    for layer in range(DEPTH):
        x2 = _mixer_layer(x2, batch, seq, layer, p, wbias, cbias)
        i = layer // 2
        if layer % 2 == 0:
            hf = _rmsnorm(x2, ffn_norm_g[layer], out_dtype=BF16, name="ffn_norm")
            hid = _swiglu_up(hf, ffn_w_gate[i].astype(BF16), ffn_w_up[i].astype(BF16),
                             bm=1024, bn=256, name="ffn_up")
            x2 = _matmul_ksplit_res(hid, ffn_w_down[i].astype(BF16), x2,
                                    bm=1024, bn=512, bk=D_FF // 2, name="ffn_down")
        else:
            hf = _rmsnorm(x2, ffn_norm_g[layer], out_dtype=F32, name="ffn_norm")
            x2 = _moe_swiglu(hf, x2, moe_router[i], moe_w_gate[i], moe_w_up[i], moe_w_down[i])
    return _rmsnorm(x2, final_norm_g, out_dtype=F32, name="final_norm").reshape(batch, seq, D_MODEL)
```

```python
import functools
import math

import jax
import jax.numpy as jnp
import numpy as np
from jax import lax
from jax.experimental import pallas as pl
from jax.experimental.pallas import tpu as pltpu

D_MODEL = 4096
DEPTH = 2
HEAD_DIM = 128
N_MIXERS = 4
GROUP_WIDTH = D_MODEL // N_MIXERS
GROUP_HEADS = GROUP_WIDTH // HEAD_DIM
KV_HEADS = 2
REP = GROUP_HEADS // KV_HEADS
WINDOW = 128
Q_BLOCK = 128
CONV_WIDTH = 31
CMP_LEN = 32
CMP_STRIDE = 16
SLC_LEN = 64
N_SELECT = 16
N_BRANCH = 3
N_BUCKETS = 32
MAX_DISTANCE = 128
D_FF = 11008
N_EXPERTS = 8
TOP_K = 2
D_EXPERT = D_FF // 2
NORM_EPS = 1e-6
SCALE = HEAD_DIM ** -0.5

VMEM_LIMIT_BYTES = 56 * 1024 * 1024
LANES = 128

F32 = jnp.float32
BF16 = jnp.bfloat16
NEG_INF = float("-inf")
F32_TINY = float(np.finfo(np.float32).tiny)

PA_AQ, PA_AK, PA_AV = 0, 1024, 1280
PA_CQ, PA_CKS, PA_CVS, PA_CKW, PA_CVW = 1536, 2560, 2816, 3072, 3328
PA_DQ, PA_DK, PA_DV = 3584, 4608, 5632
PA_COLS = 6656
PF_VAL, PF_GATE, PF_KC, PF_VC, PF_G = 0, 1024, 2048, 2304, 2560
PF_COLS = 2688


def _params(*sem):
    return pltpu.CompilerParams(dimension_semantics=sem, vmem_limit_bytes=VMEM_LIMIT_BYTES)


def _bucket_thresholds():
    n = np.arange(0, 4 * MAX_DISTANCE)
    max_exact = N_BUCKETS // 2
    nf = np.maximum(n, 1).astype(np.float32)
    large = max_exact + (np.log(nf / max_exact) / math.log(MAX_DISTANCE / max_exact)
                         * (N_BUCKETS - max_exact)).astype(np.int32)
    large = np.minimum(large, N_BUCKETS - 1)
    bucket = np.where(n < max_exact, n, large)
    return [int(np.argmax(bucket >= k)) for k in range(N_BUCKETS)]


BUCKET_THRESHOLDS = _bucket_thresholds()


def _mm_kernel(x_ref, w_ref, o_ref):
    o_ref[...] = jnp.dot(x_ref[...], w_ref[...], preferred_element_type=F32).astype(o_ref.dtype)


def _mm_res_kernel(x_ref, w_ref, r_ref, o_ref):
    acc = jnp.dot(x_ref[...], w_ref[...], preferred_element_type=F32)
    o_ref[...] = (r_ref[...] + acc).astype(o_ref.dtype)


def _mm_res_ksplit_kernel(x_ref, w_ref, r_ref, o_ref):
    @pl.when(pl.program_id(2) == 0)
    def _():
        o_ref[...] = r_ref[...]

    o_ref[...] += jnp.dot(x_ref[...], w_ref[...], preferred_element_type=F32)


def _matmul(x, w, *, bm, bn, out_dtype, residual=None, name="matmul"):
    m, k = x.shape
    _, n = w.shape
    assert m % bm == 0 and n % bn == 0
    in_specs = [pl.BlockSpec((bm, k), lambda i, j: (i, 0)),
                pl.BlockSpec((k, bn), lambda i, j: (0, j))]
    args = [x, w]
    body = _mm_kernel
    if residual is not None:
        in_specs.append(pl.BlockSpec((bm, bn), lambda i, j: (i, j)))
        args.append(residual)
        body = _mm_res_kernel
    return pl.pallas_call(
        body,
        out_shape=jax.ShapeDtypeStruct((m, n), out_dtype),
        grid=(m // bm, n // bn),
        in_specs=in_specs,
        out_specs=pl.BlockSpec((bm, bn), lambda i, j: (i, j)),
        compiler_params=_params("parallel", "arbitrary"),
        name=name,
    )(*args)


def _matmul_ksplit_res(x, w, residual, *, bm, bn, bk, name="matmul_ksplit"):
    m, k = x.shape
    _, n = w.shape
    assert m % bm == 0 and n % bn == 0 and k % bk == 0
    return pl.pallas_call(
        _mm_res_ksplit_kernel,
        out_shape=jax.ShapeDtypeStruct((m, n), F32),
        grid=(m // bm, n // bn, k // bk),
        in_specs=[pl.BlockSpec((bm, bk), lambda i, j, kk: (i, kk)),
                  pl.BlockSpec((bk, bn), lambda i, j, kk: (kk, j)),
                  pl.BlockSpec((bm, bn), lambda i, j, kk: (i, j))],
        out_specs=pl.BlockSpec((bm, bn), lambda i, j, kk: (i, j)),
        compiler_params=_params("parallel", "arbitrary", "arbitrary"),
        name=name,
    )(x, w, residual)


def _swiglu_kernel(x_ref, wg_ref, wu_ref, o_ref):
    x = x_ref[...]
    g = jnp.dot(x, wg_ref[...], preferred_element_type=F32)
    u = jnp.dot(x, wu_ref[...], preferred_element_type=F32)
    o_ref[...] = (g * jax.nn.sigmoid(g) * u).astype(o_ref.dtype)


def _swiglu_up(x, wg, wu, *, bm, bn, name="swiglu_up"):
    m, k = x.shape
    _, n = wg.shape
    assert m % bm == 0 and n % bn == 0
    return pl.pallas_call(
        _swiglu_kernel,
        out_shape=jax.ShapeDtypeStruct((m, n), BF16),
        grid=(m // bm, n // bn),
        in_specs=[pl.BlockSpec((bm, k), lambda i, j: (i, 0)),
                  pl.BlockSpec((k, bn), lambda i, j: (0, j)),
                  pl.BlockSpec((k, bn), lambda i, j: (0, j))],
        out_specs=pl.BlockSpec((bm, bn), lambda i, j: (i, j)),
        compiler_params=_params("parallel", "arbitrary"),
        name=name,
    )(x, wg, wu)


def _rmsnorm_kernel(x_ref, g_ref, o_ref):
    x = x_ref[...]
    y = x * lax.rsqrt(jnp.mean(x * x, axis=-1, keepdims=True) + NORM_EPS)
    o_ref[...] = (y * g_ref[...]).astype(o_ref.dtype)


def _rmsnorm(x, g, *, out_dtype, bm=256, name="rmsnorm"):
    m, d = x.shape
    return pl.pallas_call(
        _rmsnorm_kernel,
        out_shape=jax.ShapeDtypeStruct((m, d), out_dtype),
        grid=(m // bm,),
        in_specs=[pl.BlockSpec((bm, d), lambda i: (i, 0)),
                  pl.BlockSpec((1, d), lambda i: (0, 0))],
        out_specs=pl.BlockSpec((bm, d), lambda i: (i, 0)),
        compiler_params=_params("parallel"),
        name=name,
    )(x, g.reshape(1, d))


def _bias_of_dist(dist, tab_ref, head):
    out = jnp.full(dist.shape, tab_ref[0, head], F32)
    for k in range(1, N_BUCKETS):
        out = jnp.where(dist >= BUCKET_THRESHOLDS[k], tab_ref[k, head], out)
    return out


def _win_bias_kernel(tab_ref, o_ref):
    h = pl.program_id(0)
    qi = lax.broadcasted_iota(jnp.int32, (Q_BLOCK, 2 * Q_BLOCK), 0)
    kj = lax.broadcasted_iota(jnp.int32, (Q_BLOCK, 2 * Q_BLOCK), 1)
    o_ref[0] = _bias_of_dist(qi + Q_BLOCK - kj, tab_ref, h)


def _win_bias(rel_bias):
    nh = rel_bias.shape[1]
    return pl.pallas_call(
        _win_bias_kernel,
        out_shape=jax.ShapeDtypeStruct((nh, Q_BLOCK, 2 * Q_BLOCK), F32),
        grid=(nh,),
        in_specs=[pl.BlockSpec(memory_space=pltpu.SMEM)],
        out_specs=pl.BlockSpec((1, Q_BLOCK, 2 * Q_BLOCK), lambda h: (h, 0, 0)),
        compiler_params=_params("arbitrary"),
        name="win_bias",
    )(rel_bias)


def _cmp_bias_kernel(tab_ref, o_ref, *, head0, rows):
    h = pl.program_id(0) + head0
    n = pl.program_id(1)
    ncmp = o_ref.shape[2]
    t = n * rows + lax.broadcasted_iota(jnp.int32, (rows, ncmp), 0)
    c = lax.broadcasted_iota(jnp.int32, (rows, ncmp), 1)
    o_ref[0] = _bias_of_dist(t - c * CMP_STRIDE - (CMP_LEN - 1), tab_ref, h)


def _cmp_bias(rel_bias, seq, head0, rows=512):
    ncmp = seq // CMP_STRIDE
    rows = min(rows, seq)
    return pl.pallas_call(
        functools.partial(_cmp_bias_kernel, head0=head0, rows=rows),
        out_shape=jax.ShapeDtypeStruct((GROUP_HEADS, seq, ncmp), F32),
        grid=(GROUP_HEADS, seq // rows),
        in_specs=[pl.BlockSpec(memory_space=pltpu.SMEM)],
        out_specs=pl.BlockSpec((1, rows, ncmp), lambda h, n: (h, n, 0)),
        compiler_params=_params("arbitrary", "arbitrary"),
        name="cmp_bias",
    )(rel_bias)


def _stack_heads(q):
    return jnp.concatenate([q[:, r * HEAD_DIM:(r + 1) * HEAD_DIM] for r in range(REP)], axis=0)


def _store_heads(o_ref, o, rows):
    for r in range(REP):
        o_ref[:, r * HEAD_DIM:(r + 1) * HEAD_DIM] = o[r * rows:(r + 1) * rows].astype(o_ref.dtype)


def _qk(q, k):
    return lax.dot_general(q, k, (((1,), (1,)), ((), ())), preferred_element_type=F32)


def _window_kernel(q_ref, kp_ref, kc_ref, vp_ref, vc_ref, bias_ref, sink_ref, o_ref, *, has_sink):
    n = pl.program_id(2)
    q4 = _stack_heads(q_ref[...])
    kcat = jnp.concatenate([kp_ref[...], kc_ref[...]], axis=0)
    vcat = jnp.concatenate([vp_ref[...], vc_ref[...]], axis=0)
    s = _qk(q4, kcat).reshape(REP, Q_BLOCK, 2 * Q_BLOCK) * SCALE + bias_ref[...]
    qi = lax.broadcasted_iota(jnp.int32, (Q_BLOCK, 2 * Q_BLOCK), 0)
    kj = lax.broadcasted_iota(jnp.int32, (Q_BLOCK, 2 * Q_BLOCK), 1)
    dist = qi + Q_BLOCK - kj
    mask = ((dist >= 0) & (dist < WINDOW) & ((n > 0) | (kj >= Q_BLOCK)))[None]
    s = jnp.where(mask, s, NEG_INF)
    m = jnp.max(s, axis=-1, keepdims=True)
    if has_sink:
        sink = sink_ref[...]
        m = jnp.maximum(m, sink)
    p = jnp.where(mask, jnp.exp(s - m), 0.0)
    den = jnp.sum(p, axis=-1, keepdims=True)
    if has_sink:
        den = den + jnp.exp(sink - m)
    p = p / jnp.maximum(den, F32_TINY)
    o = jnp.dot(p.reshape(REP * Q_BLOCK, 2 * Q_BLOCK).astype(BF16), vcat, preferred_element_type=F32)
    _store_heads(o_ref, o, Q_BLOCK)


def _window_attention(pa, batch, seq, q_col, k_col, v_col, bias, bias_head0, sinks, name):
    nb = seq // Q_BLOCK
    qw = REP * HEAD_DIM
    has_sink = sinks is not None
    sink_arr = (sinks if has_sink else jnp.zeros((GROUP_HEADS,), F32)).reshape(GROUP_HEADS, 1, 1)

    def cur(col):
        return lambda b, g, n: (b * nb + n, col // HEAD_DIM + g)

    def prev(col):
        return lambda b, g, n: (b * nb + jnp.maximum(n - 1, 0), col // HEAD_DIM + g)

    kv_block = (Q_BLOCK, HEAD_DIM)
    return pl.pallas_call(
        functools.partial(_window_kernel, has_sink=has_sink),
        out_shape=jax.ShapeDtypeStruct((batch * seq, GROUP_WIDTH), F32),
        grid=(batch, KV_HEADS, nb),
        in_specs=[pl.BlockSpec((Q_BLOCK, qw), lambda b, g, n: (b * nb + n, q_col // qw + g)),
                  pl.BlockSpec(kv_block, prev(k_col)), pl.BlockSpec(kv_block, cur(k_col)),
                  pl.BlockSpec(kv_block, prev(v_col)), pl.BlockSpec(kv_block, cur(v_col)),
                  pl.BlockSpec((REP, Q_BLOCK, 2 * Q_BLOCK),
                               lambda b, g, n: (bias_head0 // REP + g, 0, 0)),
                  pl.BlockSpec((REP, 1, 1), lambda b, g, n: (g, 0, 0))],
        out_specs=pl.BlockSpec((Q_BLOCK, qw), lambda b, g, n: (b * nb + n, g)),
        compiler_params=_params("parallel", "parallel", "arbitrary"),
        name=name,
    )(pa, pa, pa, pa, pa, bias, sink_arr)


def _compress_kernel(t_ref, pe_ref, w1_ref, w2_ref, o_ref):
    ncmp = o_ref.shape[3]
    half = CMP_LEN // 2
    pe = pe_ref[0]
    rows = [t_ref[pl.ds(r, ncmp, stride=CMP_STRIDE), :] for r in range(CMP_STRIDE)]
    xa = jnp.concatenate([rows[r] + pe[r:r + 1] for r in range(half)], axis=1).astype(BF16)
    xb = jnp.concatenate([rows[r] + pe[half + r:half + r + 1] for r in range(half)], axis=1).astype(BF16)
    kw = half * HEAD_DIM
    p0 = jnp.dot(xa, w1_ref[0, :kw, :], preferred_element_type=F32)
    p1 = jnp.dot(xb, w1_ref[0, kw:, :], preferred_element_type=F32)
    pre = p0 + pltpu.roll(p1, ncmp - 1, 0)
    hid = pre * jax.nn.sigmoid(pre)
    o_ref[0, 0, 0] = jnp.dot(hid.astype(BF16), w2_ref[0], preferred_element_type=F32).astype(o_ref.dtype)


def _compress(pf, batch, seq, pe, w1, w2):
    ncmp = seq // CMP_STRIDE
    return pl.pallas_call(
        _compress_kernel,
        out_shape=jax.ShapeDtypeStruct((2, batch, KV_HEADS, ncmp, HEAD_DIM), BF16),
        grid=(2, batch, KV_HEADS),
        in_specs=[pl.BlockSpec((seq, HEAD_DIM), lambda kv, b, g: (b, PF_KC // HEAD_DIM + KV_HEADS * kv + g)),
                  pl.BlockSpec((1, CMP_LEN, HEAD_DIM), lambda kv, b, g: (kv, 0, 0)),
                  pl.BlockSpec((1, CMP_LEN * HEAD_DIM, HEAD_DIM), lambda kv, b, g: (kv, 0, 0)),
                  pl.BlockSpec((1, HEAD_DIM, HEAD_DIM), lambda kv, b, g: (kv, 0, 0))],
        out_specs=pl.BlockSpec((1, 1, 1, ncmp, HEAD_DIM), lambda kv, b, g: (kv, b, g, 0, 0)),
        compiler_params=_params("arbitrary", "arbitrary", "arbitrary"),
        name="nsa_compress",
    )(pf, pe, w1.astype(BF16), w2.astype(BF16))


def _cmp_attn_kernel(q_ref, kc_ref, vc_ref, bias_ref, ov_ref, o_ref, sel_ref, *, n_sel):
    n = pl.program_id(2)
    ncmp = kc_ref.shape[3]
    nslc = ov_ref.shape[1]
    q4 = _stack_heads(q_ref[...])
    s = _qk(q4, kc_ref[0, 0, 0]).reshape(REP, Q_BLOCK, ncmp) * SCALE + bias_ref[...]
    t = n * Q_BLOCK + lax.broadcasted_iota(jnp.int32, (Q_BLOCK, ncmp), 0)
    c = lax.broadcasted_iota(jnp.int32, (Q_BLOCK, ncmp), 1)
    vis = (t - c * CMP_STRIDE - (CMP_LEN - 1) >= 0)[None]
    s = jnp.where(vis, s, NEG_INF)
    m = jnp.max(s, axis=-1, keepdims=True)
    m = jnp.where(m == NEG_INF, 0.0, m)
    p = jnp.where(vis, jnp.exp(s - m), 0.0)
    den = jnp.sum(p, axis=-1, keepdims=True)
    p = p / jnp.maximum(den, F32_TINY)
    o = jnp.dot(p.reshape(REP * Q_BLOCK, ncmp).astype(BF16), vc_ref[0, 0, 0], preferred_element_type=F32)
    _store_heads(o_ref, o, Q_BLOCK)

    psum = p[0]
    for r in range(1, REP):
        psum = psum + p[r]
    hi = psum.astype(BF16)
    lo = (psum - hi.astype(F32)).astype(BF16)
    ov = ov_ref[...]
    imp = jnp.dot(hi, ov, preferred_element_type=F32) + jnp.dot(lo, ov, preferred_element_type=F32)
    tq = n * Q_BLOCK + lax.broadcasted_iota(jnp.int32, (Q_BLOCK, nslc), 0)
    blk = lax.broadcasted_iota(jnp.int32, (Q_BLOCK, nslc), 1)
    cur = tq // SLC_LEN
    forced = (blk == 0) | (blk == cur) | (blk == cur - 1)
    score = jnp.where(forced, jnp.inf, jnp.where(blk <= cur, imp, NEG_INF))
    rank = jnp.zeros((Q_BLOCK, nslc), F32)
    for k in range(nslc):
        col = score[:, k:k + 1]
        ahead = (col > score) | ((col == score) & (blk > k))
        rank = rank + jnp.where(ahead, 1.0, 0.0)
    sel_ref[0, 0] = jnp.where(rank < n_sel, 1.0, 0.0)


def _overlap_matrix(ncmp, nslc):
    cs = np.arange(ncmp)[:, None] * CMP_STRIDE
    ss = np.arange(nslc)[None, :] * SLC_LEN
    return ((cs < ss + SLC_LEN) & (cs + CMP_LEN > ss)).astype(np.float32)


def _cmp_attention(pa, cmp_kv, cbias, batch, seq):
    nb = seq // Q_BLOCK
    ncmp = seq // CMP_STRIDE
    nslc = seq // SLC_LEN
    qw = REP * HEAD_DIM
    ov = jnp.asarray(_overlap_matrix(ncmp, nslc), BF16)
    return pl.pallas_call(
        functools.partial(_cmp_attn_kernel, n_sel=min(N_SELECT, nslc)),
        out_shape=(jax.ShapeDtypeStruct((batch * seq, GROUP_WIDTH), F32),
                   jax.ShapeDtypeStruct((batch, KV_HEADS, seq, nslc), F32)),
        grid=(batch, KV_HEADS, nb),
        in_specs=[pl.BlockSpec((Q_BLOCK, qw), lambda b, g, n: (b * nb + n, PA_CQ // qw + g)),
                  pl.BlockSpec((1, 1, 1, ncmp, HEAD_DIM), lambda b, g, n: (0, b, g, 0, 0)),
                  pl.BlockSpec((1, 1, 1, ncmp, HEAD_DIM), lambda b, g, n: (1, b, g, 0, 0)),
                  pl.BlockSpec((REP, Q_BLOCK, ncmp), lambda b, g, n: (g, n, 0)),
                  pl.BlockSpec((ncmp, nslc), lambda b, g, n: (0, 0))],
        out_specs=(pl.BlockSpec((Q_BLOCK, qw), lambda b, g, n: (b * nb + n, g)),
                   pl.BlockSpec((1, 1, Q_BLOCK, nslc), lambda b, g, n: (b, g, n, 0))),
        compiler_params=_params("parallel", "parallel", "arbitrary"),
        name="nsa_cmp_attn",
    )(pa, cmp_kv, cmp_kv, cbias, ov)


FAR_TILE = 2 * Q_BLOCK


def _slc_attn_kernel(q_ref, k_ref, v_ref, sel_ref, bias_ref, far_bias_ref, efar_ref, enear_ref, o_ref):
    n = pl.program_id(2)
    q4 = _stack_heads(q_ref[...])
    selb = sel_ref[0, 0].astype(BF16)
    far_bias = far_bias_ref[...]
    rows = REP * Q_BLOCK

    def online(carry, s, mask, v):
        m, l, acc = carry
        s = jnp.where(mask[None], s, NEG_INF)
        m_new = jnp.maximum(m, jnp.max(s, axis=-1, keepdims=True))
        m_safe = jnp.where(m_new == NEG_INF, 0.0, m_new)
        alpha = jnp.exp(m - m_safe)
        p = jnp.exp(s - m_safe)
        l = alpha * l + jnp.sum(p, axis=-1, keepdims=True)
        pv = jnp.dot(p.reshape(rows, p.shape[-1]).astype(BF16), v, preferred_element_type=F32)
        acc = alpha * acc + pv.reshape(REP, Q_BLOCK, HEAD_DIM)
        return m_new, l, acc

    far_end = (n - 1) * Q_BLOCK

    def far_step(j, carry):
        k0 = pl.multiple_of(j * FAR_TILE, FAR_TILE)
        k = k_ref[pl.ds(k0, FAR_TILE), :]
        v = v_ref[pl.ds(k0, FAR_TILE), :]
        s = _qk(q4, k).reshape(REP, Q_BLOCK, FAR_TILE) * SCALE + far_bias
        picked = jnp.dot(selb, efar_ref[j], preferred_element_type=F32) > 0.5
        kidx = k0 + lax.broadcasted_iota(jnp.int32, (Q_BLOCK, FAR_TILE), 1)
        return online(carry, s, picked & (kidx < far_end), v)

    init = (jnp.full((REP, Q_BLOCK, 1), NEG_INF, F32), jnp.zeros((REP, Q_BLOCK, 1), F32),
            jnp.zeros((REP, Q_BLOCK, HEAD_DIM), F32))
    carry = lax.fori_loop(0, n // 2, far_step, init)

    pb = jnp.maximum(n - 1, 0)
    p0 = pl.multiple_of(pb * Q_BLOCK, Q_BLOCK)
    c0 = pl.multiple_of(n * Q_BLOCK, Q_BLOCK)
    kcat = jnp.concatenate([k_ref[pl.ds(p0, Q_BLOCK), :], k_ref[pl.ds(c0, Q_BLOCK), :]], axis=0)
    vcat = jnp.concatenate([v_ref[pl.ds(p0, Q_BLOCK), :], v_ref[pl.ds(c0, Q_BLOCK), :]], axis=0)
    s = _qk(q4, kcat).reshape(REP, Q_BLOCK, 2 * Q_BLOCK) * SCALE + bias_ref[...]
    picked = jnp.concatenate(
        [jnp.dot(selb, enear_ref[pb], preferred_element_type=F32),
         jnp.dot(selb, enear_ref[n], preferred_element_type=F32)], axis=1) > 0.5
    qi = lax.broadcasted_iota(jnp.int32, (Q_BLOCK, 2 * Q_BLOCK), 0)
    kj = lax.broadcasted_iota(jnp.int32, (Q_BLOCK, 2 * Q_BLOCK), 1)
    causal = (qi + Q_BLOCK - kj >= 0) & ((n > 0) | (kj >= Q_BLOCK))
    m, l, acc = online(carry, s, picked & causal, vcat)
    o = acc / jnp.maximum(l, F32_TINY)
    for r in range(REP):
        o_ref[:, r * HEAD_DIM:(r + 1) * HEAD_DIM] = o[r].astype(o_ref.dtype)


def _expand_matrix(seq, tile):
    nslc = seq // SLC_LEN
    key = np.arange(seq).reshape(seq // tile, 1, tile)
    return (key // SLC_LEN == np.arange(nslc)[None, :, None]).astype(np.float32)


def _slc_attention(pa, sel, wbias, rel_bias_c, batch, seq):
    nb = seq // Q_BLOCK
    nslc = seq // SLC_LEN
    qw = REP * HEAD_DIM
    nfar = max(seq // FAR_TILE, 1)
    efar = jnp.asarray(_expand_matrix(max(seq, FAR_TILE), FAR_TILE)[:, :nslc], BF16)
    enear = jnp.asarray(_expand_matrix(seq, Q_BLOCK), BF16)
    far_bias = rel_bias_c[N_BUCKETS - 1].reshape(GROUP_HEADS, 1, 1)
    return pl.pallas_call(
        _slc_attn_kernel,
        out_shape=jax.ShapeDtypeStruct((batch * seq, GROUP_WIDTH), F32),
        grid=(batch, KV_HEADS, nb),
        in_specs=[pl.BlockSpec((Q_BLOCK, qw), lambda b, g, n: (b * nb + n, PA_CQ // qw + g)),
                  pl.BlockSpec((seq, HEAD_DIM), lambda b, g, n: (b, PA_CKS // HEAD_DIM + g)),
                  pl.BlockSpec((seq, HEAD_DIM), lambda b, g, n: (b, PA_CVS // HEAD_DIM + g)),
                  pl.BlockSpec((1, 1, Q_BLOCK, nslc), lambda b, g, n: (b, g, n, 0)),
                  pl.BlockSpec((REP, Q_BLOCK, 2 * Q_BLOCK),
                               lambda b, g, n: (GROUP_HEADS // REP + g, 0, 0)),
                  pl.BlockSpec((REP, 1, 1), lambda b, g, n: (g, 0, 0)),
                  pl.BlockSpec((nfar, nslc, FAR_TILE), lambda b, g, n: (0, 0, 0)),
                  pl.BlockSpec((nb, nslc, Q_BLOCK), lambda b, g, n: (0, 0, 0))],
        out_specs=pl.BlockSpec((Q_BLOCK, qw), lambda b, g, n: (b * nb + n, g)),
        compiler_params=_params("parallel", "parallel", "arbitrary"),
        name="nsa_slc_attn",
    )(pa, pa, pa, sel, wbias, far_bias, efar, enear)


SB_TQ = 512
SB_TK = 256
SB_HEADS = 2
LOG2E = math.log2(math.e)


def _sb_kernel(q_ref, k_ref, v_ref, u_ref, o_ref):
    n = pl.program_id(2)
    u2 = u_ref[...]
    ratio = SB_TQ // SB_TK

    def tile(j, carry, masked, hd):
        later, acc = carry
        cols = slice(hd * HEAD_DIM, (hd + 1) * HEAD_DIM)
        k0 = pl.multiple_of(j * SB_TK, SB_TK)
        k = k_ref[pl.ds(k0, SB_TK), cols]
        v = v_ref[pl.ds(k0, SB_TK), cols]
        z = _qk(q_ref[:, cols], k) * (SCALE * LOG2E)
        zneg = jnp.minimum(z, 0.0)
        zpos_neg = zneg - z
        t = jnp.log2(1.0 + jnp.exp2(zneg + zpos_neg))
        log_keep = zpos_neg - t
        if masked:
            qi = n * SB_TQ + lax.broadcasted_iota(jnp.int32, (SB_TQ, SB_TK), 0)
            kj = k0 + lax.broadcasted_iota(jnp.int32, (SB_TQ, SB_TK), 1)
            before = kj < qi
            log_keep = jnp.where(before, log_keep, 0.0)
        hi = log_keep.astype(BF16)
        lo = (log_keep - hi.astype(F32)).astype(BF16)
        suffix = jnp.dot(jnp.concatenate([hi, lo], axis=1), u2, preferred_element_type=F32)
        a = jnp.exp2(z + suffix + later)
        if masked:
            a = jnp.where(before, a, 0.0)
        acc = acc + jnp.dot(a.astype(BF16), v, preferred_element_type=F32)
        later = later + suffix[:, 0:1]
        return later, acc

    def tiles(j, carries, masked):
        return tuple(tile(j, carries[hd], masked, hd) for hd in range(SB_HEADS))

    carries = tuple((jnp.zeros((SB_TQ, 1), F32), jnp.zeros((SB_TQ, HEAD_DIM), F32))
                    for _ in range(SB_HEADS))
    for d in range(ratio):
        carries = tiles(n * ratio + (ratio - 1 - d), carries, True)
    carries = lax.fori_loop(0, n * ratio, lambda i, c: tiles(n * ratio - 1 - i, c, False), carries)
    for hd in range(SB_HEADS):
        o_ref[:, hd * HEAD_DIM:(hd + 1) * HEAD_DIM] = carries[hd][1].astype(o_ref.dtype)


def _stick_breaking(pa, batch, seq):
    assert seq % SB_TQ == 0
    nb = seq // SB_TQ
    hw = SB_HEADS * HEAD_DIM
    tri = np.tril(np.ones((SB_TK, SB_TK), np.float32))
    u2 = jnp.asarray(np.concatenate([tri, tri], axis=0), BF16)
    return pl.pallas_call(
        _sb_kernel,
        out_shape=jax.ShapeDtypeStruct((batch * seq, GROUP_WIDTH), F32),
        grid=(batch, GROUP_HEADS // SB_HEADS, nb),
        in_specs=[pl.BlockSpec((SB_TQ, hw), lambda b, h, n: (b * nb + n, PA_DQ // hw + h)),
                  pl.BlockSpec((seq, hw), lambda b, h, n: (b, PA_DK // hw + h)),
                  pl.BlockSpec((seq, hw), lambda b, h, n: (b, PA_DV // hw + h)),
                  pl.BlockSpec((2 * SB_TK, SB_TK), lambda b, h, n: (0, 0))],
        out_specs=pl.BlockSpec((SB_TQ, hw), lambda b, h, n: (b * nb + n, h)),
        compiler_params=_params("parallel", "parallel", "arbitrary"),
        name="stick_breaking",
    )(pa, pa, pa, u2)


CONV_TILE = 256
CONV_HALO = 32


def _conv_kernel(val_ref, gate_ref, hval_ref, hgate_ref, dw_ref, dwb_ref, lng_ref, lnb_ref, pw_ref,
                 o_ref, ext_ref):
    n = pl.program_id(1)
    halo = hval_ref[...] * jax.nn.sigmoid(hgate_ref[...])
    ext_ref[0:CONV_HALO, :] = jnp.where(n > 0, halo, 0.0)
    ext_ref[CONV_HALO:, :] = val_ref[...] * jax.nn.sigmoid(gate_ref[...])
    first = CONV_HALO - (CONV_WIDTH - 1)
    acc = jnp.zeros((CONV_TILE, GROUP_WIDTH), F32) + dwb_ref[...]
    for w in range(CONV_WIDTH):
        acc = acc + ext_ref[first + w:first + w + CONV_TILE, :] * dw_ref[w:w + 1, :]
    mu = jnp.mean(acc, axis=-1, keepdims=True)
    cen = acc - mu
    var = jnp.mean(cen * cen, axis=-1, keepdims=True)
    y = cen * lax.rsqrt(var + NORM_EPS) * lng_ref[...] + lnb_ref[...]
    y = y * jax.nn.sigmoid(y)
    o_ref[...] = jnp.dot(y.astype(BF16), pw_ref[...], preferred_element_type=F32).astype(o_ref.dtype)


def _conformer_conv(pf, batch, seq, dw, dw_b, ln_g, ln_b, pw):
    nt = seq // CONV_TILE
    hpt = CONV_TILE // CONV_HALO
    c = GROUP_WIDTH
    vec = pl.BlockSpec((1, c), lambda b, n: (0, 0))

    def halo(col):
        return lambda b, n: (jnp.maximum((b * nt + n) * hpt - 1, 0), col // c)

    return pl.pallas_call(
        _conv_kernel,
        out_shape=jax.ShapeDtypeStruct((batch * seq, c), F32),
        grid=(batch, nt),
        in_specs=[pl.BlockSpec((CONV_TILE, c), lambda b, n: (b * nt + n, PF_VAL // c)),
                  pl.BlockSpec((CONV_TILE, c), lambda b, n: (b * nt + n, PF_GATE // c)),
                  pl.BlockSpec((CONV_HALO, c), halo(PF_VAL)),
                  pl.BlockSpec((CONV_HALO, c), halo(PF_GATE)),
                  pl.BlockSpec((CONV_WIDTH, c), lambda b, n: (0, 0)),
                  vec, vec, vec,
                  pl.BlockSpec((c, c), lambda b, n: (0, 0))],
        out_specs=pl.BlockSpec((CONV_TILE, c), lambda b, n: (b * nt + n, 0)),
        scratch_shapes=[pltpu.VMEM((CONV_HALO + CONV_TILE, c), F32)],
        compiler_params=_params("parallel", "arbitrary"),
        name="conformer_conv",
    )(pf, pf, pf, pf, dw, dw_b.reshape(1, c), ln_g.reshape(1, c), ln_b.reshape(1, c), pw.astype(BF16))


MIX_TILE = 256


def _mix_kernel(oa_ref, ob_ref, ocmp_ref, oslc_ref, owin_ref, gl_ref, od_ref, g_ref, o_ref):
    gates = jax.nn.sigmoid(gl_ref[...])

    def norm_store(x, grp):
        y = x * lax.rsqrt(jnp.mean(x * x, axis=-1, keepdims=True) + NORM_EPS)
        sl = slice(grp * GROUP_WIDTH, (grp + 1) * GROUP_WIDTH)
        o_ref[:, sl] = (y * g_ref[:, sl]).astype(o_ref.dtype)

    norm_store(oa_ref[...], 0)
    norm_store(ob_ref[...], 1)
    heads = []
    for h in range(GROUP_HEADS):
        sl = slice(h * HEAD_DIM, (h + 1) * HEAD_DIM)
        c0 = h * N_BRANCH
        heads.append(gates[:, c0:c0 + 1] * ocmp_ref[:, sl] + gates[:, c0 + 1:c0 + 2] * oslc_ref[:, sl]
                     + gates[:, c0 + 2:c0 + 3] * owin_ref[:, sl])
    norm_store(jnp.concatenate(heads, axis=1), 2)
    norm_store(od_ref[...], 3)


def _mix(o_a, o_b, o_cmp, o_slc, o_win, pf, o_d, g):
    t = o_a.shape[0]
    grp = pl.BlockSpec((MIX_TILE, GROUP_WIDTH), lambda i: (i, 0))
    return pl.pallas_call(
        _mix_kernel,
        out_shape=jax.ShapeDtypeStruct((t, D_MODEL), BF16),
        grid=(t // MIX_TILE,),
        in_specs=[grp, grp, grp, grp, grp,
                  pl.BlockSpec((MIX_TILE, LANES), lambda i: (i, PF_G // LANES)),
                  grp,
                  pl.BlockSpec((1, D_MODEL), lambda i: (0, 0))],
        out_specs=pl.BlockSpec((MIX_TILE, D_MODEL), lambda i: (i, 0)),
        compiler_params=_params("parallel"),
        name="mix_norm",
    )(o_a, o_b, o_cmp, o_slc, o_win, pf, o_d, g.reshape(1, D_MODEL))


MOE_TILE = 256
MOE_UP_BN = 512
MOE_DOWN_BN = 1024
D_EXPERT_PAD = 5632
ROUTE_TILE = 256
GATHER_TILE = 256
COMBINE_TILE = 128


def _norm_route_kernel(x_ref, g_ref, r_ref, h_ref, route_ref):
    x = x_ref[...]
    y = x * lax.rsqrt(jnp.mean(x * x, axis=-1, keepdims=True) + NORM_EPS) * g_ref[...]
    h_ref[...] = y
    logits = jnp.dot(y.astype(BF16), r_ref[...], preferred_element_type=F32)
    lane = lax.broadcasted_iota(jnp.int32, logits.shape, 1)
    logits = jnp.where(lane < N_EXPERTS, logits, NEG_INF)
    e = jnp.exp(logits - jnp.max(logits, axis=-1, keepdims=True))
    probs = e / jnp.sum(e, axis=-1, keepdims=True)
    p1 = jnp.max(probs, axis=-1, keepdims=True)
    i1 = jnp.min(jnp.where(probs == p1, lane, LANES), axis=-1, keepdims=True)
    rest = jnp.where(lane == i1, -1.0, probs)
    p2 = jnp.max(rest, axis=-1, keepdims=True)
    i2 = jnp.min(jnp.where(rest == p2, lane, LANES), axis=-1, keepdims=True)
    tot = p1 + p2
    route_ref[...] = jnp.where(lane == 0, p1 / tot,
                               jnp.where(lane == 1, p2 / tot,
                                         jnp.where(lane == 2, i1.astype(F32),
                                                   jnp.where(lane == 3, i2.astype(F32), 0.0))))


def _norm_route(x, g, router):
    m, d = x.shape
    rpad = jnp.zeros((d, LANES), BF16).at[:, :N_EXPERTS].set(router.astype(BF16))
    return pl.pallas_call(
        _norm_route_kernel,
        out_shape=(jax.ShapeDtypeStruct((m, d), F32), jax.ShapeDtypeStruct((m, LANES), F32)),
        grid=(m // ROUTE_TILE,),
        in_specs=[pl.BlockSpec((ROUTE_TILE, d), lambda i: (i, 0)),
                  pl.BlockSpec((1, d), lambda i: (0, 0)),
                  pl.BlockSpec((d, LANES), lambda i: (0, 0))],
        out_specs=(pl.BlockSpec((ROUTE_TILE, d), lambda i: (i, 0)),
                   pl.BlockSpec((ROUTE_TILE, LANES), lambda i: (i, 0))),
        compiler_params=_params("parallel"),
        name="ffn_norm_route",
    )(x, g.reshape(1, d), rpad)


def _row_copy(src_hbm, dst_ref, sem, src_row, dst_row):
    return pltpu.make_async_copy(src_hbm.at[pl.ds(src_row, 1), :], dst_ref.at[pl.ds(dst_row, 1), :], sem)


def _gather_kernel(tok_ref, h_hbm, o_ref, buf_ref, sem):
    base = pl.program_id(0) * GATHER_TILE

    def start(r, c):
        _row_copy(h_hbm, buf_ref, sem, tok_ref[base + r], r).start()
        return c

    def wait(r, c):
        _row_copy(h_hbm, buf_ref, sem, 0, r).wait()
        return c

    lax.fori_loop(0, GATHER_TILE, start, 0)
    lax.fori_loop(0, GATHER_TILE, wait, 0)
    o_ref[...] = buf_ref[...].astype(o_ref.dtype)


def _gather_rows(h, row_token):
    rows = row_token.shape[0]
    d = h.shape[1]
    return pl.pallas_call(
        _gather_kernel,
        out_shape=jax.ShapeDtypeStruct((rows, d), BF16),
        grid_spec=pltpu.PrefetchScalarGridSpec(
            num_scalar_prefetch=1,
            grid=(rows // GATHER_TILE,),
            in_specs=[pl.BlockSpec(memory_space=pl.ANY)],
            out_specs=pl.BlockSpec((GATHER_TILE, d), lambda i, tok: (i, 0)),
            scratch_shapes=[pltpu.VMEM((GATHER_TILE, d), F32), pltpu.SemaphoreType.DMA(())]),
        compiler_params=_params("arbitrary"),
        name="moe_gather",
    )(row_token, h)


def _moe_up_kernel(te_ref, nu_ref, x_ref, wg_ref, wu_ref, o_ref):
    i = pl.program_id(1)

    @pl.when(i < nu_ref[0])
    def _():
        x = x_ref[...]
        g = jnp.dot(x, wg_ref[0], preferred_element_type=F32)
        u = jnp.dot(x, wu_ref[0], preferred_element_type=F32)
        o_ref[...] = (g * jax.nn.sigmoid(g) * u).astype(o_ref.dtype)

    @pl.when(i >= nu_ref[0])
    def _():
        o_ref[...] = jnp.zeros_like(o_ref)


def _moe_down_kernel(te_ref, nu_ref, x_ref, w_ref, o_ref):
    i = pl.program_id(1)

    @pl.when(i < nu_ref[0])
    def _():
        o_ref[...] = jnp.dot(x_ref[...], w_ref[0], preferred_element_type=F32)

    @pl.when(i >= nu_ref[0])
    def _():
        o_ref[...] = jnp.zeros_like(o_ref)


def _grouped_call(body, tile_expert, n_used, xs, weights, bn, out_dtype, name):
    rows, k = xs.shape
    n = weights[0].shape[2]
    nt = rows // MOE_TILE

    def used(i, nu):
        return jnp.minimum(i, nu[0] - 1)

    return pl.pallas_call(
        body,
        out_shape=jax.ShapeDtypeStruct((rows, n), out_dtype),
        grid_spec=pltpu.PrefetchScalarGridSpec(
            num_scalar_prefetch=2,
            grid=(n // bn, nt),
            in_specs=[pl.BlockSpec((MOE_TILE, k), lambda j, i, te, nu: (used(i, nu), 0))]
            + [pl.BlockSpec((1, k, bn), lambda j, i, te, nu: (te[used(i, nu)], 0, j))
               for _ in weights],
            out_specs=pl.BlockSpec((MOE_TILE, bn), lambda j, i, te, nu: (i, j))),
        compiler_params=_params("arbitrary", "arbitrary"),
        name=name,
    )(tile_expert, n_used, xs, *weights)


def _combine_kernel(dest_ref, x_ref, route_ref, y_hbm, o_ref, y0_ref, y1_ref, sem):
    base = pl.program_id(0) * COMBINE_TILE

    def start(r, c):
        _row_copy(y_hbm, y0_ref, sem, dest_ref[2 * (base + r)], r).start()
        _row_copy(y_hbm, y1_ref, sem, dest_ref[2 * (base + r) + 1], r).start()
        return c

    def wait(r, c):
        _row_copy(y_hbm, y0_ref, sem, 0, r).wait()
        _row_copy(y_hbm, y1_ref, sem, 0, r).wait()
        return c

    lax.fori_loop(0, COMBINE_TILE, start, 0)
    lax.fori_loop(0, COMBINE_TILE, wait, 0)
    route = route_ref[...]
    o_ref[...] = x_ref[...] + route[:, 0:1] * y0_ref[...] + route[:, 1:2] * y1_ref[...]


def _moe_combine(x2, route, y, dest):
    t, d = x2.shape
    return pl.pallas_call(
        _combine_kernel,
        out_shape=jax.ShapeDtypeStruct((t, d), F32),
        grid_spec=pltpu.PrefetchScalarGridSpec(
            num_scalar_prefetch=1,
            grid=(t // COMBINE_TILE,),
            in_specs=[pl.BlockSpec((COMBINE_TILE, d), lambda i, dest: (i, 0)),
                      pl.BlockSpec((COMBINE_TILE, LANES), lambda i, dest: (i, 0)),
                      pl.BlockSpec(memory_space=pl.ANY)],
            out_specs=pl.BlockSpec((COMBINE_TILE, d), lambda i, dest: (i, 0)),
            scratch_shapes=[pltpu.VMEM((COMBINE_TILE, d), F32), pltpu.VMEM((COMBINE_TILE, d), F32),
                            pltpu.SemaphoreType.DMA(())]),
        compiler_params=_params("arbitrary"),
        name="moe_combine",
    )(dest, x2, route, y)


def _routing_tables(top_i):
    t = top_i.shape[0]
    e_flat = top_i.reshape(-1)
    onehot = (e_flat[:, None] == jnp.arange(N_EXPERTS, dtype=jnp.int32)[None, :]).astype(jnp.int32)
    csum = jnp.cumsum(onehot, axis=0)
    counts = csum[-1]
    pos = jnp.take_along_axis(csum, e_flat[:, None], axis=1)[:, 0] - 1
    padded = ((counts + MOE_TILE - 1) // MOE_TILE) * MOE_TILE
    ends = jnp.cumsum(padded)
    dest = ((ends - padded)[e_flat] + pos).astype(jnp.int32)
    rows = TOP_K * t + N_EXPERTS * MOE_TILE
    row_token = jnp.zeros((rows,), jnp.int32).at[dest].set(jnp.arange(TOP_K * t, dtype=jnp.int32) // TOP_K)
    tile_start = jnp.arange(rows // MOE_TILE, dtype=jnp.int32) * MOE_TILE
    tile_expert = jnp.minimum(jnp.sum(tile_start[:, None] >= ends[None, :], axis=1), N_EXPERTS - 1)
    n_used = (ends[-1:] // MOE_TILE).astype(jnp.int32)
    return dest, row_token, tile_expert.astype(jnp.int32), n_used


def _moe_swiglu(x2, norm_g, router, w_gate, w_up, w_down):
    h, route = _norm_route(x2, norm_g, router)
    top_i = route[:, 2:4].astype(jnp.int32)
    dest, row_token, tile_expert, n_used = _routing_tables(top_i)
    xs = _gather_rows(h, row_token)
    cpad = ((0, 0), (0, 0), (0, D_EXPERT_PAD - D_EXPERT))
    wg = jnp.pad(w_gate.astype(BF16), cpad)
    wu = jnp.pad(w_up.astype(BF16), cpad)
    wd = jnp.pad(w_down.astype(BF16), ((0, 0), (0, D_EXPERT_PAD - D_EXPERT), (0, 0)))
    hid = _grouped_call(_moe_up_kernel, tile_expert, n_used, xs, [wg, wu], MOE_UP_BN, BF16, "moe_up")
    y = _grouped_call(_moe_down_kernel, tile_expert, n_used, hid, [wd], MOE_DOWN_BN, F32, "moe_down")
    return _moe_combine(x2, route, y, dest)


def _split_in_weights(w):
    sizes = (1024, 256, 256, 1024, 1024, 1024, 256, 256, 256, 256, 256, 256, 24, 1024, 1024, 1024)
    offs = np.concatenate([[0], np.cumsum(sizes)])
    (a_q, a_k, a_v, b_val, b_gate, c_q, c_kc, c_vc, c_ks, c_vs, c_kw, c_vw, c_g,
     d_q, d_k, d_v) = [w[:, int(offs[i]):int(offs[i + 1])] for i in range(len(sizes))]
    wa = jnp.concatenate([a_q, a_k, a_v, c_q, c_ks, c_vs, c_kw, c_vw, d_q, d_k, d_v], axis=1)
    pad = jnp.zeros((w.shape[0], PF_COLS - PF_G - c_g.shape[1]), w.dtype)
    wf = jnp.concatenate([b_val, b_gate, c_kc, c_vc, c_g, pad], axis=1)
    return wa.astype(BF16), wf.astype(BF16)


def _mixer_layer(x2, batch, seq, layer, p, wbias, cbias):
    h = _rmsnorm(x2, p["attn_norm_g"][layer], out_dtype=BF16, name="attn_norm")
    wa, wf = _split_in_weights(p["w_in"][layer])
    pa = _matmul(h, wa, bm=1024, bn=512, out_dtype=BF16, name="in_proj_attn")
    pf = _matmul(h, wf, bm=1024, bn=PF_COLS // 3, out_dtype=F32, name="in_proj_f32")

    o_a = _window_attention(pa, batch, seq, PA_AQ, PA_AK, PA_AV, wbias, 0,
                            p["swa_sinks"][layer], "swa_attn")
    o_b = _conformer_conv(pf, batch, seq, p["conv_dw"][layer], p["conv_dw_b"][layer],
                          p["conv_ln_g"][layer], p["conv_ln_b"][layer], p["conv_pw"][layer])
    cmp_kv = _compress(pf, batch, seq, p["nsa_cmp_pe"][layer], p["nsa_cmp_w1"][layer],
                       p["nsa_cmp_w2"][layer])
    o_cmp, sel = _cmp_attention(pa, cmp_kv, cbias, batch, seq)
    o_slc = _slc_attention(pa, sel, wbias, p["rel_bias"][:, GROUP_HEADS:], batch, seq)
    o_win = _window_attention(pa, batch, seq, PA_CQ, PA_CKW, PA_CVW, wbias, GROUP_HEADS, None,
                              "nsa_win_attn")
    o_d = _stick_breaking(pa, batch, seq)
    mixed = _mix(o_a, o_b, o_cmp, o_slc, o_win, pf, o_d, p["mix_norm_g"][layer])
    return _matmul(mixed, p["w_out"][layer].astype(BF16), bm=1024, bn=512, out_dtype=F32,
                   residual=x2, name="out_proj")


def kernel(x, attn_norm_g, ffn_norm_g, final_norm_g, w_in, w_out, mix_norm_g, rel_bias,
           swa_sinks, conv_dw, conv_dw_b, conv_ln_g, conv_ln_b, conv_pw, nsa_cmp_pe,
           nsa_cmp_w1, nsa_cmp_w2, ffn_w_gate, ffn_w_up, ffn_w_down, moe_router,
           moe_w_gate, moe_w_up, moe_w_down):
    batch, seq, _ = x.shape
    t = batch * seq
    p = dict(attn_norm_g=attn_norm_g, w_in=w_in, w_out=w_out, mix_norm_g=mix_norm_g,
             rel_bias=rel_bias, swa_sinks=swa_sinks, conv_dw=conv_dw, conv_dw_b=conv_dw_b,
             conv_ln_g=conv_ln_g, conv_ln_b=conv_ln_b, conv_pw=conv_pw, nsa_cmp_pe=nsa_cmp_pe,
             nsa_cmp_w1=nsa_cmp_w1, nsa_cmp_w2=nsa_cmp_w2)
    wbias = _win_bias(rel_bias)
    cbias = _cmp_bias(rel_bias, seq, GROUP_HEADS)
    x2 = x.reshape(t, D_MODEL)
    for layer in range(DEPTH):
        x2 = _mixer_layer(x2, batch, seq, layer, p, wbias, cbias)
        i = layer // 2
        if layer % 2 == 0:
            hf = _rmsnorm(x2, ffn_norm_g[layer], out_dtype=BF16, name="ffn_norm")
            hid = _swiglu_up(hf, ffn_w_gate[i].astype(BF16), ffn_w_up[i].astype(BF16),
                             bm=1024, bn=256, name="ffn_up")
            x2 = _matmul_ksplit_res(hid, ffn_w_down[i].astype(BF16), x2,
                                    bm=1024, bn=512, bk=D_FF // 2, name="ffn_down")
        else:
            x2 = _moe_swiglu(x2, ffn_norm_g[layer], moe_router[i], moe_w_gate[i], moe_w_up[i],
                             moe_w_down[i])
    return _rmsnorm(x2, final_norm_g, out_dtype=F32, name="final_norm").reshape(batch, seq, D_MODEL)
```

```python
import functools
import math

import jax
import jax.numpy as jnp
import numpy as np
from jax import lax
from jax.experimental import pallas as pl
from jax.experimental.pallas import tpu as pltpu

D_MODEL = 4096
DEPTH = 2
HEAD_DIM = 128
N_MIXERS = 4
GROUP_WIDTH = D_MODEL // N_MIXERS
GROUP_HEADS = GROUP_WIDTH // HEAD_DIM
KV_HEADS = 2
REP = GROUP_HEADS // KV_HEADS
WINDOW = 128
Q_BLOCK = 128
CONV_WIDTH = 31
CMP_LEN = 32
CMP_STRIDE = 16
SLC_LEN = 64
N_SELECT = 16
N_BRANCH = 3
N_BUCKETS = 32
MAX_DISTANCE = 128
D_FF = 11008
N_EXPERTS = 8
TOP_K = 2
D_EXPERT = D_FF // 2
NORM_EPS = 1e-6
SCALE = HEAD_DIM ** -0.5

VMEM_LIMIT_BYTES = 56 * 1024 * 1024
LANES = 128

F32 = jnp.float32
BF16 = jnp.bfloat16
NEG_INF = float("-inf")
F32_TINY = float(np.finfo(np.float32).tiny)

PA_AQ, PA_AK, PA_AV = 0, 1024, 1280
PA_CQ, PA_CKS, PA_CVS, PA_CKW, PA_CVW = 1536, 2560, 2816, 3072, 3328
PA_DQ, PA_DK, PA_DV = 3584, 4608, 5632
PA_COLS = 6656
PF_VAL, PF_GATE, PF_KC, PF_VC, PF_G = 0, 1024, 2048, 2304, 2560
PF_COLS = 2688


def _params(*sem):
    return pltpu.CompilerParams(dimension_semantics=sem, vmem_limit_bytes=VMEM_LIMIT_BYTES)


def _bucket_thresholds():
    n = np.arange(0, 4 * MAX_DISTANCE)
    max_exact = N_BUCKETS // 2
    nf = np.maximum(n, 1).astype(np.float32)
    large = max_exact + (np.log(nf / max_exact) / math.log(MAX_DISTANCE / max_exact)
                         * (N_BUCKETS - max_exact)).astype(np.int32)
    large = np.minimum(large, N_BUCKETS - 1)
    bucket = np.where(n < max_exact, n, large)
    return [int(np.argmax(bucket >= k)) for k in range(N_BUCKETS)]


BUCKET_THRESHOLDS = _bucket_thresholds()


def _mm_kernel(x_ref, w_ref, o_ref):
    o_ref[...] = jnp.dot(x_ref[...], w_ref[...], preferred_element_type=F32).astype(o_ref.dtype)


def _mm_res_kernel(x_ref, w_ref, r_ref, o_ref):
    acc = jnp.dot(x_ref[...], w_ref[...], preferred_element_type=F32)
    o_ref[...] = (r_ref[...] + acc).astype(o_ref.dtype)


def _mm_res_ksplit_kernel(x_ref, w_ref, r_ref, o_ref):
    @pl.when(pl.program_id(2) == 0)
    def _():
        o_ref[...] = r_ref[...]

    o_ref[...] += jnp.dot(x_ref[...], w_ref[...], preferred_element_type=F32)


def _matmul(x, w, *, bm, bn, out_dtype, residual=None, name="matmul"):
    m, k = x.shape
    _, n = w.shape
    assert m % bm == 0 and n % bn == 0
    in_specs = [pl.BlockSpec((bm, k), lambda i, j: (i, 0)),
                pl.BlockSpec((k, bn), lambda i, j: (0, j))]
    args = [x, w]
    body = _mm_kernel
    if residual is not None:
        in_specs.append(pl.BlockSpec((bm, bn), lambda i, j: (i, j)))
        args.append(residual)
        body = _mm_res_kernel
    return pl.pallas_call(
        body,
        out_shape=jax.ShapeDtypeStruct((m, n), out_dtype),
        grid=(m // bm, n // bn),
        in_specs=in_specs,
        out_specs=pl.BlockSpec((bm, bn), lambda i, j: (i, j)),
        compiler_params=_params("parallel", "arbitrary"),
        name=name,
    )(*args)


def _matmul_ksplit_res(x, w, residual, *, bm, bn, bk, name="matmul_ksplit"):
    m, k = x.shape
    _, n = w.shape
    assert m % bm == 0 and n % bn == 0 and k % bk == 0
    return pl.pallas_call(
        _mm_res_ksplit_kernel,
        out_shape=jax.ShapeDtypeStruct((m, n), F32),
        grid=(m // bm, n // bn, k // bk),
        in_specs=[pl.BlockSpec((bm, bk), lambda i, j, kk: (i, kk)),
                  pl.BlockSpec((bk, bn), lambda i, j, kk: (kk, j)),
                  pl.BlockSpec((bm, bn), lambda i, j, kk: (i, j))],
        out_specs=pl.BlockSpec((bm, bn), lambda i, j, kk: (i, j)),
        compiler_params=_params("parallel", "arbitrary", "arbitrary"),
        name=name,
    )(x, w, residual)


def _swiglu_kernel(x_ref, wg_ref, wu_ref, o_ref):
    x = x_ref[...]
    g = jnp.dot(x, wg_ref[...], preferred_element_type=F32)
    u = jnp.dot(x, wu_ref[...], preferred_element_type=F32)
    o_ref[...] = (g * jax.nn.sigmoid(g) * u).astype(o_ref.dtype)


def _swiglu_up(x, wg, wu, *, bm, bn, name="swiglu_up"):
    m, k = x.shape
    _, n = wg.shape
    assert m % bm == 0 and n % bn == 0
    return pl.pallas_call(
        _swiglu_kernel,
        out_shape=jax.ShapeDtypeStruct((m, n), BF16),
        grid=(m // bm, n // bn),
        in_specs=[pl.BlockSpec((bm, k), lambda i, j: (i, 0)),
                  pl.BlockSpec((k, bn), lambda i, j: (0, j)),
                  pl.BlockSpec((k, bn), lambda i, j: (0, j))],
        out_specs=pl.BlockSpec((bm, bn), lambda i, j: (i, j)),
        compiler_params=_params("parallel", "arbitrary"),
        name=name,
    )(x, wg, wu)


def _rmsnorm_kernel(x_ref, g_ref, o_ref):
    x = x_ref[...]
    y = x * lax.rsqrt(jnp.mean(x * x, axis=-1, keepdims=True) + NORM_EPS)
    o_ref[...] = (y * g_ref[...]).astype(o_ref.dtype)


def _rmsnorm(x, g, *, out_dtype, bm=256, name="rmsnorm"):
    m, d = x.shape
    return pl.pallas_call(
        _rmsnorm_kernel,
        out_shape=jax.ShapeDtypeStruct((m, d), out_dtype),
        grid=(m // bm,),
        in_specs=[pl.BlockSpec((bm, d), lambda i: (i, 0)),
                  pl.BlockSpec((1, d), lambda i: (0, 0))],
        out_specs=pl.BlockSpec((bm, d), lambda i: (i, 0)),
        compiler_params=_params("parallel"),
        name=name,
    )(x, g.reshape(1, d))


def _bias_of_dist(dist, tab_ref, head):
    out = jnp.full(dist.shape, tab_ref[0, head], F32)
    for k in range(1, N_BUCKETS):
        out = jnp.where(dist >= BUCKET_THRESHOLDS[k], tab_ref[k, head], out)
    return out


def _win_bias_kernel(tab_ref, o_ref):
    h = pl.program_id(0)
    qi = lax.broadcasted_iota(jnp.int32, (Q_BLOCK, 2 * Q_BLOCK), 0)
    kj = lax.broadcasted_iota(jnp.int32, (Q_BLOCK, 2 * Q_BLOCK), 1)
    o_ref[0] = _bias_of_dist(qi + Q_BLOCK - kj, tab_ref, h)


def _win_bias(rel_bias):
    nh = rel_bias.shape[1]
    return pl.pallas_call(
        _win_bias_kernel,
        out_shape=jax.ShapeDtypeStruct((nh, Q_BLOCK, 2 * Q_BLOCK), F32),
        grid=(nh,),
        in_specs=[pl.BlockSpec(memory_space=pltpu.SMEM)],
        out_specs=pl.BlockSpec((1, Q_BLOCK, 2 * Q_BLOCK), lambda h: (h, 0, 0)),
        compiler_params=_params("arbitrary"),
        name="win_bias",
    )(rel_bias)


def _cmp_bias_kernel(tab_ref, o_ref, *, head0, rows):
    h = pl.program_id(0) + head0
    n = pl.program_id(1)
    ncmp = o_ref.shape[2]
    t = n * rows + lax.broadcasted_iota(jnp.int32, (rows, ncmp), 0)
    c = lax.broadcasted_iota(jnp.int32, (rows, ncmp), 1)
    o_ref[0] = _bias_of_dist(t - c * CMP_STRIDE - (CMP_LEN - 1), tab_ref, h)


def _cmp_bias(rel_bias, seq, head0, rows=512):
    ncmp = seq // CMP_STRIDE
    rows = min(rows, seq)
    return pl.pallas_call(
        functools.partial(_cmp_bias_kernel, head0=head0, rows=rows),
        out_shape=jax.ShapeDtypeStruct((GROUP_HEADS, seq, ncmp), F32),
        grid=(GROUP_HEADS, seq // rows),
        in_specs=[pl.BlockSpec(memory_space=pltpu.SMEM)],
        out_specs=pl.BlockSpec((1, rows, ncmp), lambda h, n: (h, n, 0)),
        compiler_params=_params("arbitrary", "arbitrary"),
        name="cmp_bias",
    )(rel_bias)


def _stack_heads(q):
    return jnp.concatenate([q[:, r * HEAD_DIM:(r + 1) * HEAD_DIM] for r in range(REP)], axis=0)


def _store_heads(o_ref, o, rows):
    for r in range(REP):
        o_ref[:, r * HEAD_DIM:(r + 1) * HEAD_DIM] = o[r * rows:(r + 1) * rows].astype(o_ref.dtype)


def _qk(q, k):
    return lax.dot_general(q, k, (((1,), (1,)), ((), ())), preferred_element_type=F32)


def _window_kernel(q_ref, kp_ref, kc_ref, vp_ref, vc_ref, bias_ref, sink_ref, o_ref, *, has_sink):
    n = pl.program_id(2)
    q4 = _stack_heads(q_ref[...])
    kcat = jnp.concatenate([kp_ref[...], kc_ref[...]], axis=0)
    vcat = jnp.concatenate([vp_ref[...], vc_ref[...]], axis=0)
    s = _qk(q4, kcat).reshape(REP, Q_BLOCK, 2 * Q_BLOCK) * SCALE + bias_ref[...]
    qi = lax.broadcasted_iota(jnp.int32, (Q_BLOCK, 2 * Q_BLOCK), 0)
    kj = lax.broadcasted_iota(jnp.int32, (Q_BLOCK, 2 * Q_BLOCK), 1)
    dist = qi + Q_BLOCK - kj
    mask = ((dist >= 0) & (dist < WINDOW) & ((n > 0) | (kj >= Q_BLOCK)))[None]
    s = jnp.where(mask, s, NEG_INF)
    m = jnp.max(s, axis=-1, keepdims=True)
    if has_sink:
        sink = sink_ref[...]
        m = jnp.maximum(m, sink)
    p = jnp.where(mask, jnp.exp(s - m), 0.0)
    den = jnp.sum(p, axis=-1, keepdims=True)
    if has_sink:
        den = den + jnp.exp(sink - m)
    p = p / jnp.maximum(den, F32_TINY)
    o = jnp.dot(p.reshape(REP * Q_BLOCK, 2 * Q_BLOCK).astype(BF16), vcat, preferred_element_type=F32)
    _store_heads(o_ref, o, Q_BLOCK)


def _window_attention(pa, batch, seq, q_col, k_col, v_col, bias, bias_head0, sinks, name):
    nb = seq // Q_BLOCK
    qw = REP * HEAD_DIM
    has_sink = sinks is not None
    sink_arr = (sinks if has_sink else jnp.zeros((GROUP_HEADS,), F32)).reshape(GROUP_HEADS, 1, 1)

    def cur(col):
        return lambda b, g, n: (b * nb + n, col // HEAD_DIM + g)

    def prev(col):
        return lambda b, g, n: (b * nb + jnp.maximum(n - 1, 0), col // HEAD_DIM + g)

    kv_block = (Q_BLOCK, HEAD_DIM)
    return pl.pallas_call(
        functools.partial(_window_kernel, has_sink=has_sink),
        out_shape=jax.ShapeDtypeStruct((batch * seq, GROUP_WIDTH), F32),
        grid=(batch, KV_HEADS, nb),
        in_specs=[pl.BlockSpec((Q_BLOCK, qw), lambda b, g, n: (b * nb + n, q_col // qw + g)),
                  pl.BlockSpec(kv_block, prev(k_col)), pl.BlockSpec(kv_block, cur(k_col)),
                  pl.BlockSpec(kv_block, prev(v_col)), pl.BlockSpec(kv_block, cur(v_col)),
                  pl.BlockSpec((REP, Q_BLOCK, 2 * Q_BLOCK),
                               lambda b, g, n: (bias_head0 // REP + g, 0, 0)),
                  pl.BlockSpec((REP, 1, 1), lambda b, g, n: (g, 0, 0))],
        out_specs=pl.BlockSpec((Q_BLOCK, qw), lambda b, g, n: (b * nb + n, g)),
        compiler_params=_params("parallel", "parallel", "arbitrary"),
        name=name,
    )(pa, pa, pa, pa, pa, bias, sink_arr)


def _compress_kernel(t_ref, pe_ref, w1_ref, w2_ref, o_ref):
    ncmp = o_ref.shape[3]
    half = CMP_LEN // 2
    pe = pe_ref[0]
    rows = [t_ref[pl.ds(r, ncmp, stride=CMP_STRIDE), :] for r in range(CMP_STRIDE)]
    xa = jnp.concatenate([rows[r] + pe[r:r + 1] for r in range(half)], axis=1).astype(BF16)
    xb = jnp.concatenate([rows[r] + pe[half + r:half + r + 1] for r in range(half)], axis=1).astype(BF16)
    kw = half * HEAD_DIM
    p0 = jnp.dot(xa, w1_ref[0, :kw, :], preferred_element_type=F32)
    p1 = jnp.dot(xb, w1_ref[0, kw:, :], preferred_element_type=F32)
    pre = p0 + pltpu.roll(p1, ncmp - 1, 0)
    hid = pre * jax.nn.sigmoid(pre)
    o_ref[0, 0, 0] = jnp.dot(hid.astype(BF16), w2_ref[0], preferred_element_type=F32).astype(o_ref.dtype)


def _compress(pf, batch, seq, pe, w1, w2):
    ncmp = seq // CMP_STRIDE
    return pl.pallas_call(
        _compress_kernel,
        out_shape=jax.ShapeDtypeStruct((2, batch, KV_HEADS, ncmp, HEAD_DIM), BF16),
        grid=(2, batch, KV_HEADS),
        in_specs=[pl.BlockSpec((seq, HEAD_DIM), lambda kv, b, g: (b, PF_KC // HEAD_DIM + KV_HEADS * kv + g)),
                  pl.BlockSpec((1, CMP_LEN, HEAD_DIM), lambda kv, b, g: (kv, 0, 0)),
                  pl.BlockSpec((1, CMP_LEN * HEAD_DIM, HEAD_DIM), lambda kv, b, g: (kv, 0, 0)),
                  pl.BlockSpec((1, HEAD_DIM, HEAD_DIM), lambda kv, b, g: (kv, 0, 0))],
        out_specs=pl.BlockSpec((1, 1, 1, ncmp, HEAD_DIM), lambda kv, b, g: (kv, b, g, 0, 0)),
        compiler_params=_params("arbitrary", "arbitrary", "arbitrary"),
        name="nsa_compress",
    )(pf, pe, w1.astype(BF16), w2.astype(BF16))


def _cmp_attn_kernel(q_ref, kc_ref, vc_ref, bias_ref, ov_ref, o_ref, sel_ref, *, n_sel):
    n = pl.program_id(2)
    ncmp = kc_ref.shape[3]
    nslc = ov_ref.shape[1]
    q4 = _stack_heads(q_ref[...])
    s = _qk(q4, kc_ref[0, 0, 0]).reshape(REP, Q_BLOCK, ncmp) * SCALE + bias_ref[...]
    t = n * Q_BLOCK + lax.broadcasted_iota(jnp.int32, (Q_BLOCK, ncmp), 0)
    c = lax.broadcasted_iota(jnp.int32, (Q_BLOCK, ncmp), 1)
    vis = (t - c * CMP_STRIDE - (CMP_LEN - 1) >= 0)[None]
    s = jnp.where(vis, s, NEG_INF)
    m = jnp.max(s, axis=-1, keepdims=True)
    m = jnp.where(m == NEG_INF, 0.0, m)
    p = jnp.where(vis, jnp.exp(s - m), 0.0)
    den = jnp.sum(p, axis=-1, keepdims=True)
    p = p / jnp.maximum(den, F32_TINY)
    o = jnp.dot(p.reshape(REP * Q_BLOCK, ncmp).astype(BF16), vc_ref[0, 0, 0], preferred_element_type=F32)
    _store_heads(o_ref, o, Q_BLOCK)

    psum = p[0]
    for r in range(1, REP):
        psum = psum + p[r]
    hi = psum.astype(BF16)
    lo = (psum - hi.astype(F32)).astype(BF16)
    ov = ov_ref[...]
    imp = jnp.dot(hi, ov, preferred_element_type=F32) + jnp.dot(lo, ov, preferred_element_type=F32)
    tq = n * Q_BLOCK + lax.broadcasted_iota(jnp.int32, (Q_BLOCK, nslc), 0)
    blk = lax.broadcasted_iota(jnp.int32, (Q_BLOCK, nslc), 1)
    cur = tq // SLC_LEN
    forced = (blk == 0) | (blk == cur) | (blk == cur - 1)
    score = jnp.where(forced, jnp.inf, jnp.where(blk <= cur, imp, NEG_INF))
    rank = jnp.zeros((Q_BLOCK, nslc), F32)
    for k in range(nslc):
        col = score[:, k:k + 1]
        ahead = (col > score) | ((col == score) & (blk > k))
        rank = rank + jnp.where(ahead, 1.0, 0.0)
    sel_ref[0, 0] = jnp.where(rank < n_sel, 1.0, 0.0)


def _overlap_matrix(ncmp, nslc):
    cs = np.arange(ncmp)[:, None] * CMP_STRIDE
    ss = np.arange(nslc)[None, :] * SLC_LEN
    return ((cs < ss + SLC_LEN) & (cs + CMP_LEN > ss)).astype(np.float32)


def _cmp_attention(pa, cmp_kv, cbias, batch, seq):
    nb = seq // Q_BLOCK
    ncmp = seq // CMP_STRIDE
    nslc = seq // SLC_LEN
    qw = REP * HEAD_DIM
    ov = jnp.asarray(_overlap_matrix(ncmp, nslc), BF16)
    return pl.pallas_call(
        functools.partial(_cmp_attn_kernel, n_sel=min(N_SELECT, nslc)),
        out_shape=(jax.ShapeDtypeStruct((batch * seq, GROUP_WIDTH), F32),
                   jax.ShapeDtypeStruct((batch, KV_HEADS, seq, nslc), F32)),
        grid=(batch, KV_HEADS, nb),
        in_specs=[pl.BlockSpec((Q_BLOCK, qw), lambda b, g, n: (b * nb + n, PA_CQ // qw + g)),
                  pl.BlockSpec((1, 1, 1, ncmp, HEAD_DIM), lambda b, g, n: (0, b, g, 0, 0)),
                  pl.BlockSpec((1, 1, 1, ncmp, HEAD_DIM), lambda b, g, n: (1, b, g, 0, 0)),
                  pl.BlockSpec((REP, Q_BLOCK, ncmp), lambda b, g, n: (g, n, 0)),
                  pl.BlockSpec((ncmp, nslc), lambda b, g, n: (0, 0))],
        out_specs=(pl.BlockSpec((Q_BLOCK, qw), lambda b, g, n: (b * nb + n, g)),
                   pl.BlockSpec((1, 1, Q_BLOCK, nslc), lambda b, g, n: (b, g, n, 0))),
        compiler_params=_params("parallel", "parallel", "arbitrary"),
        name="nsa_cmp_attn",
    )(pa, cmp_kv, cmp_kv, cbias, ov)


FAR_TILE = 4 * Q_BLOCK


def _slc_attn_kernel(q_ref, k_ref, v_ref, sel_ref, bias_ref, far_bias_ref, efar_ref, enear_ref, o_ref):
    n = pl.program_id(2)
    q4 = _stack_heads(q_ref[...])
    selb = sel_ref[0, 0].astype(BF16)
    far_bias = far_bias_ref[...]
    rows = REP * Q_BLOCK

    def online(carry, s, mask, v):
        m, l, acc = carry
        s = jnp.where(mask[None], s, NEG_INF)
        m_new = jnp.maximum(m, jnp.max(s, axis=-1, keepdims=True))
        m_safe = jnp.where(m_new == NEG_INF, 0.0, m_new)
        alpha = jnp.exp(m - m_safe)
        p = jnp.exp(s - m_safe)
        l = alpha * l + jnp.sum(p, axis=-1, keepdims=True)
        pv = jnp.dot(p.reshape(rows, p.shape[-1]).astype(BF16), v, preferred_element_type=F32)
        acc = alpha * acc + pv.reshape(REP, Q_BLOCK, HEAD_DIM)
        return m_new, l, acc

    far_end = (n - 1) * Q_BLOCK

    def far_step(j, carry):
        k0 = pl.multiple_of(j * FAR_TILE, FAR_TILE)
        k = k_ref[pl.ds(k0, FAR_TILE), :]
        v = v_ref[pl.ds(k0, FAR_TILE), :]
        s = _qk(q4, k).reshape(REP, Q_BLOCK, FAR_TILE) * SCALE + far_bias
        picked = jnp.dot(selb, efar_ref[j], preferred_element_type=F32) > 0.5
        kidx = k0 + lax.broadcasted_iota(jnp.int32, (Q_BLOCK, FAR_TILE), 1)
        return online(carry, s, picked & (kidx < far_end), v)

    init = (jnp.full((REP, Q_BLOCK, 1), NEG_INF, F32), jnp.zeros((REP, Q_BLOCK, 1), F32),
            jnp.zeros((REP, Q_BLOCK, HEAD_DIM), F32))
    n_far = (jnp.maximum(far_end, 0) + FAR_TILE - 1) // FAR_TILE
    carry = lax.fori_loop(0, n_far, far_step, init)

    pb = jnp.maximum(n - 1, 0)
    p0 = pl.multiple_of(pb * Q_BLOCK, Q_BLOCK)
    c0 = pl.multiple_of(n * Q_BLOCK, Q_BLOCK)
    kcat = jnp.concatenate([k_ref[pl.ds(p0, Q_BLOCK), :], k_ref[pl.ds(c0, Q_BLOCK), :]], axis=0)
    vcat = jnp.concatenate([v_ref[pl.ds(p0, Q_BLOCK), :], v_ref[pl.ds(c0, Q_BLOCK), :]], axis=0)
    s = _qk(q4, kcat).reshape(REP, Q_BLOCK, 2 * Q_BLOCK) * SCALE + bias_ref[...]
    picked = jnp.concatenate(
        [jnp.dot(selb, enear_ref[pb], preferred_element_type=F32),
         jnp.dot(selb, enear_ref[n], preferred_element_type=F32)], axis=1) > 0.5
    qi = lax.broadcasted_iota(jnp.int32, (Q_BLOCK, 2 * Q_BLOCK), 0)
    kj = lax.broadcasted_iota(jnp.int32, (Q_BLOCK, 2 * Q_BLOCK), 1)
    causal = (qi + Q_BLOCK - kj >= 0) & ((n > 0) | (kj >= Q_BLOCK))
    m, l, acc = online(carry, s, picked & causal, vcat)
    o = acc / jnp.maximum(l, F32_TINY)
    for r in range(REP):
        o_ref[:, r * HEAD_DIM:(r + 1) * HEAD_DIM] = o[r].astype(o_ref.dtype)


def _expand_matrix(seq, tile):
    nslc = seq // SLC_LEN
    key = np.arange(seq).reshape(seq // tile, 1, tile)
    return (key // SLC_LEN == np.arange(nslc)[None, :, None]).astype(np.float32)


def _slc_attention(pa, sel, wbias, rel_bias_c, batch, seq):
    nb = seq // Q_BLOCK
    nslc = seq // SLC_LEN
    qw = REP * HEAD_DIM
    nfar = max(seq // FAR_TILE, 1)
    efar = jnp.asarray(_expand_matrix(max(seq, FAR_TILE), FAR_TILE)[:, :nslc], BF16)
    enear = jnp.asarray(_expand_matrix(seq, Q_BLOCK), BF16)
    far_bias = rel_bias_c[N_BUCKETS - 1].reshape(GROUP_HEADS, 1, 1)
    return pl.pallas_call(
        _slc_attn_kernel,
        out_shape=jax.ShapeDtypeStruct((batch * seq, GROUP_WIDTH), F32),
        grid=(batch, KV_HEADS, nb),
        in_specs=[pl.BlockSpec((Q_BLOCK, qw), lambda b, g, n: (b * nb + n, PA_CQ // qw + g)),
                  pl.BlockSpec((seq, HEAD_DIM), lambda b, g, n: (b, PA_CKS // HEAD_DIM + g)),
                  pl.BlockSpec((seq, HEAD_DIM), lambda b, g, n: (b, PA_CVS // HEAD_DIM + g)),
                  pl.BlockSpec((1, 1, Q_BLOCK, nslc), lambda b, g, n: (b, g, n, 0)),
                  pl.BlockSpec((REP, Q_BLOCK, 2 * Q_BLOCK),
                               lambda b, g, n: (GROUP_HEADS // REP + g, 0, 0)),
                  pl.BlockSpec((REP, 1, 1), lambda b, g, n: (g, 0, 0)),
                  pl.BlockSpec((nfar, nslc, FAR_TILE), lambda b, g, n: (0, 0, 0)),
                  pl.BlockSpec((nb, nslc, Q_BLOCK), lambda b, g, n: (0, 0, 0))],
        out_specs=pl.BlockSpec((Q_BLOCK, qw), lambda b, g, n: (b * nb + n, g)),
        compiler_params=_params("parallel", "parallel", "arbitrary"),
        name="nsa_slc_attn",
    )(pa, pa, pa, sel, wbias, far_bias, efar, enear)


SB_TQ = 512
SB_TK = 256
SB_HEADS = 2
LOG2E = math.log2(math.e)


def _sb_kernel(q_ref, k_ref, v_ref, u_ref, o_ref):
    n = pl.program_id(2)
    u2 = u_ref[...]
    ratio = SB_TQ // SB_TK

    def tile(j, carry, masked, hd):
        later, acc = carry
        cols = slice(hd * HEAD_DIM, (hd + 1) * HEAD_DIM)
        k0 = pl.multiple_of(j * SB_TK, SB_TK)
        k = k_ref[pl.ds(k0, SB_TK), cols]
        v = v_ref[pl.ds(k0, SB_TK), cols]
        z = _qk(q_ref[:, cols], k) * (SCALE * LOG2E)
        zneg = jnp.minimum(z, 0.0)
        zpos_neg = zneg - z
        t = jnp.log2(1.0 + jnp.exp2(zneg + zpos_neg))
        log_keep = zpos_neg - t
        if masked:
            qi = n * SB_TQ + lax.broadcasted_iota(jnp.int32, (SB_TQ, SB_TK), 0)
            kj = k0 + lax.broadcasted_iota(jnp.int32, (SB_TQ, SB_TK), 1)
            before = kj < qi
            log_keep = jnp.where(before, log_keep, 0.0)
        hi = log_keep.astype(BF16)
        lo = (log_keep - hi.astype(F32)).astype(BF16)
        suffix = jnp.dot(jnp.concatenate([hi, lo], axis=1), u2, preferred_element_type=F32)
        a = jnp.exp2(z + suffix + later)
        if masked:
            a = jnp.where(before, a, 0.0)
        acc = acc + jnp.dot(a.astype(BF16), v, preferred_element_type=F32)
        later = later + suffix[:, 0:1]
        return later, acc

    def tiles(j, carries, masked):
        return tuple(tile(j, carries[hd], masked, hd) for hd in range(SB_HEADS))

    carries = tuple((jnp.zeros((SB_TQ, 1), F32), jnp.zeros((SB_TQ, HEAD_DIM), F32))
                    for _ in range(SB_HEADS))
    for d in range(ratio):
        carries = tiles(n * ratio + (ratio - 1 - d), carries, True)
    carries = lax.fori_loop(0, n * ratio, lambda i, c: tiles(n * ratio - 1 - i, c, False), carries)
    for hd in range(SB_HEADS):
        o_ref[:, hd * HEAD_DIM:(hd + 1) * HEAD_DIM] = carries[hd][1].astype(o_ref.dtype)


def _stick_breaking(pa, batch, seq):
    assert seq % SB_TQ == 0
    nb = seq // SB_TQ
    hw = SB_HEADS * HEAD_DIM
    tri = np.tril(np.ones((SB_TK, SB_TK), np.float32))
    u2 = jnp.asarray(np.concatenate([tri, tri], axis=0), BF16)
    return pl.pallas_call(
        _sb_kernel,
        out_shape=jax.ShapeDtypeStruct((batch * seq, GROUP_WIDTH), F32),
        grid=(batch, GROUP_HEADS // SB_HEADS, nb),
        in_specs=[pl.BlockSpec((SB_TQ, hw), lambda b, h, n: (b * nb + n, PA_DQ // hw + h)),
                  pl.BlockSpec((seq, hw), lambda b, h, n: (b, PA_DK // hw + h)),
                  pl.BlockSpec((seq, hw), lambda b, h, n: (b, PA_DV // hw + h)),
                  pl.BlockSpec((2 * SB_TK, SB_TK), lambda b, h, n: (0, 0))],
        out_specs=pl.BlockSpec((SB_TQ, hw), lambda b, h, n: (b * nb + n, h)),
        compiler_params=_params("parallel", "parallel", "arbitrary"),
        name="stick_breaking",
    )(pa, pa, pa, u2)


CONV_TILE = 256
CONV_HALO = 32


def _conv_kernel(val_ref, gate_ref, hval_ref, hgate_ref, dw_ref, dwb_ref, lng_ref, lnb_ref, pw_ref,
                 o_ref, ext_ref):
    n = pl.program_id(1)
    halo = hval_ref[...] * jax.nn.sigmoid(hgate_ref[...])
    ext_ref[0:CONV_HALO, :] = jnp.where(n > 0, halo, 0.0)
    ext_ref[CONV_HALO:, :] = val_ref[...] * jax.nn.sigmoid(gate_ref[...])
    first = CONV_HALO - (CONV_WIDTH - 1)
    acc = jnp.zeros((CONV_TILE, GROUP_WIDTH), F32) + dwb_ref[...]
    for w in range(CONV_WIDTH):
        acc = acc + ext_ref[first + w:first + w + CONV_TILE, :] * dw_ref[w:w + 1, :]
    mu = jnp.mean(acc, axis=-1, keepdims=True)
    cen = acc - mu
    var = jnp.mean(cen * cen, axis=-1, keepdims=True)
    y = cen * lax.rsqrt(var + NORM_EPS) * lng_ref[...] + lnb_ref[...]
    y = y * jax.nn.sigmoid(y)
    o_ref[...] = jnp.dot(y.astype(BF16), pw_ref[...], preferred_element_type=F32).astype(o_ref.dtype)


def _conformer_conv(pf, batch, seq, dw, dw_b, ln_g, ln_b, pw):
    nt = seq // CONV_TILE
    hpt = CONV_TILE // CONV_HALO
    c = GROUP_WIDTH
    vec = pl.BlockSpec((1, c), lambda b, n: (0, 0))

    def halo(col):
        return lambda b, n: (jnp.maximum((b * nt + n) * hpt - 1, 0), col // c)

    return pl.pallas_call(
        _conv_kernel,
        out_shape=jax.ShapeDtypeStruct((batch * seq, c), F32),
        grid=(batch, nt),
        in_specs=[pl.BlockSpec((CONV_TILE, c), lambda b, n: (b * nt + n, PF_VAL // c)),
                  pl.BlockSpec((CONV_TILE, c), lambda b, n: (b * nt + n, PF_GATE // c)),
                  pl.BlockSpec((CONV_HALO, c), halo(PF_VAL)),
                  pl.BlockSpec((CONV_HALO, c), halo(PF_GATE)),
                  pl.BlockSpec((CONV_WIDTH, c), lambda b, n: (0, 0)),
                  vec, vec, vec,
                  pl.BlockSpec((c, c), lambda b, n: (0, 0))],
        out_specs=pl.BlockSpec((CONV_TILE, c), lambda b, n: (b * nt + n, 0)),
        scratch_shapes=[pltpu.VMEM((CONV_HALO + CONV_TILE, c), F32)],
        compiler_params=_params("parallel", "arbitrary"),
        name="conformer_conv",
    )(pf, pf, pf, pf, dw, dw_b.reshape(1, c), ln_g.reshape(1, c), ln_b.reshape(1, c), pw.astype(BF16))


MIX_TILE = 256


def _mix_kernel(oa_ref, ob_ref, ocmp_ref, oslc_ref, owin_ref, gl_ref, od_ref, g_ref, o_ref):
    gates = jax.nn.sigmoid(gl_ref[...])

    def norm_store(x, grp):
        y = x * lax.rsqrt(jnp.mean(x * x, axis=-1, keepdims=True) + NORM_EPS)
        sl = slice(grp * GROUP_WIDTH, (grp + 1) * GROUP_WIDTH)
        o_ref[:, sl] = (y * g_ref[:, sl]).astype(o_ref.dtype)

    norm_store(oa_ref[...], 0)
    norm_store(ob_ref[...], 1)
    heads = []
    for h in range(GROUP_HEADS):
        sl = slice(h * HEAD_DIM, (h + 1) * HEAD_DIM)
        c0 = h * N_BRANCH
        heads.append(gates[:, c0:c0 + 1] * ocmp_ref[:, sl] + gates[:, c0 + 1:c0 + 2] * oslc_ref[:, sl]
                     + gates[:, c0 + 2:c0 + 3] * owin_ref[:, sl])
    norm_store(jnp.concatenate(heads, axis=1), 2)
    norm_store(od_ref[...], 3)


def _mix(o_a, o_b, o_cmp, o_slc, o_win, pf, o_d, g):
    t = o_a.shape[0]
    grp = pl.BlockSpec((MIX_TILE, GROUP_WIDTH), lambda i: (i, 0))
    return pl.pallas_call(
        _mix_kernel,
        out_shape=jax.ShapeDtypeStruct((t, D_MODEL), BF16),
        grid=(t // MIX_TILE,),
        in_specs=[grp, grp, grp, grp, grp,
                  pl.BlockSpec((MIX_TILE, LANES), lambda i: (i, PF_G // LANES)),
                  grp,
                  pl.BlockSpec((1, D_MODEL), lambda i: (0, 0))],
        out_specs=pl.BlockSpec((MIX_TILE, D_MODEL), lambda i: (i, 0)),
        compiler_params=_params("parallel"),
        name="mix_norm",
    )(o_a, o_b, o_cmp, o_slc, o_win, pf, o_d, g.reshape(1, D_MODEL))


MOE_TILE = 256
MOE_UP_BN = 512
MOE_DOWN_BN = 512
ROUTE_TILE = 256
GATHER_TILE = 256
COMBINE_TILE = 128


def _norm_route_kernel(x_ref, g_ref, r_ref, h_ref, route_ref):
    x = x_ref[...]
    y = x * lax.rsqrt(jnp.mean(x * x, axis=-1, keepdims=True) + NORM_EPS) * g_ref[...]
    h_ref[...] = y
    logits = jnp.dot(y.astype(BF16), r_ref[...], preferred_element_type=F32)
    lane = lax.broadcasted_iota(jnp.int32, logits.shape, 1)
    logits = jnp.where(lane < N_EXPERTS, logits, NEG_INF)
    e = jnp.exp(logits - jnp.max(logits, axis=-1, keepdims=True))
    probs = e / jnp.sum(e, axis=-1, keepdims=True)
    p1 = jnp.max(probs, axis=-1, keepdims=True)
    i1 = jnp.min(jnp.where(probs == p1, lane, LANES), axis=-1, keepdims=True)
    rest = jnp.where(lane == i1, -1.0, probs)
    p2 = jnp.max(rest, axis=-1, keepdims=True)
    i2 = jnp.min(jnp.where(rest == p2, lane, LANES), axis=-1, keepdims=True)
    tot = p1 + p2
    route_ref[...] = jnp.where(lane == 0, p1 / tot,
                               jnp.where(lane == 1, p2 / tot,
                                         jnp.where(lane == 2, i1.astype(F32),
                                                   jnp.where(lane == 3, i2.astype(F32), 0.0))))


def _norm_route(x, g, router):
    m, d = x.shape
    rpad = jnp.zeros((d, LANES), BF16).at[:, :N_EXPERTS].set(router.astype(BF16))
    return pl.pallas_call(
        _norm_route_kernel,
        out_shape=(jax.ShapeDtypeStruct((m, d), F32), jax.ShapeDtypeStruct((m, LANES), F32)),
        grid=(m // ROUTE_TILE,),
        in_specs=[pl.BlockSpec((ROUTE_TILE, d), lambda i: (i, 0)),
                  pl.BlockSpec((1, d), lambda i: (0, 0)),
                  pl.BlockSpec((d, LANES), lambda i: (0, 0))],
        out_specs=(pl.BlockSpec((ROUTE_TILE, d), lambda i: (i, 0)),
                   pl.BlockSpec((ROUTE_TILE, LANES), lambda i: (i, 0))),
        compiler_params=_params("parallel"),
        name="ffn_norm_route",
    )(x, g.reshape(1, d), rpad)


def _row_copy(src_hbm, dst_ref, sem, src_row, dst_row):
    return pltpu.make_async_copy(src_hbm.at[pl.ds(src_row, 1), :], dst_ref.at[pl.ds(dst_row, 1), :], sem)


def _gather_kernel(tok_ref, h_hbm, o_ref, buf_ref, sem):
    i = pl.program_id(0)

    def issue(tile, slot):
        def start(r, c):
            _row_copy(h_hbm, buf_ref.at[slot], sem.at[slot], tok_ref[tile * GATHER_TILE + r], r).start()
            return c
        lax.fori_loop(0, GATHER_TILE, start, 0)

    @pl.when(i == 0)
    def _():
        issue(0, 0)

    @pl.when(i + 1 < pl.num_programs(0))
    def _():
        issue(i + 1, (i + 1) % 2)

    slot = i % 2

    def wait(r, c):
        _row_copy(h_hbm, buf_ref.at[slot], sem.at[slot], 0, r).wait()
        return c

    lax.fori_loop(0, GATHER_TILE, wait, 0)
    o_ref[...] = buf_ref[slot].astype(o_ref.dtype)


def _gather_rows(h, row_token):
    rows = row_token.shape[0]
    d = h.shape[1]
    return pl.pallas_call(
        _gather_kernel,
        out_shape=jax.ShapeDtypeStruct((rows, d), BF16),
        grid_spec=pltpu.PrefetchScalarGridSpec(
            num_scalar_prefetch=1,
            grid=(rows // GATHER_TILE,),
            in_specs=[pl.BlockSpec(memory_space=pl.ANY)],
            out_specs=pl.BlockSpec((GATHER_TILE, d), lambda i, tok: (i, 0)),
            scratch_shapes=[pltpu.VMEM((2, GATHER_TILE, d), F32), pltpu.SemaphoreType.DMA((2,))]),
        compiler_params=_params("arbitrary"),
        name="moe_gather",
    )(row_token, h)


CAST_ROWS = 128


def _cast_weight_tile(w_ref, s_ref):
    def body(c, carry):
        r0 = pl.multiple_of(c * CAST_ROWS, CAST_ROWS)
        s_ref[pl.ds(r0, CAST_ROWS), :] = w_ref[0, pl.ds(r0, CAST_ROWS), :].astype(BF16)
        return carry

    lax.fori_loop(0, s_ref.shape[0] // CAST_ROWS, body, 0)


def _grouped_prologue(te_ref, nu_ref, w_refs, s_refs):
    i = pl.program_id(1)
    last = nu_ref[0] - 1
    cur = te_ref[jnp.minimum(i, last)]
    prev = te_ref[jnp.minimum(jnp.maximum(i - 1, 0), last)]

    @pl.when((i == 0) | (cur != prev))
    def _():
        for w_ref, s_ref in zip(w_refs, s_refs):
            _cast_weight_tile(w_ref, s_ref)

    return i < nu_ref[0]


def _moe_up_kernel(te_ref, nu_ref, x_ref, wg_ref, wu_ref, o_ref, sg_ref, su_ref):
    live = _grouped_prologue(te_ref, nu_ref, (wg_ref, wu_ref), (sg_ref, su_ref))

    @pl.when(live)
    def _():
        x = x_ref[...]
        g = jnp.dot(x, sg_ref[...], preferred_element_type=F32)
        u = jnp.dot(x, su_ref[...], preferred_element_type=F32)
        o_ref[...] = (g * jax.nn.sigmoid(g) * u).astype(o_ref.dtype)

    @pl.when(jnp.logical_not(live))
    def _():
        o_ref[...] = jnp.zeros_like(o_ref)


def _moe_down_kernel(te_ref, nu_ref, x_ref, w_ref, o_ref, s_ref):
    live = _grouped_prologue(te_ref, nu_ref, (w_ref,), (s_ref,))

    @pl.when(live)
    def _():
        o_ref[...] = jnp.dot(x_ref[...], s_ref[...], preferred_element_type=F32)

    @pl.when(jnp.logical_not(live))
    def _():
        o_ref[...] = jnp.zeros_like(o_ref)


def _grouped_call(body, tile_expert, n_used, xs, weights, bn, out_dtype, name):
    rows, k = xs.shape
    n = weights[0].shape[2]
    nt = rows // MOE_TILE
    assert k % CAST_ROWS == 0

    def used(i, nu):
        return jnp.minimum(i, nu[0] - 1)

    return pl.pallas_call(
        body,
        out_shape=jax.ShapeDtypeStruct((rows, n), out_dtype),
        grid_spec=pltpu.PrefetchScalarGridSpec(
            num_scalar_prefetch=2,
            grid=(pl.cdiv(n, bn), nt),
            in_specs=[pl.BlockSpec((MOE_TILE, k), lambda j, i, te, nu: (used(i, nu), 0))]
            + [pl.BlockSpec((1, k, bn), lambda j, i, te, nu: (te[used(i, nu)], 0, j))
               for _ in weights],
            out_specs=pl.BlockSpec((MOE_TILE, bn), lambda j, i, te, nu: (i, j)),
            scratch_shapes=[pltpu.VMEM((k, bn), BF16) for _ in weights]),
        compiler_params=_params("arbitrary", "arbitrary"),
        name=name,
    )(tile_expert, n_used, xs, *weights)


def _combine_kernel(dest_ref, x_ref, route_ref, y_hbm, o_ref, y0_ref, y1_ref, sem):
    i = pl.program_id(0)

    def issue(tile, slot):
        def start(r, c):
            tok = tile * COMBINE_TILE + r
            _row_copy(y_hbm, y0_ref.at[slot], sem.at[slot], dest_ref[2 * tok], r).start()
            _row_copy(y_hbm, y1_ref.at[slot], sem.at[slot], dest_ref[2 * tok + 1], r).start()
            return c
        lax.fori_loop(0, COMBINE_TILE, start, 0)

    @pl.when(i == 0)
    def _():
        issue(0, 0)

    @pl.when(i + 1 < pl.num_programs(0))
    def _():
        issue(i + 1, (i + 1) % 2)

    slot = i % 2

    def wait(r, c):
        _row_copy(y_hbm, y0_ref.at[slot], sem.at[slot], 0, r).wait()
        _row_copy(y_hbm, y1_ref.at[slot], sem.at[slot], 0, r).wait()
        return c

    lax.fori_loop(0, COMBINE_TILE, wait, 0)
    route = route_ref[...]
    o_ref[...] = x_ref[...] + route[:, 0:1] * y0_ref[slot] + route[:, 1:2] * y1_ref[slot]


def _moe_combine(x2, route, y, dest):
    t, d = x2.shape
    return pl.pallas_call(
        _combine_kernel,
        out_shape=jax.ShapeDtypeStruct((t, d), F32),
        grid_spec=pltpu.PrefetchScalarGridSpec(
            num_scalar_prefetch=1,
            grid=(t // COMBINE_TILE,),
            in_specs=[pl.BlockSpec((COMBINE_TILE, d), lambda i, dest: (i, 0)),
                      pl.BlockSpec((COMBINE_TILE, LANES), lambda i, dest: (i, 0)),
                      pl.BlockSpec(memory_space=pl.ANY)],
            out_specs=pl.BlockSpec((COMBINE_TILE, d), lambda i, dest: (i, 0)),
            scratch_shapes=[pltpu.VMEM((2, COMBINE_TILE, d), F32), pltpu.VMEM((2, COMBINE_TILE, d), F32),
                            pltpu.SemaphoreType.DMA((2,))]),
        compiler_params=_params("arbitrary"),
        name="moe_combine",
    )(dest, x2, route, y)


def _routing_tables(top_i):
    t = top_i.shape[0]
    e_flat = top_i.reshape(-1)
    onehot = (e_flat[:, None] == jnp.arange(N_EXPERTS, dtype=jnp.int32)[None, :]).astype(jnp.int32)
    csum = jnp.cumsum(onehot, axis=0)
    counts = csum[-1]
    pos = jnp.take_along_axis(csum, e_flat[:, None], axis=1)[:, 0] - 1
    padded = ((counts + MOE_TILE - 1) // MOE_TILE) * MOE_TILE
    ends = jnp.cumsum(padded)
    dest = ((ends - padded)[e_flat] + pos).astype(jnp.int32)
    rows = TOP_K * t + N_EXPERTS * MOE_TILE
    row_token = jnp.zeros((rows,), jnp.int32).at[dest].set(jnp.arange(TOP_K * t, dtype=jnp.int32) // TOP_K)
    tile_start = jnp.arange(rows // MOE_TILE, dtype=jnp.int32) * MOE_TILE
    tile_expert = jnp.minimum(jnp.sum(tile_start[:, None] >= ends[None, :], axis=1), N_EXPERTS - 1)
    n_used = (ends[-1:] // MOE_TILE).astype(jnp.int32)
    return dest, row_token, tile_expert.astype(jnp.int32), n_used


def _moe_swiglu(x2, norm_g, router, w_gate, w_up, w_down):
    h, route = _norm_route(x2, norm_g, router)
    top_i = route[:, 2:4].astype(jnp.int32)
    dest, row_token, tile_expert, n_used = _routing_tables(top_i)
    xs = _gather_rows(h, row_token)
    hid = _grouped_call(_moe_up_kernel, tile_expert, n_used, xs, [w_gate, w_up], MOE_UP_BN, BF16,
                        "moe_up")
    y = _grouped_call(_moe_down_kernel, tile_expert, n_used, hid, [w_down], MOE_DOWN_BN, F32,
                      "moe_down")
    return _moe_combine(x2, route, y, dest)


def _split_in_weights(w):
    sizes = (1024, 256, 256, 1024, 1024, 1024, 256, 256, 256, 256, 256, 256, 24, 1024, 1024, 1024)
    offs = np.concatenate([[0], np.cumsum(sizes)])
    (a_q, a_k, a_v, b_val, b_gate, c_q, c_kc, c_vc, c_ks, c_vs, c_kw, c_vw, c_g,
     d_q, d_k, d_v) = [w[:, int(offs[i]):int(offs[i + 1])] for i in range(len(sizes))]
    wa = jnp.concatenate([a_q, a_k, a_v, c_q, c_ks, c_vs, c_kw, c_vw, d_q, d_k, d_v], axis=1)
    pad = jnp.zeros((w.shape[0], PF_COLS - PF_G - c_g.shape[1]), w.dtype)
    wf = jnp.concatenate([b_val, b_gate, c_kc, c_vc, c_g, pad], axis=1)
    return wa.astype(BF16), wf.astype(BF16)


def _mixer_layer(x2, batch, seq, layer, p, wbias, cbias):
    h = _rmsnorm(x2, p["attn_norm_g"][layer], out_dtype=BF16, name="attn_norm")
    wa, wf = _split_in_weights(p["w_in"][layer])
    pa = _matmul(h, wa, bm=1024, bn=512, out_dtype=BF16, name="in_proj_attn")
    pf = _matmul(h, wf, bm=1024, bn=PF_COLS // 3, out_dtype=F32, name="in_proj_f32")

    o_a = _window_attention(pa, batch, seq, PA_AQ, PA_AK, PA_AV, wbias, 0,
                            p["swa_sinks"][layer], "swa_attn")
    o_b = _conformer_conv(pf, batch, seq, p["conv_dw"][layer], p["conv_dw_b"][layer],
                          p["conv_ln_g"][layer], p["conv_ln_b"][layer], p["conv_pw"][layer])
    cmp_kv = _compress(pf, batch, seq, p["nsa_cmp_pe"][layer], p["nsa_cmp_w1"][layer],
                       p["nsa_cmp_w2"][layer])
    o_cmp, sel = _cmp_attention(pa, cmp_kv, cbias, batch, seq)
    o_slc = _slc_attention(pa, sel, wbias, p["rel_bias"][:, GROUP_HEADS:], batch, seq)
    o_win = _window_attention(pa, batch, seq, PA_CQ, PA_CKW, PA_CVW, wbias, GROUP_HEADS, None,
                              "nsa_win_attn")
    o_d = _stick_breaking(pa, batch, seq)
    mixed = _mix(o_a, o_b, o_cmp, o_slc, o_win, pf, o_d, p["mix_norm_g"][layer])
    return _matmul(mixed, p["w_out"][layer].astype(BF16), bm=1024, bn=512, out_dtype=F32,
                   residual=x2, name="out_proj")


def kernel(x, attn_norm_g, ffn_norm_g, final_norm_g, w_in, w_out, mix_norm_g, rel_bias,
           swa_sinks, conv_dw, conv_dw_b, conv_ln_g, conv_ln_b, conv_pw, nsa_cmp_pe,
           nsa_cmp_w1, nsa_cmp_w2, ffn_w_gate, ffn_w_up, ffn_w_down, moe_router,
           moe_w_gate, moe_w_up, moe_w_down):
    batch, seq, _ = x.shape
    t = batch * seq
    p = dict(attn_norm_g=attn_norm_g, w_in=w_in, w_out=w_out, mix_norm_g=mix_norm_g,
             rel_bias=rel_bias, swa_sinks=swa_sinks, conv_dw=conv_dw, conv_dw_b=conv_dw_b,
             conv_ln_g=conv_ln_g, conv_ln_b=conv_ln_b, conv_pw=conv_pw, nsa_cmp_pe=nsa_cmp_pe,
             nsa_cmp_w1=nsa_cmp_w1, nsa_cmp_w2=nsa_cmp_w2)
    wbias = _win_bias(rel_bias)
    cbias = _cmp_bias(rel_bias, seq, GROUP_HEADS)
    x2 = x.reshape(t, D_MODEL)
    for layer in range(DEPTH):
        x2 = _mixer_layer(x2, batch, seq, layer, p, wbias, cbias)
        i = layer // 2
        if layer % 2 == 0:
            hf = _rmsnorm(x2, ffn_norm_g[layer], out_dtype=BF16, name="ffn_norm")
            hid = _swiglu_up(hf, ffn_w_gate[i].astype(BF16), ffn_w_up[i].astype(BF16),
                             bm=1024, bn=256, name="ffn_up")
            x2 = _matmul_ksplit_res(hid, ffn_w_down[i].astype(BF16), x2,
                                    bm=1024, bn=512, bk=D_FF // 2, name="ffn_down")
        else:
            x2 = _moe_swiglu(x2, ffn_norm_g[layer], moe_router[i], moe_w_gate[i], moe_w_up[i],
                             moe_w_down[i])
    return _rmsnorm(x2, final_norm_g, out_dtype=F32, name="final_norm").reshape(batch, seq, D_MODEL)
```

```python
import functools
import math

import jax
import jax.numpy as jnp
import numpy as np
from jax import lax
from jax.experimental import pallas as pl
from jax.experimental.pallas import tpu as pltpu

D_MODEL = 4096
DEPTH = 2
HEAD_DIM = 128
N_MIXERS = 4
GROUP_WIDTH = D_MODEL // N_MIXERS
GROUP_HEADS = GROUP_WIDTH // HEAD_DIM
KV_HEADS = 2
REP = GROUP_HEADS // KV_HEADS
WINDOW = 128
Q_BLOCK = 128
CONV_WIDTH = 31
CMP_LEN = 32
CMP_STRIDE = 16
SLC_LEN = 64
N_SELECT = 16
N_BRANCH = 3
N_BUCKETS = 32
MAX_DISTANCE = 128
D_FF = 11008
N_EXPERTS = 8
TOP_K = 2
D_EXPERT = D_FF // 2
NORM_EPS = 1e-6
SCALE = HEAD_DIM ** -0.5

VMEM_LIMIT_BYTES = 56 * 1024 * 1024
LANES = 128

F32 = jnp.float32
BF16 = jnp.bfloat16
NEG_INF = float("-inf")
F32_TINY = float(np.finfo(np.float32).tiny)

PA_AQ, PA_AK, PA_AV = 0, 1024, 1280
PA_CQ, PA_CKS, PA_CVS, PA_CKW, PA_CVW = 1536, 2560, 2816, 3072, 3328
PA_DQ, PA_DK, PA_DV = 3584, 4608, 5632
PA_COLS = 6656
PF_VAL, PF_GATE, PF_KC, PF_VC, PF_G = 0, 1024, 2048, 2304, 2560
PF_COLS = 2688


def _params(*sem):
    return pltpu.CompilerParams(dimension_semantics=sem, vmem_limit_bytes=VMEM_LIMIT_BYTES)


def _bucket_thresholds():
    n = np.arange(0, 4 * MAX_DISTANCE)
    max_exact = N_BUCKETS // 2
    nf = np.maximum(n, 1).astype(np.float32)
    large = max_exact + (np.log(nf / max_exact) / math.log(MAX_DISTANCE / max_exact)
                         * (N_BUCKETS - max_exact)).astype(np.int32)
    large = np.minimum(large, N_BUCKETS - 1)
    bucket = np.where(n < max_exact, n, large)
    return [int(np.argmax(bucket >= k)) for k in range(N_BUCKETS)]


BUCKET_THRESHOLDS = _bucket_thresholds()


def _mm_kernel(x_ref, w_ref, o_ref):
    o_ref[...] = jnp.dot(x_ref[...], w_ref[...], preferred_element_type=F32).astype(o_ref.dtype)


def _mm_res_kernel(x_ref, w_ref, r_ref, o_ref):
    acc = jnp.dot(x_ref[...], w_ref[...], preferred_element_type=F32)
    o_ref[...] = (r_ref[...] + acc).astype(o_ref.dtype)


def _mm_res_ksplit_kernel(x_ref, w_ref, r_ref, o_ref):
    @pl.when(pl.program_id(2) == 0)
    def _():
        o_ref[...] = r_ref[...]

    o_ref[...] += jnp.dot(x_ref[...], w_ref[...], preferred_element_type=F32)


def _matmul(x, w, *, bm, bn, out_dtype, residual=None, name="matmul"):
    m, k = x.shape
    _, n = w.shape
    assert m % bm == 0 and n % bn == 0
    in_specs = [pl.BlockSpec((bm, k), lambda i, j: (i, 0)),
                pl.BlockSpec((k, bn), lambda i, j: (0, j))]
    args = [x, w]
    body = _mm_kernel
    if residual is not None:
        in_specs.append(pl.BlockSpec((bm, bn), lambda i, j: (i, j)))
        args.append(residual)
        body = _mm_res_kernel
    return pl.pallas_call(
        body,
        out_shape=jax.ShapeDtypeStruct((m, n), out_dtype),
        grid=(m // bm, n // bn),
        in_specs=in_specs,
        out_specs=pl.BlockSpec((bm, bn), lambda i, j: (i, j)),
        compiler_params=_params("parallel", "arbitrary"),
        name=name,
    )(*args)


def _matmul_ksplit_res(x, w, residual, *, bm, bn, bk, name="matmul_ksplit"):
    m, k = x.shape
    _, n = w.shape
    assert m % bm == 0 and n % bn == 0 and k % bk == 0
    return pl.pallas_call(
        _mm_res_ksplit_kernel,
        out_shape=jax.ShapeDtypeStruct((m, n), F32),
        grid=(m // bm, n // bn, k // bk),
        in_specs=[pl.BlockSpec((bm, bk), lambda i, j, kk: (i, kk)),
                  pl.BlockSpec((bk, bn), lambda i, j, kk: (kk, j)),
                  pl.BlockSpec((bm, bn), lambda i, j, kk: (i, j))],
        out_specs=pl.BlockSpec((bm, bn), lambda i, j, kk: (i, j)),
        compiler_params=_params("parallel", "arbitrary", "arbitrary"),
        name=name,
    )(x, w, residual)


def _swiglu_kernel(x_ref, wg_ref, wu_ref, o_ref):
    x = x_ref[...]
    g = jnp.dot(x, wg_ref[...], preferred_element_type=F32)
    u = jnp.dot(x, wu_ref[...], preferred_element_type=F32)
    o_ref[...] = (g * jax.nn.sigmoid(g) * u).astype(o_ref.dtype)


def _swiglu_up(x, wg, wu, *, bm, bn, name="swiglu_up"):
    m, k = x.shape
    _, n = wg.shape
    assert m % bm == 0 and n % bn == 0
    return pl.pallas_call(
        _swiglu_kernel,
        out_shape=jax.ShapeDtypeStruct((m, n), BF16),
        grid=(m // bm, n // bn),
        in_specs=[pl.BlockSpec((bm, k), lambda i, j: (i, 0)),
                  pl.BlockSpec((k, bn), lambda i, j: (0, j)),
                  pl.BlockSpec((k, bn), lambda i, j: (0, j))],
        out_specs=pl.BlockSpec((bm, bn), lambda i, j: (i, j)),
        compiler_params=_params("parallel", "arbitrary"),
        name=name,
    )(x, wg, wu)


def _rmsnorm_kernel(x_ref, g_ref, o_ref):
    x = x_ref[...]
    y = x * lax.rsqrt(jnp.mean(x * x, axis=-1, keepdims=True) + NORM_EPS)
    o_ref[...] = (y * g_ref[...]).astype(o_ref.dtype)


def _rmsnorm(x, g, *, out_dtype, bm=256, name="rmsnorm"):
    m, d = x.shape
    return pl.pallas_call(
        _rmsnorm_kernel,
        out_shape=jax.ShapeDtypeStruct((m, d), out_dtype),
        grid=(m // bm,),
        in_specs=[pl.BlockSpec((bm, d), lambda i: (i, 0)),
                  pl.BlockSpec((1, d), lambda i: (0, 0))],
        out_specs=pl.BlockSpec((bm, d), lambda i: (i, 0)),
        compiler_params=_params("parallel"),
        name=name,
    )(x, g.reshape(1, d))


def _bias_of_dist(dist, tab_ref, head):
    out = jnp.full(dist.shape, tab_ref[0, head], F32)
    for k in range(1, N_BUCKETS):
        out = jnp.where(dist >= BUCKET_THRESHOLDS[k], tab_ref[k, head], out)
    return out


def _win_bias_kernel(tab_ref, o_ref):
    h = pl.program_id(0)
    qi = lax.broadcasted_iota(jnp.int32, (Q_BLOCK, 2 * Q_BLOCK), 0)
    kj = lax.broadcasted_iota(jnp.int32, (Q_BLOCK, 2 * Q_BLOCK), 1)
    o_ref[0] = _bias_of_dist(qi + Q_BLOCK - kj, tab_ref, h)


def _win_bias(rel_bias):
    nh = rel_bias.shape[1]
    return pl.pallas_call(
        _win_bias_kernel,
        out_shape=jax.ShapeDtypeStruct((nh, Q_BLOCK, 2 * Q_BLOCK), F32),
        grid=(nh,),
        in_specs=[pl.BlockSpec(memory_space=pltpu.SMEM)],
        out_specs=pl.BlockSpec((1, Q_BLOCK, 2 * Q_BLOCK), lambda h: (h, 0, 0)),
        compiler_params=_params("arbitrary"),
        name="win_bias",
    )(rel_bias)


def _cmp_bias_kernel(tab_ref, o_ref, *, head0, rows):
    h = pl.program_id(0) + head0
    n = pl.program_id(1)
    ncmp = o_ref.shape[2]
    t = n * rows + lax.broadcasted_iota(jnp.int32, (rows, ncmp), 0)
    c = lax.broadcasted_iota(jnp.int32, (rows, ncmp), 1)
    o_ref[0] = _bias_of_dist(t - c * CMP_STRIDE - (CMP_LEN - 1), tab_ref, h)


def _cmp_bias(rel_bias, seq, head0, rows=512):
    ncmp = seq // CMP_STRIDE
    rows = min(rows, seq)
    return pl.pallas_call(
        functools.partial(_cmp_bias_kernel, head0=head0, rows=rows),
        out_shape=jax.ShapeDtypeStruct((GROUP_HEADS, seq, ncmp), F32),
        grid=(GROUP_HEADS, seq // rows),
        in_specs=[pl.BlockSpec(memory_space=pltpu.SMEM)],
        out_specs=pl.BlockSpec((1, rows, ncmp), lambda h, n: (h, n, 0)),
        compiler_params=_params("arbitrary", "arbitrary"),
        name="cmp_bias",
    )(rel_bias)


def _stack_heads(q):
    return jnp.concatenate([q[:, r * HEAD_DIM:(r + 1) * HEAD_DIM] for r in range(REP)], axis=0)


def _store_heads(o_ref, o, rows):
    for r in range(REP):
        o_ref[:, r * HEAD_DIM:(r + 1) * HEAD_DIM] = o[r * rows:(r + 1) * rows].astype(o_ref.dtype)


def _qk(q, k):
    return lax.dot_general(q, k, (((1,), (1,)), ((), ())), preferred_element_type=F32)


def _window_kernel(q_ref, kp_ref, kc_ref, vp_ref, vc_ref, bias_ref, sink_ref, o_ref, *, has_sink):
    n = pl.program_id(2)
    q4 = _stack_heads(q_ref[...])
    kcat = jnp.concatenate([kp_ref[...], kc_ref[...]], axis=0)
    vcat = jnp.concatenate([vp_ref[...], vc_ref[...]], axis=0)
    s = _qk(q4, kcat).reshape(REP, Q_BLOCK, 2 * Q_BLOCK) * SCALE + bias_ref[...]
    qi = lax.broadcasted_iota(jnp.int32, (Q_BLOCK, 2 * Q_BLOCK), 0)
    kj = lax.broadcasted_iota(jnp.int32, (Q_BLOCK, 2 * Q_BLOCK), 1)
    dist = qi + Q_BLOCK - kj
    mask = ((dist >= 0) & (dist < WINDOW) & ((n > 0) | (kj >= Q_BLOCK)))[None]
    s = jnp.where(mask, s, NEG_INF)
    m = jnp.max(s, axis=-1, keepdims=True)
    if has_sink:
        sink = sink_ref[...]
        m = jnp.maximum(m, sink)
    p = jnp.where(mask, jnp.exp(s - m), 0.0)
    den = jnp.sum(p, axis=-1, keepdims=True)
    if has_sink:
        den = den + jnp.exp(sink - m)
    p = p / jnp.maximum(den, F32_TINY)
    o = jnp.dot(p.reshape(REP * Q_BLOCK, 2 * Q_BLOCK).astype(BF16), vcat, preferred_element_type=F32)
    _store_heads(o_ref, o, Q_BLOCK)


def _window_attention(pa, batch, seq, q_col, k_col, v_col, bias, bias_head0, sinks, name):
    nb = seq // Q_BLOCK
    qw = REP * HEAD_DIM
    has_sink = sinks is not None
    sink_arr = (sinks if has_sink else jnp.zeros((GROUP_HEADS,), F32)).reshape(GROUP_HEADS, 1, 1)

    def cur(col):
        return lambda b, g, n: (b * nb + n, col // HEAD_DIM + g)

    def prev(col):
        return lambda b, g, n: (b * nb + jnp.maximum(n - 1, 0), col // HEAD_DIM + g)

    kv_block = (Q_BLOCK, HEAD_DIM)
    return pl.pallas_call(
        functools.partial(_window_kernel, has_sink=has_sink),
        out_shape=jax.ShapeDtypeStruct((batch * seq, GROUP_WIDTH), F32),
        grid=(batch, KV_HEADS, nb),
        in_specs=[pl.BlockSpec((Q_BLOCK, qw), lambda b, g, n: (b * nb + n, q_col // qw + g)),
                  pl.BlockSpec(kv_block, prev(k_col)), pl.BlockSpec(kv_block, cur(k_col)),
                  pl.BlockSpec(kv_block, prev(v_col)), pl.BlockSpec(kv_block, cur(v_col)),
                  pl.BlockSpec((REP, Q_BLOCK, 2 * Q_BLOCK),
                               lambda b, g, n: (bias_head0 // REP + g, 0, 0)),
                  pl.BlockSpec((REP, 1, 1), lambda b, g, n: (g, 0, 0))],
        out_specs=pl.BlockSpec((Q_BLOCK, qw), lambda b, g, n: (b * nb + n, g)),
        compiler_params=_params("parallel", "parallel", "arbitrary"),
        name=name,
    )(pa, pa, pa, pa, pa, bias, sink_arr)


def _compress_kernel(t_ref, pe_ref, w1_ref, w2_ref, o_ref):
    ncmp = o_ref.shape[3]
    half = CMP_LEN // 2
    pe = pe_ref[0]
    rows = [t_ref[pl.ds(r, ncmp, stride=CMP_STRIDE), :] for r in range(CMP_STRIDE)]
    xa = jnp.concatenate([rows[r] + pe[r:r + 1] for r in range(half)], axis=1).astype(BF16)
    xb = jnp.concatenate([rows[r] + pe[half + r:half + r + 1] for r in range(half)], axis=1).astype(BF16)
    kw = half * HEAD_DIM
    p0 = jnp.dot(xa, w1_ref[0, :kw, :], preferred_element_type=F32)
    p1 = jnp.dot(xb, w1_ref[0, kw:, :], preferred_element_type=F32)
    pre = p0 + pltpu.roll(p1, ncmp - 1, 0)
    hid = pre * jax.nn.sigmoid(pre)
    o_ref[0, 0, 0] = jnp.dot(hid.astype(BF16), w2_ref[0], preferred_element_type=F32).astype(o_ref.dtype)


def _compress(pf, batch, seq, pe, w1, w2):
    ncmp = seq // CMP_STRIDE
    return pl.pallas_call(
        _compress_kernel,
        out_shape=jax.ShapeDtypeStruct((2, batch, KV_HEADS, ncmp, HEAD_DIM), BF16),
        grid=(2, batch, KV_HEADS),
        in_specs=[pl.BlockSpec((seq, HEAD_DIM), lambda kv, b, g: (b, PF_KC // HEAD_DIM + KV_HEADS * kv + g)),
                  pl.BlockSpec((1, CMP_LEN, HEAD_DIM), lambda kv, b, g: (kv, 0, 0)),
                  pl.BlockSpec((1, CMP_LEN * HEAD_DIM, HEAD_DIM), lambda kv, b, g: (kv, 0, 0)),
                  pl.BlockSpec((1, HEAD_DIM, HEAD_DIM), lambda kv, b, g: (kv, 0, 0))],
        out_specs=pl.BlockSpec((1, 1, 1, ncmp, HEAD_DIM), lambda kv, b, g: (kv, b, g, 0, 0)),
        compiler_params=_params("arbitrary", "arbitrary", "arbitrary"),
        name="nsa_compress",
    )(pf, pe, w1.astype(BF16), w2.astype(BF16))


def _cmp_attn_kernel(q_ref, kc_ref, vc_ref, bias_ref, ov_ref, o_ref, sel_ref, *, n_sel):
    n = pl.program_id(2)
    ncmp = kc_ref.shape[3]
    nslc = ov_ref.shape[1]
    q4 = _stack_heads(q_ref[...])
    s = _qk(q4, kc_ref[0, 0, 0]).reshape(REP, Q_BLOCK, ncmp) * SCALE + bias_ref[...]
    t = n * Q_BLOCK + lax.broadcasted_iota(jnp.int32, (Q_BLOCK, ncmp), 0)
    c = lax.broadcasted_iota(jnp.int32, (Q_BLOCK, ncmp), 1)
    vis = (t - c * CMP_STRIDE - (CMP_LEN - 1) >= 0)[None]
    s = jnp.where(vis, s, NEG_INF)
    m = jnp.max(s, axis=-1, keepdims=True)
    m = jnp.where(m == NEG_INF, 0.0, m)
    p = jnp.where(vis, jnp.exp(s - m), 0.0)
    den = jnp.sum(p, axis=-1, keepdims=True)
    p = p / jnp.maximum(den, F32_TINY)
    o = jnp.dot(p.reshape(REP * Q_BLOCK, ncmp).astype(BF16), vc_ref[0, 0, 0], preferred_element_type=F32)
    _store_heads(o_ref, o, Q_BLOCK)

    psum = p[0]
    for r in range(1, REP):
        psum = psum + p[r]
    hi = psum.astype(BF16)
    lo = (psum - hi.astype(F32)).astype(BF16)
    ov = ov_ref[...]
    imp = jnp.dot(hi, ov, preferred_element_type=F32) + jnp.dot(lo, ov, preferred_element_type=F32)
    tq = n * Q_BLOCK + lax.broadcasted_iota(jnp.int32, (Q_BLOCK, nslc), 0)
    blk = lax.broadcasted_iota(jnp.int32, (Q_BLOCK, nslc), 1)
    cur = jnp.right_shift(tq, SLC_LEN.bit_length() - 1)
    forced = (blk == 0) | (blk == cur) | (blk == cur - 1)
    score = jnp.where(forced, jnp.inf, jnp.where(blk <= cur, imp, NEG_INF))
    rank = jnp.zeros((Q_BLOCK, nslc), F32)
    for k in range(nslc):
        col = score[:, k:k + 1]
        ahead = (col > score) | ((col == score) & (blk > k))
        rank = rank + jnp.where(ahead, 1.0, 0.0)
    sel_ref[0, 0] = jnp.where(rank < n_sel, 1.0, 0.0)


def _overlap_matrix(ncmp, nslc):
    cs = np.arange(ncmp)[:, None] * CMP_STRIDE
    ss = np.arange(nslc)[None, :] * SLC_LEN
    return ((cs < ss + SLC_LEN) & (cs + CMP_LEN > ss)).astype(np.float32)


def _cmp_attention(pa, cmp_kv, cbias, batch, seq):
    nb = seq // Q_BLOCK
    ncmp = seq // CMP_STRIDE
    nslc = seq // SLC_LEN
    qw = REP * HEAD_DIM
    ov = jnp.asarray(_overlap_matrix(ncmp, nslc), BF16)
    return pl.pallas_call(
        functools.partial(_cmp_attn_kernel, n_sel=min(N_SELECT, nslc)),
        out_shape=(jax.ShapeDtypeStruct((batch * seq, GROUP_WIDTH), F32),
                   jax.ShapeDtypeStruct((batch, KV_HEADS, seq, nslc), F32)),
        grid=(batch, KV_HEADS, nb),
        in_specs=[pl.BlockSpec((Q_BLOCK, qw), lambda b, g, n: (b * nb + n, PA_CQ // qw + g)),
                  pl.BlockSpec((1, 1, 1, ncmp, HEAD_DIM), lambda b, g, n: (0, b, g, 0, 0)),
                  pl.BlockSpec((1, 1, 1, ncmp, HEAD_DIM), lambda b, g, n: (1, b, g, 0, 0)),
                  pl.BlockSpec((REP, Q_BLOCK, ncmp), lambda b, g, n: (g, n, 0)),
                  pl.BlockSpec((ncmp, nslc), lambda b, g, n: (0, 0))],
        out_specs=(pl.BlockSpec((Q_BLOCK, qw), lambda b, g, n: (b * nb + n, g)),
                   pl.BlockSpec((1, 1, Q_BLOCK, nslc), lambda b, g, n: (b, g, n, 0))),
        compiler_params=_params("parallel", "parallel", "arbitrary"),
        name="nsa_cmp_attn",
    )(pa, cmp_kv, cmp_kv, cbias, ov)


FAR_TILE = 4 * Q_BLOCK


def _slc_attn_kernel(q_ref, k_ref, v_ref, sel_ref, bias_ref, far_bias_ref, efar_ref, enear_ref, o_ref):
    n = pl.program_id(2)
    q4 = _stack_heads(q_ref[...])
    selb = sel_ref[0, 0].astype(BF16)
    far_bias = far_bias_ref[...]
    rows = REP * Q_BLOCK

    def online(carry, s, mask, v):
        m, l, acc = carry
        s = jnp.where(mask[None], s, NEG_INF)
        m_new = jnp.maximum(m, jnp.max(s, axis=-1, keepdims=True))
        m_safe = jnp.where(m_new == NEG_INF, 0.0, m_new)
        alpha = jnp.exp(m - m_safe)
        p = jnp.exp(s - m_safe)
        l = alpha * l + jnp.sum(p, axis=-1, keepdims=True)
        pv = jnp.dot(p.reshape(rows, p.shape[-1]).astype(BF16), v, preferred_element_type=F32)
        acc = alpha * acc + pv.reshape(REP, Q_BLOCK, HEAD_DIM)
        return m_new, l, acc

    far_end = (n - 1) * Q_BLOCK

    def far_step(j, carry):
        k0 = pl.multiple_of(j * FAR_TILE, FAR_TILE)
        k = k_ref[pl.ds(k0, FAR_TILE), :]
        v = v_ref[pl.ds(k0, FAR_TILE), :]
        s = _qk(q4, k).reshape(REP, Q_BLOCK, FAR_TILE) * SCALE + far_bias
        picked = jnp.dot(selb, efar_ref[j], preferred_element_type=F32) > 0.5
        kidx = k0 + lax.broadcasted_iota(jnp.int32, (Q_BLOCK, FAR_TILE), 1)
        return online(carry, s, picked & (kidx < far_end), v)

    init = (jnp.full((REP, Q_BLOCK, 1), NEG_INF, F32), jnp.zeros((REP, Q_BLOCK, 1), F32),
            jnp.zeros((REP, Q_BLOCK, HEAD_DIM), F32))
    n_far = (jnp.maximum(far_end, 0) + FAR_TILE - 1) // FAR_TILE
    carry = lax.fori_loop(0, n_far, far_step, init)

    pb = jnp.maximum(n - 1, 0)
    p0 = pl.multiple_of(pb * Q_BLOCK, Q_BLOCK)
    c0 = pl.multiple_of(n * Q_BLOCK, Q_BLOCK)
    kcat = jnp.concatenate([k_ref[pl.ds(p0, Q_BLOCK), :], k_ref[pl.ds(c0, Q_BLOCK), :]], axis=0)
    vcat = jnp.concatenate([v_ref[pl.ds(p0, Q_BLOCK), :], v_ref[pl.ds(c0, Q_BLOCK), :]], axis=0)
    s = _qk(q4, kcat).reshape(REP, Q_BLOCK, 2 * Q_BLOCK) * SCALE + bias_ref[...]
    picked = jnp.concatenate(
        [jnp.dot(selb, enear_ref[pb], preferred_element_type=F32),
         jnp.dot(selb, enear_ref[n], preferred_element_type=F32)], axis=1) > 0.5
    qi = lax.broadcasted_iota(jnp.int32, (Q_BLOCK, 2 * Q_BLOCK), 0)
    kj = lax.broadcasted_iota(jnp.int32, (Q_BLOCK, 2 * Q_BLOCK), 1)
    causal = (qi + Q_BLOCK - kj >= 0) & ((n > 0) | (kj >= Q_BLOCK))
    m, l, acc = online(carry, s, picked & causal, vcat)
    o = acc / jnp.maximum(l, F32_TINY)
    for r in range(REP):
        o_ref[:, r * HEAD_DIM:(r + 1) * HEAD_DIM] = o[r].astype(o_ref.dtype)


def _expand_matrix(seq, tile):
    nslc = seq // SLC_LEN
    key = np.arange(seq).reshape(seq // tile, 1, tile)
    return (key // SLC_LEN == np.arange(nslc)[None, :, None]).astype(np.float32)


def _slc_attention(pa, sel, wbias, rel_bias_c, batch, seq):
    nb = seq // Q_BLOCK
    nslc = seq // SLC_LEN
    qw = REP * HEAD_DIM
    nfar = max(seq // FAR_TILE, 1)
    efar = jnp.asarray(_expand_matrix(max(seq, FAR_TILE), FAR_TILE)[:, :nslc], BF16)
    enear = jnp.asarray(_expand_matrix(seq, Q_BLOCK), BF16)
    far_bias = rel_bias_c[N_BUCKETS - 1].reshape(GROUP_HEADS, 1, 1)
    return pl.pallas_call(
        _slc_attn_kernel,
        out_shape=jax.ShapeDtypeStruct((batch * seq, GROUP_WIDTH), F32),
        grid=(batch, KV_HEADS, nb),
        in_specs=[pl.BlockSpec((Q_BLOCK, qw), lambda b, g, n: (b * nb + n, PA_CQ // qw + g)),
                  pl.BlockSpec((seq, HEAD_DIM), lambda b, g, n: (b, PA_CKS // HEAD_DIM + g)),
                  pl.BlockSpec((seq, HEAD_DIM), lambda b, g, n: (b, PA_CVS // HEAD_DIM + g)),
                  pl.BlockSpec((1, 1, Q_BLOCK, nslc), lambda b, g, n: (b, g, n, 0)),
                  pl.BlockSpec((REP, Q_BLOCK, 2 * Q_BLOCK),
                               lambda b, g, n: (GROUP_HEADS // REP + g, 0, 0)),
                  pl.BlockSpec((REP, 1, 1), lambda b, g, n: (g, 0, 0)),
                  pl.BlockSpec((nfar, nslc, FAR_TILE), lambda b, g, n: (0, 0, 0)),
                  pl.BlockSpec((nb, nslc, Q_BLOCK), lambda b, g, n: (0, 0, 0))],
        out_specs=pl.BlockSpec((Q_BLOCK, qw), lambda b, g, n: (b * nb + n, g)),
        compiler_params=_params("parallel", "parallel", "arbitrary"),
        name="nsa_slc_attn",
    )(pa, pa, pa, sel, wbias, far_bias, efar, enear)


SB_TK = 256
SB_BLOCKS = 4
SB_STEP = SB_BLOCKS * SB_TK
LOG2E = math.log2(math.e)


def _sb_kernel(q_ref, k_ref, v_ref, u_ref, o_ref):
    n = pl.program_id(2)
    u2 = u_ref[...]
    q = q_ref[...]

    def step(j, carry, masked):
        later, acc = carry
        k0 = pl.multiple_of(j * SB_STEP, SB_STEP)
        k = k_ref[pl.ds(k0, SB_STEP), :]
        v = v_ref[pl.ds(k0, SB_STEP), :]
        z = _qk(q, k) * (SCALE * LOG2E)
        zneg = jnp.minimum(z, 0.0)
        zpos_neg = zneg - z
        t = jnp.log2(1.0 + jnp.exp2(zneg + zpos_neg))
        log_keep = zpos_neg - t
        if masked:
            qi = lax.broadcasted_iota(jnp.int32, (SB_STEP, SB_STEP), 0)
            kj = lax.broadcasted_iota(jnp.int32, (SB_STEP, SB_STEP), 1)
            before = kj < qi
            log_keep = jnp.where(before, log_keep, 0.0)
        hi = log_keep.astype(BF16)
        lo = (log_keep - hi.astype(F32)).astype(BF16)
        blocks = [jnp.concatenate([hi[:, b * SB_TK:(b + 1) * SB_TK], lo[:, b * SB_TK:(b + 1) * SB_TK]],
                                  axis=1) for b in range(SB_BLOCKS)]
        suffix = jnp.dot(jnp.concatenate(blocks, axis=0), u2, preferred_element_type=F32)
        parts = [None] * SB_BLOCKS
        for b in range(SB_BLOCKS - 1, -1, -1):
            sfx = suffix[b * SB_STEP:(b + 1) * SB_STEP]
            parts[b] = jnp.exp2(z[:, b * SB_TK:(b + 1) * SB_TK] + sfx + later)
            later = later + sfx[:, 0:1]
        a = jnp.concatenate(parts, axis=1)
        if masked:
            a = jnp.where(before, a, 0.0)
        acc = acc + jnp.dot(a.astype(BF16), v, preferred_element_type=F32)
        return later, acc

    carry = (jnp.zeros((SB_STEP, 1), F32), jnp.zeros((SB_STEP, HEAD_DIM), F32))
    carry = step(n, carry, True)
    carry = lax.fori_loop(0, n, lambda i, c: step(n - 1 - i, c, False), carry)
    o_ref[...] = carry[1].astype(o_ref.dtype)


def _stick_breaking(pa, batch, seq):
    assert seq % SB_STEP == 0
    nb = seq // SB_STEP
    tri = np.tril(np.ones((SB_TK, SB_TK), np.float32))
    u2 = jnp.asarray(np.concatenate([tri, tri], axis=0), BF16)
    return pl.pallas_call(
        _sb_kernel,
        out_shape=jax.ShapeDtypeStruct((batch * seq, GROUP_WIDTH), F32),
        grid=(batch, GROUP_HEADS, nb),
        in_specs=[pl.BlockSpec((SB_STEP, HEAD_DIM), lambda b, h, n: (b * nb + n, PA_DQ // HEAD_DIM + h)),
                  pl.BlockSpec((seq, HEAD_DIM), lambda b, h, n: (b, PA_DK // HEAD_DIM + h)),
                  pl.BlockSpec((seq, HEAD_DIM), lambda b, h, n: (b, PA_DV // HEAD_DIM + h)),
                  pl.BlockSpec((2 * SB_TK, SB_TK), lambda b, h, n: (0, 0))],
        out_specs=pl.BlockSpec((SB_STEP, HEAD_DIM), lambda b, h, n: (b * nb + n, h)),
        compiler_params=_params("parallel", "parallel", "arbitrary"),
        name="stick_breaking",
    )(pa, pa, pa, u2)


CONV_TILE = 256
CONV_HALO = 32


def _conv_kernel(val_ref, gate_ref, hval_ref, hgate_ref, dw_ref, dwb_ref, lng_ref, lnb_ref, pw_ref,
                 o_ref, ext_ref):
    n = pl.program_id(1)
    halo = hval_ref[...] * jax.nn.sigmoid(hgate_ref[...])
    ext_ref[0:CONV_HALO, :] = jnp.where(n > 0, halo, 0.0)
    ext_ref[CONV_HALO:, :] = val_ref[...] * jax.nn.sigmoid(gate_ref[...])
    first = CONV_HALO - (CONV_WIDTH - 1)
    acc = jnp.zeros((CONV_TILE, GROUP_WIDTH), F32) + dwb_ref[...]
    for w in range(CONV_WIDTH):
        acc = acc + ext_ref[first + w:first + w + CONV_TILE, :] * dw_ref[w:w + 1, :]
    mu = jnp.mean(acc, axis=-1, keepdims=True)
    cen = acc - mu
    var = jnp.mean(cen * cen, axis=-1, keepdims=True)
    y = cen * lax.rsqrt(var + NORM_EPS) * lng_ref[...] + lnb_ref[...]
    y = y * jax.nn.sigmoid(y)
    o_ref[...] = jnp.dot(y.astype(BF16), pw_ref[...], preferred_element_type=F32).astype(o_ref.dtype)


def _conformer_conv(pf, batch, seq, dw, dw_b, ln_g, ln_b, pw):
    nt = seq // CONV_TILE
    hpt = CONV_TILE // CONV_HALO
    c = GROUP_WIDTH
    vec = pl.BlockSpec((1, c), lambda b, n: (0, 0))

    def halo(col):
        return lambda b, n: (jnp.maximum((b * nt + n) * hpt - 1, 0), col // c)

    return pl.pallas_call(
        _conv_kernel,
        out_shape=jax.ShapeDtypeStruct((batch * seq, c), F32),
        grid=(batch, nt),
        in_specs=[pl.BlockSpec((CONV_TILE, c), lambda b, n: (b * nt + n, PF_VAL // c)),
                  pl.BlockSpec((CONV_TILE, c), lambda b, n: (b * nt + n, PF_GATE // c)),
                  pl.BlockSpec((CONV_HALO, c), halo(PF_VAL)),
                  pl.BlockSpec((CONV_HALO, c), halo(PF_GATE)),
                  pl.BlockSpec((CONV_WIDTH, c), lambda b, n: (0, 0)),
                  vec, vec, vec,
                  pl.BlockSpec((c, c), lambda b, n: (0, 0))],
        out_specs=pl.BlockSpec((CONV_TILE, c), lambda b, n: (b * nt + n, 0)),
        scratch_shapes=[pltpu.VMEM((CONV_HALO + CONV_TILE, c), F32)],
        compiler_params=_params("parallel", "arbitrary"),
        name="conformer_conv",
    )(pf, pf, pf, pf, dw, dw_b.reshape(1, c), ln_g.reshape(1, c), ln_b.reshape(1, c), pw.astype(BF16))


MIX_TILE = 256


def _mix_kernel(oa_ref, ob_ref, ocmp_ref, oslc_ref, owin_ref, gl_ref, od_ref, g_ref, o_ref):
    gates = jax.nn.sigmoid(gl_ref[...])

    def norm_store(x, grp):
        y = x * lax.rsqrt(jnp.mean(x * x, axis=-1, keepdims=True) + NORM_EPS)
        sl = slice(grp * GROUP_WIDTH, (grp + 1) * GROUP_WIDTH)
        o_ref[:, sl] = (y * g_ref[:, sl]).astype(o_ref.dtype)

    norm_store(oa_ref[...], 0)
    norm_store(ob_ref[...], 1)
    heads = []
    for h in range(GROUP_HEADS):
        sl = slice(h * HEAD_DIM, (h + 1) * HEAD_DIM)
        c0 = h * N_BRANCH
        heads.append(gates[:, c0:c0 + 1] * ocmp_ref[:, sl] + gates[:, c0 + 1:c0 + 2] * oslc_ref[:, sl]
                     + gates[:, c0 + 2:c0 + 3] * owin_ref[:, sl])
    norm_store(jnp.concatenate(heads, axis=1), 2)
    norm_store(od_ref[...], 3)


def _mix(o_a, o_b, o_cmp, o_slc, o_win, pf, o_d, g):
    t = o_a.shape[0]
    grp = pl.BlockSpec((MIX_TILE, GROUP_WIDTH), lambda i: (i, 0))
    return pl.pallas_call(
        _mix_kernel,
        out_shape=jax.ShapeDtypeStruct((t, D_MODEL), BF16),
        grid=(t // MIX_TILE,),
        in_specs=[grp, grp, grp, grp, grp,
                  pl.BlockSpec((MIX_TILE, LANES), lambda i: (i, PF_G // LANES)),
                  grp,
                  pl.BlockSpec((1, D_MODEL), lambda i: (0, 0))],
        out_specs=pl.BlockSpec((MIX_TILE, D_MODEL), lambda i: (i, 0)),
        compiler_params=_params("parallel"),
        name="mix_norm",
    )(o_a, o_b, o_cmp, o_slc, o_win, pf, o_d, g.reshape(1, D_MODEL))


MOE_TILE = 512
MOE_UP_BN = 512
MOE_DOWN_BN = 512
ROUTE_TILE = 256
GATHER_TILE = 256
COMBINE_TILE = 128


def _norm_route_kernel(x_ref, g_ref, r_ref, h_ref, route_ref):
    x = x_ref[...]
    y = x * lax.rsqrt(jnp.mean(x * x, axis=-1, keepdims=True) + NORM_EPS) * g_ref[...]
    h_ref[...] = y
    logits = jnp.dot(y.astype(BF16), r_ref[...], preferred_element_type=F32)
    lane = lax.broadcasted_iota(jnp.int32, logits.shape, 1)
    logits = jnp.where(lane < N_EXPERTS, logits, NEG_INF)
    e = jnp.exp(logits - jnp.max(logits, axis=-1, keepdims=True))
    probs = e / jnp.sum(e, axis=-1, keepdims=True)
    p1 = jnp.max(probs, axis=-1, keepdims=True)
    i1 = jnp.min(jnp.where(probs == p1, lane, LANES), axis=-1, keepdims=True)
    rest = jnp.where(lane == i1, -1.0, probs)
    p2 = jnp.max(rest, axis=-1, keepdims=True)
    i2 = jnp.min(jnp.where(rest == p2, lane, LANES), axis=-1, keepdims=True)
    tot = p1 + p2
    route_ref[...] = jnp.where(lane == 0, p1 / tot,
                               jnp.where(lane == 1, p2 / tot,
                                         jnp.where(lane == 2, i1.astype(F32),
                                                   jnp.where(lane == 3, i2.astype(F32), 0.0))))


def _norm_route(x, g, router):
    m, d = x.shape
    rpad = jnp.zeros((d, LANES), BF16).at[:, :N_EXPERTS].set(router.astype(BF16))
    return pl.pallas_call(
        _norm_route_kernel,
        out_shape=(jax.ShapeDtypeStruct((m, d), F32), jax.ShapeDtypeStruct((m, LANES), F32)),
        grid=(m // ROUTE_TILE,),
        in_specs=[pl.BlockSpec((ROUTE_TILE, d), lambda i: (i, 0)),
                  pl.BlockSpec((1, d), lambda i: (0, 0)),
                  pl.BlockSpec((d, LANES), lambda i: (0, 0))],
        out_specs=(pl.BlockSpec((ROUTE_TILE, d), lambda i: (i, 0)),
                   pl.BlockSpec((ROUTE_TILE, LANES), lambda i: (i, 0))),
        compiler_params=_params("parallel"),
        name="ffn_norm_route",
    )(x, g.reshape(1, d), rpad)


def _row_copy(src_hbm, dst_ref, sem, src_row, dst_row):
    return pltpu.make_async_copy(src_hbm.at[pl.ds(src_row, 1), :], dst_ref.at[pl.ds(dst_row, 1), :], sem)


def _gather_kernel(tok_ref, nu_ref, h_hbm, o_ref, buf_ref, sem):
    i = pl.program_id(0)
    n_live = nu_ref[0] * (MOE_TILE // GATHER_TILE)

    def issue(tile, slot):
        def start(r, c):
            _row_copy(h_hbm, buf_ref.at[slot], sem.at[slot], tok_ref[tile * GATHER_TILE + r], r).start()
            return c
        lax.fori_loop(0, GATHER_TILE, start, 0)

    @pl.when(i == 0)
    def _():
        issue(0, 0)

    @pl.when(i + 1 < n_live)
    def _():
        issue(i + 1, (i + 1) % 2)

    @pl.when(i < n_live)
    def _():
        slot = i % 2

        def wait(r, c):
            _row_copy(h_hbm, buf_ref.at[slot], sem.at[slot], 0, r).wait()
            return c

        lax.fori_loop(0, GATHER_TILE, wait, 0)
        o_ref[...] = buf_ref[slot].astype(o_ref.dtype)

    @pl.when(i >= n_live)
    def _():
        o_ref[...] = jnp.zeros_like(o_ref)


def _gather_rows(h, row_token, n_used):
    rows = row_token.shape[0]
    d = h.shape[1]
    return pl.pallas_call(
        _gather_kernel,
        out_shape=jax.ShapeDtypeStruct((rows, d), BF16),
        grid_spec=pltpu.PrefetchScalarGridSpec(
            num_scalar_prefetch=2,
            grid=(rows // GATHER_TILE,),
            in_specs=[pl.BlockSpec(memory_space=pl.ANY)],
            out_specs=pl.BlockSpec((GATHER_TILE, d), lambda i, tok, nu: (i, 0)),
            scratch_shapes=[pltpu.VMEM((2, GATHER_TILE, d), F32), pltpu.SemaphoreType.DMA((2,))]),
        compiler_params=_params("arbitrary"),
        name="moe_gather",
    )(row_token, n_used, h)


CAST_ROWS = 128


def _cast_weight_tile(w_ref, s_ref):
    def body(c, carry):
        r0 = pl.multiple_of(c * CAST_ROWS, CAST_ROWS)
        s_ref[pl.ds(r0, CAST_ROWS), :] = w_ref[0, pl.ds(r0, CAST_ROWS), :].astype(BF16)
        return carry

    lax.fori_loop(0, s_ref.shape[0] // CAST_ROWS, body, 0)


def _grouped_prologue(te_ref, nu_ref, w_refs, s_refs):
    i = pl.program_id(1)
    last = nu_ref[0] - 1
    cur = te_ref[jnp.minimum(i, last)]
    prev = te_ref[jnp.minimum(jnp.maximum(i - 1, 0), last)]

    @pl.when((i == 0) | (cur != prev))
    def _():
        for w_ref, s_ref in zip(w_refs, s_refs):
            _cast_weight_tile(w_ref, s_ref)

    return i < nu_ref[0]


def _moe_up_kernel(te_ref, nu_ref, x_ref, wg_ref, wu_ref, o_ref, sg_ref, su_ref):
    live = _grouped_prologue(te_ref, nu_ref, (wg_ref, wu_ref), (sg_ref, su_ref))

    @pl.when(live)
    def _():
        x = x_ref[...]
        g = jnp.dot(x, sg_ref[...], preferred_element_type=F32)
        u = jnp.dot(x, su_ref[...], preferred_element_type=F32)
        o_ref[...] = (g * jax.nn.sigmoid(g) * u).astype(o_ref.dtype)

    @pl.when(jnp.logical_not(live))
    def _():
        o_ref[...] = jnp.zeros_like(o_ref)


def _moe_down_kernel(te_ref, nu_ref, x_ref, w_ref, o_ref, s_ref):
    live = _grouped_prologue(te_ref, nu_ref, (w_ref,), (s_ref,))

    @pl.when(live)
    def _():
        o_ref[...] = jnp.dot(x_ref[...], s_ref[...], preferred_element_type=F32)

    @pl.when(jnp.logical_not(live))
    def _():
        o_ref[...] = jnp.zeros_like(o_ref)


def _grouped_call(body, tile_expert, n_used, xs, weights, bn, out_dtype, name):
    rows, k = xs.shape
    n = weights[0].shape[2]
    nt = rows // MOE_TILE
    assert k % CAST_ROWS == 0

    def used(i, nu):
        return jnp.minimum(i, nu[0] - 1)

    return pl.pallas_call(
        body,
        out_shape=jax.ShapeDtypeStruct((rows, n), out_dtype),
        grid_spec=pltpu.PrefetchScalarGridSpec(
            num_scalar_prefetch=2,
            grid=(pl.cdiv(n, bn), nt),
            in_specs=[pl.BlockSpec((MOE_TILE, k), lambda j, i, te, nu: (used(i, nu), 0))]
            + [pl.BlockSpec((1, k, bn), lambda j, i, te, nu: (te[used(i, nu)], 0, j))
               for _ in weights],
            out_specs=pl.BlockSpec((MOE_TILE, bn), lambda j, i, te, nu: (i, j)),
            scratch_shapes=[pltpu.VMEM((k, bn), BF16) for _ in weights]),
        compiler_params=_params("arbitrary", "arbitrary"),
        name=name,
    )(tile_expert, n_used, xs, *weights)


def _combine_kernel(dest_ref, x_ref, route_ref, y_hbm, o_ref, y0_ref, y1_ref, sem):
    i = pl.program_id(0)

    def issue(tile, slot):
        def start(r, c):
            tok = tile * COMBINE_TILE + r
            _row_copy(y_hbm, y0_ref.at[slot], sem.at[slot], dest_ref[2 * tok], r).start()
            _row_copy(y_hbm, y1_ref.at[slot], sem.at[slot], dest_ref[2 * tok + 1], r).start()
            return c
        lax.fori_loop(0, COMBINE_TILE, start, 0)

    @pl.when(i == 0)
    def _():
        issue(0, 0)

    @pl.when(i + 1 < pl.num_programs(0))
    def _():
        issue(i + 1, (i + 1) % 2)

    slot = i % 2

    def wait(r, c):
        _row_copy(y_hbm, y0_ref.at[slot], sem.at[slot], 0, r).wait()
        _row_copy(y_hbm, y1_ref.at[slot], sem.at[slot], 0, r).wait()
        return c

    lax.fori_loop(0, COMBINE_TILE, wait, 0)
    route = route_ref[...]
    o_ref[...] = x_ref[...] + route[:, 0:1] * y0_ref[slot] + route[:, 1:2] * y1_ref[slot]


def _moe_combine(x2, route, y, dest):
    t, d = x2.shape
    return pl.pallas_call(
        _combine_kernel,
        out_shape=jax.ShapeDtypeStruct((t, d), F32),
        grid_spec=pltpu.PrefetchScalarGridSpec(
            num_scalar_prefetch=1,
            grid=(t // COMBINE_TILE,),
            in_specs=[pl.BlockSpec((COMBINE_TILE, d), lambda i, dest: (i, 0)),
                      pl.BlockSpec((COMBINE_TILE, LANES), lambda i, dest: (i, 0)),
                      pl.BlockSpec(memory_space=pl.ANY)],
            out_specs=pl.BlockSpec((COMBINE_TILE, d), lambda i, dest: (i, 0)),
            scratch_shapes=[pltpu.VMEM((2, COMBINE_TILE, d), F32), pltpu.VMEM((2, COMBINE_TILE, d), F32),
                            pltpu.SemaphoreType.DMA((2,))]),
        compiler_params=_params("arbitrary"),
        name="moe_combine",
    )(dest, x2, route, y)


def _routing_tables(top_i):
    t = top_i.shape[0]
    e_flat = top_i.reshape(-1)
    onehot = (e_flat[:, None] == jnp.arange(N_EXPERTS, dtype=jnp.int32)[None, :]).astype(jnp.int32)
    csum = jnp.cumsum(onehot, axis=0)
    counts = csum[-1]
    pos = jnp.take_along_axis(csum, e_flat[:, None], axis=1)[:, 0] - 1
    padded = ((counts + MOE_TILE - 1) // MOE_TILE) * MOE_TILE
    ends = jnp.cumsum(padded)
    dest = ((ends - padded)[e_flat] + pos).astype(jnp.int32)
    rows = TOP_K * t + N_EXPERTS * MOE_TILE
    row_token = jnp.zeros((rows,), jnp.int32).at[dest].set(jnp.arange(TOP_K * t, dtype=jnp.int32) // TOP_K)
    tile_start = jnp.arange(rows // MOE_TILE, dtype=jnp.int32) * MOE_TILE
    tile_expert = jnp.minimum(jnp.sum(tile_start[:, None] >= ends[None, :], axis=1), N_EXPERTS - 1)
    n_used = (ends[-1:] // MOE_TILE).astype(jnp.int32)
    return dest, row_token, tile_expert.astype(jnp.int32), n_used


def _moe_swiglu(x2, norm_g, router, w_gate, w_up, w_down):
    h, route = _norm_route(x2, norm_g, router)
    top_i = route[:, 2:4].astype(jnp.int32)
    dest, row_token, tile_expert, n_used = _routing_tables(top_i)
    xs = _gather_rows(h, row_token, n_used)
    hid = _grouped_call(_moe_up_kernel, tile_expert, n_used, xs, [w_gate, w_up], MOE_UP_BN, BF16,
                        "moe_up")
    y = _grouped_call(_moe_down_kernel, tile_expert, n_used, hid, [w_down], MOE_DOWN_BN, F32,
                      "moe_down")
    return _moe_combine(x2, route, y, dest)


def _split_in_weights(w):
    sizes = (1024, 256, 256, 1024, 1024, 1024, 256, 256, 256, 256, 256, 256, 24, 1024, 1024, 1024)
    offs = np.concatenate([[0], np.cumsum(sizes)])
    (a_q, a_k, a_v, b_val, b_gate, c_q, c_kc, c_vc, c_ks, c_vs, c_kw, c_vw, c_g,
     d_q, d_k, d_v) = [w[:, int(offs[i]):int(offs[i + 1])] for i in range(len(sizes))]
    wa = jnp.concatenate([a_q, a_k, a_v, c_q, c_ks, c_vs, c_kw, c_vw, d_q, d_k, d_v], axis=1)
    pad = jnp.zeros((w.shape[0], PF_COLS - PF_G - c_g.shape[1]), w.dtype)
    wf = jnp.concatenate([b_val, b_gate, c_kc, c_vc, c_g, pad], axis=1)
    return wa.astype(BF16), wf.astype(BF16)


def _mixer_layer(x2, batch, seq, layer, p, wbias, cbias):
    h = _rmsnorm(x2, p["attn_norm_g"][layer], out_dtype=BF16, name="attn_norm")
    wa, wf = _split_in_weights(p["w_in"][layer])
    pa = _matmul(h, wa, bm=1024, bn=512, out_dtype=BF16, name="in_proj_attn")
    pf = _matmul(h, wf, bm=1024, bn=PF_COLS // 3, out_dtype=F32, name="in_proj_f32")

    o_a = _window_attention(pa, batch, seq, PA_AQ, PA_AK, PA_AV, wbias, 0,
                            p["swa_sinks"][layer], "swa_attn")
    o_b = _conformer_conv(pf, batch, seq, p["conv_dw"][layer], p["conv_dw_b"][layer],
                          p["conv_ln_g"][layer], p["conv_ln_b"][layer], p["conv_pw"][layer])
    cmp_kv = _compress(pf, batch, seq, p["nsa_cmp_pe"][layer], p["nsa_cmp_w1"][layer],
                       p["nsa_cmp_w2"][layer])
    o_cmp, sel = _cmp_attention(pa, cmp_kv, cbias, batch, seq)
    o_slc = _slc_attention(pa, sel, wbias, p["rel_bias"][:, GROUP_HEADS:], batch, seq)
    o_win = _window_attention(pa, batch, seq, PA_CQ, PA_CKW, PA_CVW, wbias, GROUP_HEADS, None,
                              "nsa_win_attn")
    o_d = _stick_breaking(pa, batch, seq)
    mixed = _mix(o_a, o_b, o_cmp, o_slc, o_win, pf, o_d, p["mix_norm_g"][layer])
    return _matmul(mixed, p["w_out"][layer].astype(BF16), bm=1024, bn=512, out_dtype=F32,
                   residual=x2, name="out_proj")


def kernel(x, attn_norm_g, ffn_norm_g, final_norm_g, w_in, w_out, mix_norm_g, rel_bias,
           swa_sinks, conv_dw, conv_dw_b, conv_ln_g, conv_ln_b, conv_pw, nsa_cmp_pe,
           nsa_cmp_w1, nsa_cmp_w2, ffn_w_gate, ffn_w_up, ffn_w_down, moe_router,
           moe_w_gate, moe_w_up, moe_w_down):
    batch, seq, _ = x.shape
    t = batch * seq
    p = dict(attn_norm_g=attn_norm_g, w_in=w_in, w_out=w_out, mix_norm_g=mix_norm_g,
             rel_bias=rel_bias, swa_sinks=swa_sinks, conv_dw=conv_dw, conv_dw_b=conv_dw_b,
             conv_ln_g=conv_ln_g, conv_ln_b=conv_ln_b, conv_pw=conv_pw, nsa_cmp_pe=nsa_cmp_pe,
             nsa_cmp_w1=nsa_cmp_w1, nsa_cmp_w2=nsa_cmp_w2)
    wbias = _win_bias(rel_bias)
    cbias = _cmp_bias(rel_bias, seq, GROUP_HEADS)
    x2 = x.reshape(t, D_MODEL)
    for layer in range(DEPTH):
        x2 = _mixer_layer(x2, batch, seq, layer, p, wbias, cbias)
        i = layer // 2
        if layer % 2 == 0:
            hf = _rmsnorm(x2, ffn_norm_g[layer], out_dtype=BF16, name="ffn_norm")
            hid = _swiglu_up(hf, ffn_w_gate[i].astype(BF16), ffn_w_up[i].astype(BF16),
                             bm=1024, bn=256, name="ffn_up")
            x2 = _matmul_ksplit_res(hid, ffn_w_down[i].astype(BF16), x2,
                                    bm=1024, bn=512, bk=D_FF // 2, name="ffn_down")
        else:
            x2 = _moe_swiglu(x2, ffn_norm_g[layer], moe_router[i], moe_w_gate[i], moe_w_up[i],
                             moe_w_down[i])
    return _rmsnorm(x2, final_norm_g, out_dtype=F32, name="final_norm").reshape(batch, seq, D_MODEL)
```

```python
import functools
import math

import jax
import jax.numpy as jnp
import numpy as np
from jax import lax
from jax.experimental import pallas as pl
from jax.experimental.pallas import tpu as pltpu

D_MODEL = 4096
DEPTH = 2
HEAD_DIM = 128
N_MIXERS = 4
GROUP_WIDTH = D_MODEL // N_MIXERS
GROUP_HEADS = GROUP_WIDTH // HEAD_DIM
KV_HEADS = 2
REP = GROUP_HEADS // KV_HEADS
WINDOW = 128
Q_BLOCK = 128
CONV_WIDTH = 31
CMP_LEN = 32
CMP_STRIDE = 16
SLC_LEN = 64
N_SELECT = 16
N_BRANCH = 3
N_BUCKETS = 32
MAX_DISTANCE = 128
D_FF = 11008
N_EXPERTS = 8
TOP_K = 2
D_EXPERT = D_FF // 2
NORM_EPS = 1e-6
SCALE = HEAD_DIM ** -0.5

VMEM_LIMIT_BYTES = 56 * 1024 * 1024
LANES = 128

F32 = jnp.float32
BF16 = jnp.bfloat16
NEG_INF = float("-inf")
F32_TINY = float(np.finfo(np.float32).tiny)

PA_AQ, PA_AK, PA_AV = 0, 1024, 1280
PA_CQ, PA_CKS, PA_CVS, PA_CKW, PA_CVW = 1536, 2560, 2816, 3072, 3328
PA_DQ, PA_DK, PA_DV = 3584, 4608, 5632
PA_COLS = 6656
PF_VAL, PF_GATE, PF_KC, PF_VC, PF_G = 0, 1024, 2048, 2304, 2560
PF_COLS = 2688


def _params(*sem):
    return pltpu.CompilerParams(dimension_semantics=sem, vmem_limit_bytes=VMEM_LIMIT_BYTES)


def _bucket_thresholds():
    n = np.arange(0, 4 * MAX_DISTANCE)
    max_exact = N_BUCKETS // 2
    nf = np.maximum(n, 1).astype(np.float32)
    large = max_exact + (np.log(nf / max_exact) / math.log(MAX_DISTANCE / max_exact)
                         * (N_BUCKETS - max_exact)).astype(np.int32)
    large = np.minimum(large, N_BUCKETS - 1)
    bucket = np.where(n < max_exact, n, large)
    return [int(np.argmax(bucket >= k)) for k in range(N_BUCKETS)]


BUCKET_THRESHOLDS = _bucket_thresholds()


def _mm_kernel(x_ref, w_ref, o_ref):
    o_ref[...] = jnp.dot(x_ref[...], w_ref[...], preferred_element_type=F32).astype(o_ref.dtype)


def _mm_res_kernel(x_ref, w_ref, r_ref, o_ref):
    acc = jnp.dot(x_ref[...], w_ref[...], preferred_element_type=F32)
    o_ref[...] = (r_ref[...] + acc).astype(o_ref.dtype)


def _mm_res_ksplit_kernel(x_ref, w_ref, r_ref, o_ref):
    @pl.when(pl.program_id(2) == 0)
    def _():
        o_ref[...] = r_ref[...]

    o_ref[...] += jnp.dot(x_ref[...], w_ref[...], preferred_element_type=F32)


def _matmul(x, w, *, bm, bn, out_dtype, residual=None, name="matmul"):
    m, k = x.shape
    _, n = w.shape
    assert m % bm == 0 and n % bn == 0
    in_specs = [pl.BlockSpec((bm, k), lambda i, j: (i, 0)),
                pl.BlockSpec((k, bn), lambda i, j: (0, j))]
    args = [x, w]
    body = _mm_kernel
    if residual is not None:
        in_specs.append(pl.BlockSpec((bm, bn), lambda i, j: (i, j)))
        args.append(residual)
        body = _mm_res_kernel
    return pl.pallas_call(
        body,
        out_shape=jax.ShapeDtypeStruct((m, n), out_dtype),
        grid=(m // bm, n // bn),
        in_specs=in_specs,
        out_specs=pl.BlockSpec((bm, bn), lambda i, j: (i, j)),
        compiler_params=_params("parallel", "arbitrary"),
        name=name,
    )(*args)


def _matmul_ksplit_res(x, w, residual, *, bm, bn, bk, name="matmul_ksplit"):
    m, k = x.shape
    _, n = w.shape
    assert m % bm == 0 and n % bn == 0 and k % bk == 0
    return pl.pallas_call(
        _mm_res_ksplit_kernel,
        out_shape=jax.ShapeDtypeStruct((m, n), F32),
        grid=(m // bm, n // bn, k // bk),
        in_specs=[pl.BlockSpec((bm, bk), lambda i, j, kk: (i, kk)),
                  pl.BlockSpec((bk, bn), lambda i, j, kk: (kk, j)),
                  pl.BlockSpec((bm, bn), lambda i, j, kk: (i, j))],
        out_specs=pl.BlockSpec((bm, bn), lambda i, j, kk: (i, j)),
        compiler_params=_params("parallel", "arbitrary", "arbitrary"),
        name=name,
    )(x, w, residual)


def _swiglu_kernel(x_ref, wg_ref, wu_ref, o_ref):
    x = x_ref[...]
    g = jnp.dot(x, wg_ref[...], preferred_element_type=F32)
    u = jnp.dot(x, wu_ref[...], preferred_element_type=F32)
    o_ref[...] = (g * jax.nn.sigmoid(g) * u).astype(o_ref.dtype)


def _swiglu_up(x, wg, wu, *, bm, bn, name="swiglu_up"):
    m, k = x.shape
    _, n = wg.shape
    assert m % bm == 0 and n % bn == 0
    return pl.pallas_call(
        _swiglu_kernel,
        out_shape=jax.ShapeDtypeStruct((m, n), BF16),
        grid=(m // bm, n // bn),
        in_specs=[pl.BlockSpec((bm, k), lambda i, j: (i, 0)),
                  pl.BlockSpec((k, bn), lambda i, j: (0, j)),
                  pl.BlockSpec((k, bn), lambda i, j: (0, j))],
        out_specs=pl.BlockSpec((bm, bn), lambda i, j: (i, j)),
        compiler_params=_params("parallel", "arbitrary"),
        name=name,
    )(x, wg, wu)


def _rmsnorm_kernel(x_ref, g_ref, o_ref):
    x = x_ref[...]
    y = x * lax.rsqrt(jnp.mean(x * x, axis=-1, keepdims=True) + NORM_EPS)
    o_ref[...] = (y * g_ref[...]).astype(o_ref.dtype)


def _rmsnorm(x, g, *, out_dtype, bm=256, name="rmsnorm"):
    m, d = x.shape
    return pl.pallas_call(
        _rmsnorm_kernel,
        out_shape=jax.ShapeDtypeStruct((m, d), out_dtype),
        grid=(m // bm,),
        in_specs=[pl.BlockSpec((bm, d), lambda i: (i, 0)),
                  pl.BlockSpec((1, d), lambda i: (0, 0))],
        out_specs=pl.BlockSpec((bm, d), lambda i: (i, 0)),
        compiler_params=_params("parallel"),
        name=name,
    )(x, g.reshape(1, d))


def _bias_of_dist(dist, tab_ref, head):
    out = jnp.full(dist.shape, tab_ref[0, head], F32)
    for k in range(1, N_BUCKETS):
        out = jnp.where(dist >= BUCKET_THRESHOLDS[k], tab_ref[k, head], out)
    return out


def _win_bias_kernel(tab_ref, o_ref):
    h = pl.program_id(0)
    qi = lax.broadcasted_iota(jnp.int32, (Q_BLOCK, 2 * Q_BLOCK), 0)
    kj = lax.broadcasted_iota(jnp.int32, (Q_BLOCK, 2 * Q_BLOCK), 1)
    o_ref[0] = _bias_of_dist(qi + Q_BLOCK - kj, tab_ref, h)


def _win_bias(rel_bias):
    nh = rel_bias.shape[1]
    return pl.pallas_call(
        _win_bias_kernel,
        out_shape=jax.ShapeDtypeStruct((nh, Q_BLOCK, 2 * Q_BLOCK), F32),
        grid=(nh,),
        in_specs=[pl.BlockSpec(memory_space=pltpu.SMEM)],
        out_specs=pl.BlockSpec((1, Q_BLOCK, 2 * Q_BLOCK), lambda h: (h, 0, 0)),
        compiler_params=_params("arbitrary"),
        name="win_bias",
    )(rel_bias)


def _cmp_bias_kernel(tab_ref, o_ref, *, head0, rows):
    h = pl.program_id(0) + head0
    n = pl.program_id(1)
    ncmp = o_ref.shape[2]
    t = n * rows + lax.broadcasted_iota(jnp.int32, (rows, ncmp), 0)
    c = lax.broadcasted_iota(jnp.int32, (rows, ncmp), 1)
    o_ref[0] = _bias_of_dist(t - c * CMP_STRIDE - (CMP_LEN - 1), tab_ref, h)


def _cmp_bias(rel_bias, seq, head0, rows=512):
    ncmp = seq // CMP_STRIDE
    rows = min(rows, seq)
    return pl.pallas_call(
        functools.partial(_cmp_bias_kernel, head0=head0, rows=rows),
        out_shape=jax.ShapeDtypeStruct((GROUP_HEADS, seq, ncmp), F32),
        grid=(GROUP_HEADS, seq // rows),
        in_specs=[pl.BlockSpec(memory_space=pltpu.SMEM)],
        out_specs=pl.BlockSpec((1, rows, ncmp), lambda h, n: (h, n, 0)),
        compiler_params=_params("arbitrary", "arbitrary"),
        name="cmp_bias",
    )(rel_bias)


def _stack_heads(q):
    return jnp.concatenate([q[:, r * HEAD_DIM:(r + 1) * HEAD_DIM] for r in range(REP)], axis=0)


def _store_heads(o_ref, o, rows):
    for r in range(REP):
        o_ref[:, r * HEAD_DIM:(r + 1) * HEAD_DIM] = o[r * rows:(r + 1) * rows].astype(o_ref.dtype)


def _qk(q, k):
    return lax.dot_general(q, k, (((1,), (1,)), ((), ())), preferred_element_type=F32)


def _window_kernel(q_ref, kp_ref, kc_ref, vp_ref, vc_ref, bias_ref, sink_ref, o_ref, *, has_sink):
    n = pl.program_id(2)
    q4 = _stack_heads(q_ref[...])
    kcat = jnp.concatenate([kp_ref[...], kc_ref[...]], axis=0)
    vcat = jnp.concatenate([vp_ref[...], vc_ref[...]], axis=0)
    s = _qk(q4, kcat).reshape(REP, Q_BLOCK, 2 * Q_BLOCK) * SCALE + bias_ref[...]
    qi = lax.broadcasted_iota(jnp.int32, (Q_BLOCK, 2 * Q_BLOCK), 0)
    kj = lax.broadcasted_iota(jnp.int32, (Q_BLOCK, 2 * Q_BLOCK), 1)
    dist = qi + Q_BLOCK - kj
    mask = ((dist >= 0) & (dist < WINDOW) & ((n > 0) | (kj >= Q_BLOCK)))[None]
    s = jnp.where(mask, s, NEG_INF)
    m = jnp.max(s, axis=-1, keepdims=True)
    if has_sink:
        sink = sink_ref[...]
        m = jnp.maximum(m, sink)
    p = jnp.where(mask, jnp.exp(s - m), 0.0)
    den = jnp.sum(p, axis=-1, keepdims=True)
    if has_sink:
        den = den + jnp.exp(sink - m)
    p = p / jnp.maximum(den, F32_TINY)
    o = jnp.dot(p.reshape(REP * Q_BLOCK, 2 * Q_BLOCK).astype(BF16), vcat, preferred_element_type=F32)
    _store_heads(o_ref, o, Q_BLOCK)


def _window_attention(pa, batch, seq, q_col, k_col, v_col, bias, bias_head0, sinks, name):
    nb = seq // Q_BLOCK
    qw = REP * HEAD_DIM
    has_sink = sinks is not None
    sink_arr = (sinks if has_sink else jnp.zeros((GROUP_HEADS,), F32)).reshape(GROUP_HEADS, 1, 1)

    def cur(col):
        return lambda b, g, n: (b * nb + n, col // HEAD_DIM + g)

    def prev(col):
        return lambda b, g, n: (b * nb + jnp.maximum(n - 1, 0), col // HEAD_DIM + g)

    kv_block = (Q_BLOCK, HEAD_DIM)
    return pl.pallas_call(
        functools.partial(_window_kernel, has_sink=has_sink),
        out_shape=jax.ShapeDtypeStruct((batch * seq, GROUP_WIDTH), F32),
        grid=(batch, KV_HEADS, nb),
        in_specs=[pl.BlockSpec((Q_BLOCK, qw), lambda b, g, n: (b * nb + n, q_col // qw + g)),
                  pl.BlockSpec(kv_block, prev(k_col)), pl.BlockSpec(kv_block, cur(k_col)),
                  pl.BlockSpec(kv_block, prev(v_col)), pl.BlockSpec(kv_block, cur(v_col)),
                  pl.BlockSpec((REP, Q_BLOCK, 2 * Q_BLOCK),
                               lambda b, g, n: (bias_head0 // REP + g, 0, 0)),
                  pl.BlockSpec((REP, 1, 1), lambda b, g, n: (g, 0, 0))],
        out_specs=pl.BlockSpec((Q_BLOCK, qw), lambda b, g, n: (b * nb + n, g)),
        compiler_params=_params("parallel", "parallel", "arbitrary"),
        name=name,
    )(pa, pa, pa, pa, pa, bias, sink_arr)


def _compress_kernel(t_ref, pe_ref, w1_ref, w2_ref, o_ref):
    ncmp = o_ref.shape[3]
    half = CMP_LEN // 2
    pe = pe_ref[0]
    rows = [t_ref[pl.ds(r, ncmp, stride=CMP_STRIDE), :] for r in range(CMP_STRIDE)]
    xa = jnp.concatenate([rows[r] + pe[r:r + 1] for r in range(half)], axis=1).astype(BF16)
    xb = jnp.concatenate([rows[r] + pe[half + r:half + r + 1] for r in range(half)], axis=1).astype(BF16)
    kw = half * HEAD_DIM
    p0 = jnp.dot(xa, w1_ref[0, :kw, :], preferred_element_type=F32)
    p1 = jnp.dot(xb, w1_ref[0, kw:, :], preferred_element_type=F32)
    pre = p0 + pltpu.roll(p1, ncmp - 1, 0)
    hid = pre * jax.nn.sigmoid(pre)
    o_ref[0, 0, 0] = jnp.dot(hid.astype(BF16), w2_ref[0], preferred_element_type=F32).astype(o_ref.dtype)


def _compress(pf, batch, seq, pe, w1, w2):
    ncmp = seq // CMP_STRIDE
    return pl.pallas_call(
        _compress_kernel,
        out_shape=jax.ShapeDtypeStruct((2, batch, KV_HEADS, ncmp, HEAD_DIM), BF16),
        grid=(2, batch, KV_HEADS),
        in_specs=[pl.BlockSpec((seq, HEAD_DIM), lambda kv, b, g: (b, PF_KC // HEAD_DIM + KV_HEADS * kv + g)),
                  pl.BlockSpec((1, CMP_LEN, HEAD_DIM), lambda kv, b, g: (kv, 0, 0)),
                  pl.BlockSpec((1, CMP_LEN * HEAD_DIM, HEAD_DIM), lambda kv, b, g: (kv, 0, 0)),
                  pl.BlockSpec((1, HEAD_DIM, HEAD_DIM), lambda kv, b, g: (kv, 0, 0))],
        out_specs=pl.BlockSpec((1, 1, 1, ncmp, HEAD_DIM), lambda kv, b, g: (kv, b, g, 0, 0)),
        compiler_params=_params("arbitrary", "arbitrary", "arbitrary"),
        name="nsa_compress",
    )(pf, pe, w1.astype(BF16), w2.astype(BF16))


def _cmp_attn_kernel(q_ref, kc_ref, vc_ref, bias_ref, ov_ref, o_ref, sel_ref, *, n_sel):
    n = pl.program_id(2)
    ncmp = kc_ref.shape[3]
    nslc = ov_ref.shape[1]
    q4 = _stack_heads(q_ref[...])
    s = _qk(q4, kc_ref[0, 0, 0]).reshape(REP, Q_BLOCK, ncmp) * SCALE + bias_ref[...]
    t = n * Q_BLOCK + lax.broadcasted_iota(jnp.int32, (Q_BLOCK, ncmp), 0)
    c = lax.broadcasted_iota(jnp.int32, (Q_BLOCK, ncmp), 1)
    vis = (t - c * CMP_STRIDE - (CMP_LEN - 1) >= 0)[None]
    s = jnp.where(vis, s, NEG_INF)
    m = jnp.max(s, axis=-1, keepdims=True)
    m = jnp.where(m == NEG_INF, 0.0, m)
    p = jnp.where(vis, jnp.exp(s - m), 0.0)
    den = jnp.sum(p, axis=-1, keepdims=True)
    p = p / jnp.maximum(den, F32_TINY)
    o = jnp.dot(p.reshape(REP * Q_BLOCK, ncmp).astype(BF16), vc_ref[0, 0, 0], preferred_element_type=F32)
    _store_heads(o_ref, o, Q_BLOCK)

    psum = p[0]
    for r in range(1, REP):
        psum = psum + p[r]
    hi = psum.astype(BF16)
    lo = (psum - hi.astype(F32)).astype(BF16)
    ov = ov_ref[...]
    imp = jnp.dot(hi, ov, preferred_element_type=F32) + jnp.dot(lo, ov, preferred_element_type=F32)
    tq = n * Q_BLOCK + lax.broadcasted_iota(jnp.int32, (Q_BLOCK, nslc), 0)
    blk = lax.broadcasted_iota(jnp.int32, (Q_BLOCK, nslc), 1)
    cur = jnp.right_shift(tq, SLC_LEN.bit_length() - 1)
    forced = (blk == 0) | (blk == cur) | (blk == cur - 1)
    score = jnp.where(forced, jnp.inf, jnp.where(blk <= cur, imp, NEG_INF))
    rank = jnp.zeros((Q_BLOCK, nslc), F32)
    for k in range(nslc):
        col = score[:, k:k + 1]
        ahead = (col > score) | ((col == score) & (blk > k))
        rank = rank + jnp.where(ahead, 1.0, 0.0)
    sel_ref[0, 0] = jnp.where(rank < n_sel, 1.0, 0.0)


def _overlap_matrix(ncmp, nslc):
    cs = np.arange(ncmp)[:, None] * CMP_STRIDE
    ss = np.arange(nslc)[None, :] * SLC_LEN
    return ((cs < ss + SLC_LEN) & (cs + CMP_LEN > ss)).astype(np.float32)


def _cmp_attention(pa, cmp_kv, cbias, batch, seq):
    nb = seq // Q_BLOCK
    ncmp = seq // CMP_STRIDE
    nslc = seq // SLC_LEN
    qw = REP * HEAD_DIM
    ov = jnp.asarray(_overlap_matrix(ncmp, nslc), BF16)
    return pl.pallas_call(
        functools.partial(_cmp_attn_kernel, n_sel=min(N_SELECT, nslc)),
        out_shape=(jax.ShapeDtypeStruct((batch * seq, GROUP_WIDTH), F32),
                   jax.ShapeDtypeStruct((batch, KV_HEADS, seq, nslc), F32)),
        grid=(batch, KV_HEADS, nb),
        in_specs=[pl.BlockSpec((Q_BLOCK, qw), lambda b, g, n: (b * nb + n, PA_CQ // qw + g)),
                  pl.BlockSpec((1, 1, 1, ncmp, HEAD_DIM), lambda b, g, n: (0, b, g, 0, 0)),
                  pl.BlockSpec((1, 1, 1, ncmp, HEAD_DIM), lambda b, g, n: (1, b, g, 0, 0)),
                  pl.BlockSpec((REP, Q_BLOCK, ncmp), lambda b, g, n: (g, n, 0)),
                  pl.BlockSpec((ncmp, nslc), lambda b, g, n: (0, 0))],
        out_specs=(pl.BlockSpec((Q_BLOCK, qw), lambda b, g, n: (b * nb + n, g)),
                   pl.BlockSpec((1, 1, Q_BLOCK, nslc), lambda b, g, n: (b, g, n, 0))),
        compiler_params=_params("parallel", "parallel", "arbitrary"),
        name="nsa_cmp_attn",
    )(pa, cmp_kv, cmp_kv, cbias, ov)


FAR_TILE = 4 * Q_BLOCK


def _slc_attn_kernel(q_ref, k_ref, v_ref, sel_ref, bias_ref, far_bias_ref, efar_ref, enear_ref, o_ref):
    n = pl.program_id(2)
    q4 = _stack_heads(q_ref[...])
    selb = sel_ref[0, 0].astype(BF16)
    far_bias = far_bias_ref[...]
    rows = REP * Q_BLOCK

    def online(carry, s, mask, v):
        m, l, acc = carry
        s = jnp.where(mask[None], s, NEG_INF)
        m_new = jnp.maximum(m, jnp.max(s, axis=-1, keepdims=True))
        m_safe = jnp.where(m_new == NEG_INF, 0.0, m_new)
        alpha = jnp.exp(m - m_safe)
        p = jnp.exp(s - m_safe)
        l = alpha * l + jnp.sum(p, axis=-1, keepdims=True)
        pv = jnp.dot(p.reshape(rows, p.shape[-1]).astype(BF16), v, preferred_element_type=F32)
        acc = alpha * acc + pv.reshape(REP, Q_BLOCK, HEAD_DIM)
        return m_new, l, acc

    far_end = (n - 1) * Q_BLOCK

    def far_step(j, carry):
        k0 = pl.multiple_of(j * FAR_TILE, FAR_TILE)
        k = k_ref[pl.ds(k0, FAR_TILE), :]
        v = v_ref[pl.ds(k0, FAR_TILE), :]
        s = _qk(q4, k).reshape(REP, Q_BLOCK, FAR_TILE) * SCALE + far_bias
        picked = jnp.dot(selb, efar_ref[j], preferred_element_type=F32) > 0.5
        kidx = k0 + lax.broadcasted_iota(jnp.int32, (Q_BLOCK, FAR_TILE), 1)
        return online(carry, s, picked & (kidx < far_end), v)

    init = (jnp.full((REP, Q_BLOCK, 1), NEG_INF, F32), jnp.zeros((REP, Q_BLOCK, 1), F32),
            jnp.zeros((REP, Q_BLOCK, HEAD_DIM), F32))
    n_far = (jnp.maximum(far_end, 0) + FAR_TILE - 1) // FAR_TILE
    carry = lax.fori_loop(0, n_far, far_step, init)

    pb = jnp.maximum(n - 1, 0)
    p0 = pl.multiple_of(pb * Q_BLOCK, Q_BLOCK)
    c0 = pl.multiple_of(n * Q_BLOCK, Q_BLOCK)
    kcat = jnp.concatenate([k_ref[pl.ds(p0, Q_BLOCK), :], k_ref[pl.ds(c0, Q_BLOCK), :]], axis=0)
    vcat = jnp.concatenate([v_ref[pl.ds(p0, Q_BLOCK), :], v_ref[pl.ds(c0, Q_BLOCK), :]], axis=0)
    s = _qk(q4, kcat).reshape(REP, Q_BLOCK, 2 * Q_BLOCK) * SCALE + bias_ref[...]
    picked = jnp.concatenate(
        [jnp.dot(selb, enear_ref[pb], preferred_element_type=F32),
         jnp.dot(selb, enear_ref[n], preferred_element_type=F32)], axis=1) > 0.5
    qi = lax.broadcasted_iota(jnp.int32, (Q_BLOCK, 2 * Q_BLOCK), 0)
    kj = lax.broadcasted_iota(jnp.int32, (Q_BLOCK, 2 * Q_BLOCK), 1)
    causal = (qi + Q_BLOCK - kj >= 0) & ((n > 0) | (kj >= Q_BLOCK))
    m, l, acc = online(carry, s, picked & causal, vcat)
    o = acc / jnp.maximum(l, F32_TINY)
    for r in range(REP):
        o_ref[:, r * HEAD_DIM:(r + 1) * HEAD_DIM] = o[r].astype(o_ref.dtype)


def _expand_matrix(seq, tile):
    nslc = seq // SLC_LEN
    key = np.arange(seq).reshape(seq // tile, 1, tile)
    return (key // SLC_LEN == np.arange(nslc)[None, :, None]).astype(np.float32)


def _slc_attention(pa, sel, wbias, rel_bias_c, batch, seq):
    nb = seq // Q_BLOCK
    nslc = seq // SLC_LEN
    qw = REP * HEAD_DIM
    nfar = max(seq // FAR_TILE, 1)
    efar = jnp.asarray(_expand_matrix(max(seq, FAR_TILE), FAR_TILE)[:, :nslc], BF16)
    enear = jnp.asarray(_expand_matrix(seq, Q_BLOCK), BF16)
    far_bias = rel_bias_c[N_BUCKETS - 1].reshape(GROUP_HEADS, 1, 1)
    return pl.pallas_call(
        _slc_attn_kernel,
        out_shape=jax.ShapeDtypeStruct((batch * seq, GROUP_WIDTH), F32),
        grid=(batch, KV_HEADS, nb),
        in_specs=[pl.BlockSpec((Q_BLOCK, qw), lambda b, g, n: (b * nb + n, PA_CQ // qw + g)),
                  pl.BlockSpec((seq, HEAD_DIM), lambda b, g, n: (b, PA_CKS // HEAD_DIM + g)),
                  pl.BlockSpec((seq, HEAD_DIM), lambda b, g, n: (b, PA_CVS // HEAD_DIM + g)),
                  pl.BlockSpec((1, 1, Q_BLOCK, nslc), lambda b, g, n: (b, g, n, 0)),
                  pl.BlockSpec((REP, Q_BLOCK, 2 * Q_BLOCK),
                               lambda b, g, n: (GROUP_HEADS // REP + g, 0, 0)),
                  pl.BlockSpec((REP, 1, 1), lambda b, g, n: (g, 0, 0)),
                  pl.BlockSpec((nfar, nslc, FAR_TILE), lambda b, g, n: (0, 0, 0)),
                  pl.BlockSpec((nb, nslc, Q_BLOCK), lambda b, g, n: (0, 0, 0))],
        out_specs=pl.BlockSpec((Q_BLOCK, qw), lambda b, g, n: (b * nb + n, g)),
        compiler_params=_params("parallel", "parallel", "arbitrary"),
        name="nsa_slc_attn",
    )(pa, pa, pa, sel, wbias, far_bias, efar, enear)


SB_TK = 256
SB_BLOCKS = 4
SB_STEP = SB_BLOCKS * SB_TK
LOG2E = math.log2(math.e)


def _sb_kernel(q_ref, k_ref, v_ref, u_ref, o_ref):
    n = pl.program_id(2)
    u2 = u_ref[...]
    q = q_ref[...]

    def step(j, carry, masked):
        later, acc = carry
        k0 = pl.multiple_of(j * SB_STEP, SB_STEP)
        k = k_ref[pl.ds(k0, SB_STEP), :]
        v = v_ref[pl.ds(k0, SB_STEP), :]
        z = _qk(q, k) * (SCALE * LOG2E)
        zneg = jnp.minimum(z, 0.0)
        zpos_neg = zneg - z
        t = jnp.log2(1.0 + jnp.exp2(zneg + zpos_neg))
        log_keep = zpos_neg - t
        if masked:
            qi = lax.broadcasted_iota(jnp.int32, (SB_STEP, SB_STEP), 0)
            kj = lax.broadcasted_iota(jnp.int32, (SB_STEP, SB_STEP), 1)
            before = kj < qi
            log_keep = jnp.where(before, log_keep, 0.0)
        hi = log_keep.astype(BF16)
        lo = (log_keep - hi.astype(F32)).astype(BF16)
        blocks = [jnp.concatenate([hi[:, b * SB_TK:(b + 1) * SB_TK], lo[:, b * SB_TK:(b + 1) * SB_TK]],
                                  axis=1) for b in range(SB_BLOCKS)]
        suffix = jnp.dot(jnp.concatenate(blocks, axis=0), u2, preferred_element_type=F32)
        parts = [None] * SB_BLOCKS
        for b in range(SB_BLOCKS - 1, -1, -1):
            sfx = suffix[b * SB_STEP:(b + 1) * SB_STEP]
            parts[b] = jnp.exp2(z[:, b * SB_TK:(b + 1) * SB_TK] + sfx + later)
            later = later + sfx[:, 0:1]
        a = jnp.concatenate(parts, axis=1)
        if masked:
            a = jnp.where(before, a, 0.0)
        acc = acc + jnp.dot(a.astype(BF16), v, preferred_element_type=F32)
        return later, acc

    carry = (jnp.zeros((SB_STEP, 1), F32), jnp.zeros((SB_STEP, HEAD_DIM), F32))
    carry = step(n, carry, True)
    carry = lax.fori_loop(0, n, lambda i, c: step(n - 1 - i, c, False), carry)
    o_ref[...] = carry[1].astype(o_ref.dtype)


def _stick_breaking(pa, batch, seq):
    assert seq % SB_STEP == 0
    nb = seq // SB_STEP
    tri = np.tril(np.ones((SB_TK, SB_TK), np.float32))
    u2 = jnp.asarray(np.concatenate([tri, tri], axis=0), BF16)
    return pl.pallas_call(
        _sb_kernel,
        out_shape=jax.ShapeDtypeStruct((batch * seq, GROUP_WIDTH), F32),
        grid=(batch, GROUP_HEADS, nb),
        in_specs=[pl.BlockSpec((SB_STEP, HEAD_DIM), lambda b, h, n: (b * nb + n, PA_DQ // HEAD_DIM + h)),
                  pl.BlockSpec((seq, HEAD_DIM), lambda b, h, n: (b, PA_DK // HEAD_DIM + h)),
                  pl.BlockSpec((seq, HEAD_DIM), lambda b, h, n: (b, PA_DV // HEAD_DIM + h)),
                  pl.BlockSpec((2 * SB_TK, SB_TK), lambda b, h, n: (0, 0))],
        out_specs=pl.BlockSpec((SB_STEP, HEAD_DIM), lambda b, h, n: (b * nb + n, h)),
        compiler_params=_params("parallel", "parallel", "arbitrary"),
        name="stick_breaking",
    )(pa, pa, pa, u2)


CONV_TILE = 256
CONV_HALO = 32


def _conv_kernel(val_ref, gate_ref, hval_ref, hgate_ref, dw_ref, dwb_ref, lng_ref, lnb_ref, pw_ref,
                 o_ref, ext_ref):
    n = pl.program_id(1)
    halo = hval_ref[...] * jax.nn.sigmoid(hgate_ref[...])
    ext_ref[0:CONV_HALO, :] = jnp.where(n > 0, halo, 0.0)
    ext_ref[CONV_HALO:, :] = val_ref[...] * jax.nn.sigmoid(gate_ref[...])
    first = CONV_HALO - (CONV_WIDTH - 1)
    acc = jnp.zeros((CONV_TILE, GROUP_WIDTH), F32) + dwb_ref[...]
    for w in range(CONV_WIDTH):
        acc = acc + ext_ref[first + w:first + w + CONV_TILE, :] * dw_ref[w:w + 1, :]
    mu = jnp.mean(acc, axis=-1, keepdims=True)
    cen = acc - mu
    var = jnp.mean(cen * cen, axis=-1, keepdims=True)
    y = cen * lax.rsqrt(var + NORM_EPS) * lng_ref[...] + lnb_ref[...]
    y = y * jax.nn.sigmoid(y)
    o_ref[...] = jnp.dot(y.astype(BF16), pw_ref[...], preferred_element_type=F32).astype(o_ref.dtype)


def _conformer_conv(pf, batch, seq, dw, dw_b, ln_g, ln_b, pw):
    nt = seq // CONV_TILE
    hpt = CONV_TILE // CONV_HALO
    c = GROUP_WIDTH
    vec = pl.BlockSpec((1, c), lambda b, n: (0, 0))

    def halo(col):
        return lambda b, n: (jnp.maximum((b * nt + n) * hpt - 1, 0), col // c)

    return pl.pallas_call(
        _conv_kernel,
        out_shape=jax.ShapeDtypeStruct((batch * seq, c), F32),
        grid=(batch, nt),
        in_specs=[pl.BlockSpec((CONV_TILE, c), lambda b, n: (b * nt + n, PF_VAL // c)),
                  pl.BlockSpec((CONV_TILE, c), lambda b, n: (b * nt + n, PF_GATE // c)),
                  pl.BlockSpec((CONV_HALO, c), halo(PF_VAL)),
                  pl.BlockSpec((CONV_HALO, c), halo(PF_GATE)),
                  pl.BlockSpec((CONV_WIDTH, c), lambda b, n: (0, 0)),
                  vec, vec, vec,
                  pl.BlockSpec((c, c), lambda b, n: (0, 0))],
        out_specs=pl.BlockSpec((CONV_TILE, c), lambda b, n: (b * nt + n, 0)),
        scratch_shapes=[pltpu.VMEM((CONV_HALO + CONV_TILE, c), F32)],
        compiler_params=_params("parallel", "arbitrary"),
        name="conformer_conv",
    )(pf, pf, pf, pf, dw, dw_b.reshape(1, c), ln_g.reshape(1, c), ln_b.reshape(1, c), pw.astype(BF16))


MIX_TILE = 256


def _mix_kernel(oa_ref, ob_ref, ocmp_ref, oslc_ref, owin_ref, gl_ref, od_ref, g_ref, o_ref):
    gates = jax.nn.sigmoid(gl_ref[...])

    def norm_store(x, grp):
        y = x * lax.rsqrt(jnp.mean(x * x, axis=-1, keepdims=True) + NORM_EPS)
        sl = slice(grp * GROUP_WIDTH, (grp + 1) * GROUP_WIDTH)
        o_ref[:, sl] = (y * g_ref[:, sl]).astype(o_ref.dtype)

    norm_store(oa_ref[...], 0)
    norm_store(ob_ref[...], 1)
    heads = []
    for h in range(GROUP_HEADS):
        sl = slice(h * HEAD_DIM, (h + 1) * HEAD_DIM)
        c0 = h * N_BRANCH
        heads.append(gates[:, c0:c0 + 1] * ocmp_ref[:, sl] + gates[:, c0 + 1:c0 + 2] * oslc_ref[:, sl]
                     + gates[:, c0 + 2:c0 + 3] * owin_ref[:, sl])
    norm_store(jnp.concatenate(heads, axis=1), 2)
    norm_store(od_ref[...], 3)


def _mix(o_a, o_b, o_cmp, o_slc, o_win, pf, o_d, g):
    t = o_a.shape[0]
    grp = pl.BlockSpec((MIX_TILE, GROUP_WIDTH), lambda i: (i, 0))
    return pl.pallas_call(
        _mix_kernel,
        out_shape=jax.ShapeDtypeStruct((t, D_MODEL), BF16),
        grid=(t // MIX_TILE,),
        in_specs=[grp, grp, grp, grp, grp,
                  pl.BlockSpec((MIX_TILE, LANES), lambda i: (i, PF_G // LANES)),
                  grp,
                  pl.BlockSpec((1, D_MODEL), lambda i: (0, 0))],
        out_specs=pl.BlockSpec((MIX_TILE, D_MODEL), lambda i: (i, 0)),
        compiler_params=_params("parallel"),
        name="mix_norm",
    )(o_a, o_b, o_cmp, o_slc, o_win, pf, o_d, g.reshape(1, D_MODEL))


MOE_TILE = 512
MOE_UP_BN = 512
MOE_DOWN_BN = 512
ROUTE_TILE = 256
GATHER_TILE = 256
COMBINE_TILE = 128


def _norm_route_kernel(x_ref, g_ref, r_ref, h_ref, route_ref):
    x = x_ref[...]
    y = x * lax.rsqrt(jnp.mean(x * x, axis=-1, keepdims=True) + NORM_EPS) * g_ref[...]
    h_ref[...] = y
    logits = jnp.dot(y.astype(BF16), r_ref[...], preferred_element_type=F32)
    lane = lax.broadcasted_iota(jnp.int32, logits.shape, 1)
    logits = jnp.where(lane < N_EXPERTS, logits, NEG_INF)
    e = jnp.exp(logits - jnp.max(logits, axis=-1, keepdims=True))
    probs = e / jnp.sum(e, axis=-1, keepdims=True)
    p1 = jnp.max(probs, axis=-1, keepdims=True)
    i1 = jnp.min(jnp.where(probs == p1, lane, LANES), axis=-1, keepdims=True)
    rest = jnp.where(lane == i1, -1.0, probs)
    p2 = jnp.max(rest, axis=-1, keepdims=True)
    i2 = jnp.min(jnp.where(rest == p2, lane, LANES), axis=-1, keepdims=True)
    tot = p1 + p2
    route_ref[...] = jnp.where(lane == 0, p1 / tot,
                               jnp.where(lane == 1, p2 / tot,
                                         jnp.where(lane == 2, i1.astype(F32),
                                                   jnp.where(lane == 3, i2.astype(F32), 0.0))))


def _norm_route(x, g, router):
    m, d = x.shape
    rpad = jnp.zeros((d, LANES), BF16).at[:, :N_EXPERTS].set(router.astype(BF16))
    return pl.pallas_call(
        _norm_route_kernel,
        out_shape=(jax.ShapeDtypeStruct((m, d), F32), jax.ShapeDtypeStruct((m, LANES), F32)),
        grid=(m // ROUTE_TILE,),
        in_specs=[pl.BlockSpec((ROUTE_TILE, d), lambda i: (i, 0)),
                  pl.BlockSpec((1, d), lambda i: (0, 0)),
                  pl.BlockSpec((d, LANES), lambda i: (0, 0))],
        out_specs=(pl.BlockSpec((ROUTE_TILE, d), lambda i: (i, 0)),
                   pl.BlockSpec((ROUTE_TILE, LANES), lambda i: (i, 0))),
        compiler_params=_params("parallel"),
        name="ffn_norm_route",
    )(x, g.reshape(1, d), rpad)


def _row_copy(src_hbm, dst_ref, sem, src_row, dst_row):
    return pltpu.make_async_copy(src_hbm.at[pl.ds(src_row, 1), :], dst_ref.at[pl.ds(dst_row, 1), :], sem)


def _gather_kernel(tok_ref, nu_ref, h_hbm, o_ref, buf_ref, sem):
    i = pl.program_id(0)
    n_live = nu_ref[0] * (MOE_TILE // GATHER_TILE)

    def issue(tile, slot):
        def start(r, c):
            _row_copy(h_hbm, buf_ref.at[slot], sem.at[slot], tok_ref[tile * GATHER_TILE + r], r).start()
            return c
        lax.fori_loop(0, GATHER_TILE, start, 0)

    @pl.when(i == 0)
    def _():
        issue(0, 0)

    @pl.when(i + 1 < n_live)
    def _():
        issue(i + 1, (i + 1) % 2)

    @pl.when(i < n_live)
    def _():
        slot = i % 2

        def wait(r, c):
            _row_copy(h_hbm, buf_ref.at[slot], sem.at[slot], 0, r).wait()
            return c

        lax.fori_loop(0, GATHER_TILE, wait, 0)
        o_ref[...] = buf_ref[slot].astype(o_ref.dtype)

    @pl.when(i >= n_live)
    def _():
        o_ref[...] = jnp.zeros_like(o_ref)


def _gather_rows(h, row_token, n_used):
    rows = row_token.shape[0]
    d = h.shape[1]
    return pl.pallas_call(
        _gather_kernel,
        out_shape=jax.ShapeDtypeStruct((rows, d), BF16),
        grid_spec=pltpu.PrefetchScalarGridSpec(
            num_scalar_prefetch=2,
            grid=(rows // GATHER_TILE,),
            in_specs=[pl.BlockSpec(memory_space=pl.ANY)],
            out_specs=pl.BlockSpec((GATHER_TILE, d), lambda i, tok, nu: (i, 0)),
            scratch_shapes=[pltpu.VMEM((2, GATHER_TILE, d), F32), pltpu.SemaphoreType.DMA((2,))]),
        compiler_params=_params("arbitrary"),
        name="moe_gather",
    )(row_token, n_used, h)


CAST_ROWS = 128


def _swiglu_tile(x, s_ref):
    g = jnp.dot(x, s_ref[0], preferred_element_type=F32)
    u = jnp.dot(x, s_ref[1], preferred_element_type=F32)
    return g * jax.nn.sigmoid(g) * u


def _plain_tile(x, s_ref):
    return jnp.dot(x, s_ref[0], preferred_element_type=F32)


def _grouped_kernel(te_ref, nu_ref, first_ref, run_ref, nruns_ref, rune_ref, x_ref, *rest,
                    n_weights, bn, n_cols, tile_fn):
    w_hbm = rest[:n_weights]
    o_ref, wbuf_ref, s_ref, sem = rest[n_weights:]
    j = pl.program_id(0)
    i = pl.program_id(1)
    nj = pl.num_programs(0)
    k = s_ref.shape[1]
    last_width = n_cols - (pl.cdiv(n_cols, bn) - 1) * bn
    live = i < nu_ref[0]

    def tile_copy(w, expert, jj, slot, width):
        col0 = pl.multiple_of(jj * bn, bn)
        return pltpu.make_async_copy(w_hbm[w].at[expert, :, pl.ds(col0, width)],
                                     wbuf_ref.at[slot, w, :, pl.ds(0, width)], sem.at[slot, w])

    def by_width(jj, fn):
        if last_width == bn:
            fn(bn)
        else:
            pl.when(jj < nj - 1)(lambda: fn(bn))
            pl.when(jj == nj - 1)(lambda: fn(last_width))

    def issue(expert, jj, slot):
        def go(width):
            for w in range(n_weights):
                tile_copy(w, expert, jj, slot, width).start()
        by_width(jj, go)

    def wait_and_cast(expert, jj, slot):
        def go(width):
            for w in range(n_weights):
                tile_copy(w, expert, jj, slot, width).wait()

                def rows(c, carry, w=w):
                    r0 = pl.multiple_of(c * CAST_ROWS, CAST_ROWS)
                    s_ref[w, pl.ds(r0, CAST_ROWS), 0:width] = (
                        wbuf_ref[slot, w, pl.ds(r0, CAST_ROWS), 0:width].astype(BF16))
                    return carry

                lax.fori_loop(0, k // CAST_ROWS, rows, 0)
        by_width(jj, go)

    @pl.when(live & (first_ref[i] == 1))
    def _():
        run = run_ref[i]
        group = j * nruns_ref[0] + run
        slot = group % 2
        expert = te_ref[i]

        @pl.when(group == 0)
        def _():
            issue(expert, j, 0)

        wait_and_cast(expert, j, slot)
        wrap = run + 1 == nruns_ref[0]
        next_run = jnp.where(wrap, 0, run + 1)
        next_j = jnp.where(wrap, j + 1, j)

        @pl.when(next_j < nj)
        def _():
            issue(rune_ref[next_run], next_j, 1 - slot)

    @pl.when(live)
    def _():
        o_ref[...] = tile_fn(x_ref[...], s_ref).astype(o_ref.dtype)

    @pl.when(jnp.logical_not(live))
    def _():
        o_ref[...] = jnp.zeros_like(o_ref)


def _grouped_call(tile_fn, tables, xs, weights, bn, out_dtype, name):
    rows, k = xs.shape
    n = weights[0].shape[2]
    nt = rows // MOE_TILE
    nw = len(weights)
    assert k % CAST_ROWS == 0 and bn % LANES == 0 and n % LANES == 0

    def used(i, nu):
        return jnp.minimum(i, nu[0] - 1)

    body = functools.partial(_grouped_kernel, n_weights=nw, bn=bn, n_cols=n, tile_fn=tile_fn)
    return pl.pallas_call(
        body,
        out_shape=jax.ShapeDtypeStruct((rows, n), out_dtype),
        grid_spec=pltpu.PrefetchScalarGridSpec(
            num_scalar_prefetch=len(tables),
            grid=(pl.cdiv(n, bn), nt),
            in_specs=[pl.BlockSpec((MOE_TILE, k), lambda j, i, te, nu, *_: (used(i, nu), 0))]
            + [pl.BlockSpec(memory_space=pl.ANY) for _ in weights],
            out_specs=pl.BlockSpec((MOE_TILE, bn), lambda j, i, *_: (i, j)),
            scratch_shapes=[pltpu.VMEM((2, nw, k, bn), F32), pltpu.VMEM((nw, k, bn), BF16),
                            pltpu.SemaphoreType.DMA((2, nw))]),
        compiler_params=_params("arbitrary", "arbitrary"),
        name=name,
    )(*tables, xs, *weights)


def _combine_kernel(dest_ref, x_ref, route_ref, y_hbm, *rest, normalize):
    if normalize:
        g_ref, o_ref, y0_ref, y1_ref, sem = rest
    else:
        o_ref, y0_ref, y1_ref, sem = rest
    i = pl.program_id(0)

    def issue(tile, slot):
        def start(r, c):
            tok = tile * COMBINE_TILE + r
            _row_copy(y_hbm, y0_ref.at[slot], sem.at[slot], dest_ref[2 * tok], r).start()
            _row_copy(y_hbm, y1_ref.at[slot], sem.at[slot], dest_ref[2 * tok + 1], r).start()
            return c
        lax.fori_loop(0, COMBINE_TILE, start, 0)

    @pl.when(i == 0)
    def _():
        issue(0, 0)

    @pl.when(i + 1 < pl.num_programs(0))
    def _():
        issue(i + 1, (i + 1) % 2)

    slot = i % 2

    def wait(r, c):
        _row_copy(y_hbm, y0_ref.at[slot], sem.at[slot], 0, r).wait()
        _row_copy(y_hbm, y1_ref.at[slot], sem.at[slot], 0, r).wait()
        return c

    lax.fori_loop(0, COMBINE_TILE, wait, 0)
    route = route_ref[...]
    out = x_ref[...] + route[:, 0:1] * y0_ref[slot] + route[:, 1:2] * y1_ref[slot]
    if normalize:
        out = out * lax.rsqrt(jnp.mean(out * out, axis=-1, keepdims=True) + NORM_EPS) * g_ref[...]
    o_ref[...] = out


def _moe_combine(x2, route, y, dest, out_norm_g=None):
    t, d = x2.shape
    normalize = out_norm_g is not None
    in_specs = [pl.BlockSpec((COMBINE_TILE, d), lambda i, dest: (i, 0)),
                pl.BlockSpec((COMBINE_TILE, LANES), lambda i, dest: (i, 0)),
                pl.BlockSpec(memory_space=pl.ANY)]
    args = [dest, x2, route, y]
    if normalize:
        in_specs.append(pl.BlockSpec((1, d), lambda i, dest: (0, 0)))
        args.append(out_norm_g.reshape(1, d))
    return pl.pallas_call(
        functools.partial(_combine_kernel, normalize=normalize),
        out_shape=jax.ShapeDtypeStruct((t, d), F32),
        grid_spec=pltpu.PrefetchScalarGridSpec(
            num_scalar_prefetch=1,
            grid=(t // COMBINE_TILE,),
            in_specs=in_specs,
            out_specs=pl.BlockSpec((COMBINE_TILE, d), lambda i, dest: (i, 0)),
            scratch_shapes=[pltpu.VMEM((2, COMBINE_TILE, d), F32), pltpu.VMEM((2, COMBINE_TILE, d), F32),
                            pltpu.SemaphoreType.DMA((2,))]),
        compiler_params=_params("arbitrary"),
        name="moe_combine",
    )(*args)


def _routing_tables(top_i):
    t = top_i.shape[0]
    e_flat = top_i.reshape(-1)
    onehot = (e_flat[:, None] == jnp.arange(N_EXPERTS, dtype=jnp.int32)[None, :]).astype(jnp.int32)
    csum = jnp.cumsum(onehot, axis=0)
    counts = csum[-1]
    pos = jnp.take_along_axis(csum, e_flat[:, None], axis=1)[:, 0] - 1
    padded = ((counts + MOE_TILE - 1) // MOE_TILE) * MOE_TILE
    ends = jnp.cumsum(padded)
    dest = ((ends - padded)[e_flat] + pos).astype(jnp.int32)
    rows = TOP_K * t + N_EXPERTS * MOE_TILE
    row_token = jnp.zeros((rows,), jnp.int32).at[dest].set(jnp.arange(TOP_K * t, dtype=jnp.int32) // TOP_K)
    tile_start = jnp.arange(rows // MOE_TILE, dtype=jnp.int32) * MOE_TILE
    tile_expert = jnp.minimum(jnp.sum(tile_start[:, None] >= ends[None, :], axis=1), N_EXPERTS - 1)
    n_used = (ends[-1:] // MOE_TILE).astype(jnp.int32)
    tile_expert = tile_expert.astype(jnp.int32)
    tiles = jnp.arange(rows // MOE_TILE, dtype=jnp.int32)
    prev = jnp.concatenate([jnp.full((1,), -1, jnp.int32), tile_expert[:-1]])
    first = ((tiles < n_used[0]) & (tile_expert != prev)).astype(jnp.int32)
    run = jnp.maximum(jnp.cumsum(first) - 1, 0).astype(jnp.int32)
    n_runs = jnp.sum(first, keepdims=True).astype(jnp.int32)
    run_expert = jnp.zeros((N_EXPERTS,), jnp.int32).at[
        jnp.where(first == 1, run, N_EXPERTS)].set(tile_expert, mode="drop")
    return dest, row_token, n_used, (tile_expert, n_used, first, run, n_runs, run_expert)


def _moe_swiglu(x2, norm_g, router, w_gate, w_up, w_down, out_norm_g=None):
    h, route = _norm_route(x2, norm_g, router)
    top_i = route[:, 2:4].astype(jnp.int32)
    dest, row_token, n_used, tables = _routing_tables(top_i)
    xs = _gather_rows(h, row_token, n_used)
    hid = _grouped_call(_swiglu_tile, tables, xs, [w_gate, w_up], MOE_UP_BN, BF16, "moe_up")
    y = _grouped_call(_plain_tile, tables, hid, [w_down], MOE_DOWN_BN, F32, "moe_down")
    return _moe_combine(x2, route, y, dest, out_norm_g)


def _split_in_weights(w):
    sizes = (1024, 256, 256, 1024, 1024, 1024, 256, 256, 256, 256, 256, 256, 24, 1024, 1024, 1024)
    offs = np.concatenate([[0], np.cumsum(sizes)])
    (a_q, a_k, a_v, b_val, b_gate, c_q, c_kc, c_vc, c_ks, c_vs, c_kw, c_vw, c_g,
     d_q, d_k, d_v) = [w[:, int(offs[i]):int(offs[i + 1])] for i in range(len(sizes))]
    wa = jnp.concatenate([a_q, a_k, a_v, c_q, c_ks, c_vs, c_kw, c_vw, d_q, d_k, d_v], axis=1)
    pad = jnp.zeros((w.shape[0], PF_COLS - PF_G - c_g.shape[1]), w.dtype)
    wf = jnp.concatenate([b_val, b_gate, c_kc, c_vc, c_g, pad], axis=1)
    return wa.astype(BF16), wf.astype(BF16)


def _mixer_layer(x2, batch, seq, layer, p, wbias, cbias):
    h = _rmsnorm(x2, p["attn_norm_g"][layer], out_dtype=BF16, name="attn_norm")
    wa, wf = _split_in_weights(p["w_in"][layer])
    pa = _matmul(h, wa, bm=1024, bn=512, out_dtype=BF16, name="in_proj_attn")
    pf = _matmul(h, wf, bm=1024, bn=PF_COLS // 3, out_dtype=F32, name="in_proj_f32")

    o_a = _window_attention(pa, batch, seq, PA_AQ, PA_AK, PA_AV, wbias, 0,
                            p["swa_sinks"][layer], "swa_attn")
    o_b = _conformer_conv(pf, batch, seq, p["conv_dw"][layer], p["conv_dw_b"][layer],
                          p["conv_ln_g"][layer], p["conv_ln_b"][layer], p["conv_pw"][layer])
    cmp_kv = _compress(pf, batch, seq, p["nsa_cmp_pe"][layer], p["nsa_cmp_w1"][layer],
                       p["nsa_cmp_w2"][layer])
    o_cmp, sel = _cmp_attention(pa, cmp_kv, cbias, batch, seq)
    o_slc = _slc_attention(pa, sel, wbias, p["rel_bias"][:, GROUP_HEADS:], batch, seq)
    o_win = _window_attention(pa, batch, seq, PA_CQ, PA_CKW, PA_CVW, wbias, GROUP_HEADS, None,
                              "nsa_win_attn")
    o_d = _stick_breaking(pa, batch, seq)
    mixed = _mix(o_a, o_b, o_cmp, o_slc, o_win, pf, o_d, p["mix_norm_g"][layer])
    return _matmul(mixed, p["w_out"][layer].astype(BF16), bm=1024, bn=512, out_dtype=F32,
                   residual=x2, name="out_proj")


def kernel(x, attn_norm_g, ffn_norm_g, final_norm_g, w_in, w_out, mix_norm_g, rel_bias,
           swa_sinks, conv_dw, conv_dw_b, conv_ln_g, conv_ln_b, conv_pw, nsa_cmp_pe,
           nsa_cmp_w1, nsa_cmp_w2, ffn_w_gate, ffn_w_up, ffn_w_down, moe_router,
           moe_w_gate, moe_w_up, moe_w_down):
    batch, seq, _ = x.shape
    t = batch * seq
    p = dict(attn_norm_g=attn_norm_g, w_in=w_in, w_out=w_out, mix_norm_g=mix_norm_g,
             rel_bias=rel_bias, swa_sinks=swa_sinks, conv_dw=conv_dw, conv_dw_b=conv_dw_b,
             conv_ln_g=conv_ln_g, conv_ln_b=conv_ln_b, conv_pw=conv_pw, nsa_cmp_pe=nsa_cmp_pe,
             nsa_cmp_w1=nsa_cmp_w1, nsa_cmp_w2=nsa_cmp_w2)
    wbias = _win_bias(rel_bias)
    cbias = _cmp_bias(rel_bias, seq, GROUP_HEADS)
    x2 = x.reshape(t, D_MODEL)
    for layer in range(DEPTH):
        x2 = _mixer_layer(x2, batch, seq, layer, p, wbias, cbias)
        i = layer // 2
        if layer % 2 == 0:
            hf = _rmsnorm(x2, ffn_norm_g[layer], out_dtype=BF16, name="ffn_norm")
            hid = _swiglu_up(hf, ffn_w_gate[i].astype(BF16), ffn_w_up[i].astype(BF16),
                             bm=1024, bn=256, name="ffn_up")
            x2 = _matmul_ksplit_res(hid, ffn_w_down[i].astype(BF16), x2,
                                    bm=1024, bn=512, bk=D_FF // 2, name="ffn_down")
        else:
            last = layer == DEPTH - 1
            x2 = _moe_swiglu(x2, ffn_norm_g[layer], moe_router[i], moe_w_gate[i], moe_w_up[i],
                             moe_w_down[i], out_norm_g=final_norm_g if last else None)
    if DEPTH % 2 == 1:
        x2 = _rmsnorm(x2, final_norm_g, out_dtype=F32, name="final_norm")
    return x2.reshape(batch, seq, D_MODEL)
```

```python
import functools
import math

import jax
import jax.numpy as jnp
import numpy as np
from jax import lax
from jax.experimental import pallas as pl
from jax.experimental.pallas import tpu as pltpu

D_MODEL = 4096
DEPTH = 2
HEAD_DIM = 128
N_MIXERS = 4
GROUP_WIDTH = D_MODEL // N_MIXERS
GROUP_HEADS = GROUP_WIDTH // HEAD_DIM
KV_HEADS = 2
REP = GROUP_HEADS // KV_HEADS
WINDOW = 128
Q_BLOCK = 128
CONV_WIDTH = 31
CMP_LEN = 32
CMP_STRIDE = 16
SLC_LEN = 64
N_SELECT = 16
N_BRANCH = 3
N_BUCKETS = 32
MAX_DISTANCE = 128
D_FF = 11008
N_EXPERTS = 8
TOP_K = 2
D_EXPERT = D_FF // 2
NORM_EPS = 1e-6
SCALE = HEAD_DIM ** -0.5

VMEM_LIMIT_BYTES = 56 * 1024 * 1024
LANES = 128

F32 = jnp.float32
BF16 = jnp.bfloat16
NEG_INF = float("-inf")
F32_TINY = float(np.finfo(np.float32).tiny)

PA_AQ, PA_AK, PA_AV = 0, 1024, 1280
PA_CQ, PA_CKS, PA_CVS, PA_CKW, PA_CVW = 1536, 2560, 2816, 3072, 3328
PA_DQ, PA_DK, PA_DV = 3584, 4608, 5632
PA_COLS = 6656
PF_VAL, PF_GATE, PF_KC, PF_VC, PF_G = 0, 1024, 2048, 2304, 2560
PF_COLS = 2688


def _params(*sem):
    return pltpu.CompilerParams(dimension_semantics=sem, vmem_limit_bytes=VMEM_LIMIT_BYTES)


def _bucket_thresholds():
    n = np.arange(0, 4 * MAX_DISTANCE)
    max_exact = N_BUCKETS // 2
    nf = np.maximum(n, 1).astype(np.float32)
    large = max_exact + (np.log(nf / max_exact) / math.log(MAX_DISTANCE / max_exact)
                         * (N_BUCKETS - max_exact)).astype(np.int32)
    large = np.minimum(large, N_BUCKETS - 1)
    bucket = np.where(n < max_exact, n, large)
    return [int(np.argmax(bucket >= k)) for k in range(N_BUCKETS)]


BUCKET_THRESHOLDS = _bucket_thresholds()


def _mm_kernel(x_ref, w_ref, o_ref):
    o_ref[...] = jnp.dot(x_ref[...], w_ref[...], preferred_element_type=F32).astype(o_ref.dtype)


def _mm_res_kernel(x_ref, w_ref, r_ref, o_ref):
    acc = jnp.dot(x_ref[...], w_ref[...], preferred_element_type=F32)
    o_ref[...] = (r_ref[...] + acc).astype(o_ref.dtype)


def _mm_res_ksplit_kernel(x_ref, w_ref, r_ref, o_ref):
    @pl.when(pl.program_id(2) == 0)
    def _():
        o_ref[...] = r_ref[...]

    o_ref[...] += jnp.dot(x_ref[...], w_ref[...], preferred_element_type=F32)


def _matmul(x, w, *, bm, bn, out_dtype, residual=None, name="matmul"):
    m, k = x.shape
    _, n = w.shape
    assert m % bm == 0 and n % bn == 0
    in_specs = [pl.BlockSpec((bm, k), lambda i, j: (i, 0)),
                pl.BlockSpec((k, bn), lambda i, j: (0, j))]
    args = [x, w]
    body = _mm_kernel
    if residual is not None:
        in_specs.append(pl.BlockSpec((bm, bn), lambda i, j: (i, j)))
        args.append(residual)
        body = _mm_res_kernel
    return pl.pallas_call(
        body,
        out_shape=jax.ShapeDtypeStruct((m, n), out_dtype),
        grid=(m // bm, n // bn),
        in_specs=in_specs,
        out_specs=pl.BlockSpec((bm, bn), lambda i, j: (i, j)),
        compiler_params=_params("parallel", "arbitrary"),
        name=name,
    )(*args)


def _matmul_ksplit_res(x, w, residual, *, bm, bn, bk, name="matmul_ksplit"):
    m, k = x.shape
    _, n = w.shape
    assert m % bm == 0 and n % bn == 0 and k % bk == 0
    return pl.pallas_call(
        _mm_res_ksplit_kernel,
        out_shape=jax.ShapeDtypeStruct((m, n), F32),
        grid=(m // bm, n // bn, k // bk),
        in_specs=[pl.BlockSpec((bm, bk), lambda i, j, kk: (i, kk)),
                  pl.BlockSpec((bk, bn), lambda i, j, kk: (kk, j)),
                  pl.BlockSpec((bm, bn), lambda i, j, kk: (i, j))],
        out_specs=pl.BlockSpec((bm, bn), lambda i, j, kk: (i, j)),
        compiler_params=_params("parallel", "arbitrary", "arbitrary"),
        name=name,
    )(x, w, residual)


def _swiglu_kernel(x_ref, wg_ref, wu_ref, o_ref):
    x = x_ref[...]
    g = jnp.dot(x, wg_ref[...], preferred_element_type=F32)
    u = jnp.dot(x, wu_ref[...], preferred_element_type=F32)
    o_ref[...] = (g * jax.nn.sigmoid(g) * u).astype(o_ref.dtype)


def _swiglu_up(x, wg, wu, *, bm, bn, name="swiglu_up"):
    m, k = x.shape
    _, n = wg.shape
    assert m % bm == 0 and n % bn == 0
    return pl.pallas_call(
        _swiglu_kernel,
        out_shape=jax.ShapeDtypeStruct((m, n), BF16),
        grid=(m // bm, n // bn),
        in_specs=[pl.BlockSpec((bm, k), lambda i, j: (i, 0)),
                  pl.BlockSpec((k, bn), lambda i, j: (0, j)),
                  pl.BlockSpec((k, bn), lambda i, j: (0, j))],
        out_specs=pl.BlockSpec((bm, bn), lambda i, j: (i, j)),
        compiler_params=_params("parallel", "arbitrary"),
        name=name,
    )(x, wg, wu)


def _rmsnorm_kernel(x_ref, g_ref, o_ref):
    x = x_ref[...]
    y = x * lax.rsqrt(jnp.mean(x * x, axis=-1, keepdims=True) + NORM_EPS)
    o_ref[...] = (y * g_ref[...]).astype(o_ref.dtype)


def _rmsnorm(x, g, *, out_dtype, bm=256, name="rmsnorm"):
    m, d = x.shape
    return pl.pallas_call(
        _rmsnorm_kernel,
        out_shape=jax.ShapeDtypeStruct((m, d), out_dtype),
        grid=(m // bm,),
        in_specs=[pl.BlockSpec((bm, d), lambda i: (i, 0)),
                  pl.BlockSpec((1, d), lambda i: (0, 0))],
        out_specs=pl.BlockSpec((bm, d), lambda i: (i, 0)),
        compiler_params=_params("parallel"),
        name=name,
    )(x, g.reshape(1, d))


def _bias_of_dist(dist, tab_ref, head):
    out = jnp.full(dist.shape, tab_ref[0, head], F32)
    for k in range(1, N_BUCKETS):
        out = jnp.where(dist >= BUCKET_THRESHOLDS[k], tab_ref[k, head], out)
    return out


def _win_bias_kernel(tab_ref, o_ref):
    h = pl.program_id(0)
    qi = lax.broadcasted_iota(jnp.int32, (Q_BLOCK, 2 * Q_BLOCK), 0)
    kj = lax.broadcasted_iota(jnp.int32, (Q_BLOCK, 2 * Q_BLOCK), 1)
    o_ref[0] = _bias_of_dist(qi + Q_BLOCK - kj, tab_ref, h)


def _win_bias(rel_bias):
    nh = rel_bias.shape[1]
    return pl.pallas_call(
        _win_bias_kernel,
        out_shape=jax.ShapeDtypeStruct((nh, Q_BLOCK, 2 * Q_BLOCK), F32),
        grid=(nh,),
        in_specs=[pl.BlockSpec(memory_space=pltpu.SMEM)],
        out_specs=pl.BlockSpec((1, Q_BLOCK, 2 * Q_BLOCK), lambda h: (h, 0, 0)),
        compiler_params=_params("arbitrary"),
        name="win_bias",
    )(rel_bias)


def _cmp_bias_kernel(tab_ref, o_ref, *, head0, rows):
    h = pl.program_id(0) + head0
    n = pl.program_id(1)
    ncmp = o_ref.shape[2]
    t = n * rows + lax.broadcasted_iota(jnp.int32, (rows, ncmp), 0)
    c = lax.broadcasted_iota(jnp.int32, (rows, ncmp), 1)
    o_ref[0] = _bias_of_dist(t - c * CMP_STRIDE - (CMP_LEN - 1), tab_ref, h)


def _cmp_bias(rel_bias, seq, head0, rows=512):
    ncmp = seq // CMP_STRIDE
    rows = min(rows, seq)
    return pl.pallas_call(
        functools.partial(_cmp_bias_kernel, head0=head0, rows=rows),
        out_shape=jax.ShapeDtypeStruct((GROUP_HEADS, seq, ncmp), F32),
        grid=(GROUP_HEADS, seq // rows),
        in_specs=[pl.BlockSpec(memory_space=pltpu.SMEM)],
        out_specs=pl.BlockSpec((1, rows, ncmp), lambda h, n: (h, n, 0)),
        compiler_params=_params("arbitrary", "arbitrary"),
        name="cmp_bias",
    )(rel_bias)


def _stack_heads(q):
    return jnp.concatenate([q[:, r * HEAD_DIM:(r + 1) * HEAD_DIM] for r in range(REP)], axis=0)


def _store_heads(o_ref, o, rows):
    for r in range(REP):
        o_ref[:, r * HEAD_DIM:(r + 1) * HEAD_DIM] = o[r * rows:(r + 1) * rows].astype(o_ref.dtype)


def _qk(q, k):
    return lax.dot_general(q, k, (((1,), (1,)), ((), ())), preferred_element_type=F32)


def _window_kernel(q0_ref, q1_ref, kp_ref, kc_ref, vp_ref, vc_ref, bias_ref, sink_ref, o_ref, *,
                   has_sink):
    n = pl.program_id(1)
    qi = lax.broadcasted_iota(jnp.int32, (Q_BLOCK, 2 * Q_BLOCK), 0)
    kj = lax.broadcasted_iota(jnp.int32, (Q_BLOCK, 2 * Q_BLOCK), 1)
    dist = qi + Q_BLOCK - kj
    mask = ((dist >= 0) & (dist < WINDOW) & ((n > 0) | (kj >= Q_BLOCK)))[None]
    for g, q_ref in enumerate((q0_ref, q1_ref)):
        cols = slice(g * HEAD_DIM, (g + 1) * HEAD_DIM)
        heads = slice(g * REP, (g + 1) * REP)
        q4 = _stack_heads(q_ref[...])
        kcat = jnp.concatenate([kp_ref[:, cols], kc_ref[:, cols]], axis=0)
        vcat = jnp.concatenate([vp_ref[:, cols], vc_ref[:, cols]], axis=0)
        s = _qk(q4, kcat).reshape(REP, Q_BLOCK, 2 * Q_BLOCK) * SCALE + bias_ref[heads]
        s = jnp.where(mask, s, NEG_INF)
        m = jnp.max(s, axis=-1, keepdims=True)
        if has_sink:
            sink = sink_ref[heads]
            m = jnp.maximum(m, sink)
        p = jnp.where(mask, jnp.exp(s - m), 0.0)
        den = jnp.sum(p, axis=-1, keepdims=True)
        if has_sink:
            den = den + jnp.exp(sink - m)
        p = p / jnp.maximum(den, F32_TINY)
        o = jnp.dot(p.reshape(REP * Q_BLOCK, 2 * Q_BLOCK).astype(BF16), vcat,
                    preferred_element_type=F32)
        for r in range(REP):
            c0 = (g * REP + r) * HEAD_DIM
            o_ref[:, c0:c0 + HEAD_DIM] = o[r * Q_BLOCK:(r + 1) * Q_BLOCK].astype(o_ref.dtype)


def _window_attention(pa, batch, seq, q_col, k_col, v_col, bias, bias_head0, sinks, name):
    assert KV_HEADS == 2
    nb = seq // Q_BLOCK
    qw = REP * HEAD_DIM
    kvw = KV_HEADS * HEAD_DIM
    has_sink = sinks is not None
    sink_arr = (sinks if has_sink else jnp.zeros((GROUP_HEADS,), F32)).reshape(GROUP_HEADS, 1, 1)

    def cur(col):
        return lambda b, n: (b * nb + n, col // kvw)

    def prev(col):
        return lambda b, n: (b * nb + jnp.maximum(n - 1, 0), col // kvw)

    kv_block = (Q_BLOCK, kvw)
    return pl.pallas_call(
        functools.partial(_window_kernel, has_sink=has_sink),
        out_shape=jax.ShapeDtypeStruct((batch * seq, GROUP_WIDTH), F32),
        grid=(batch, nb),
        in_specs=[pl.BlockSpec((Q_BLOCK, qw), lambda b, n: (b * nb + n, q_col // qw)),
                  pl.BlockSpec((Q_BLOCK, qw), lambda b, n: (b * nb + n, q_col // qw + 1)),
                  pl.BlockSpec(kv_block, prev(k_col)), pl.BlockSpec(kv_block, cur(k_col)),
                  pl.BlockSpec(kv_block, prev(v_col)), pl.BlockSpec(kv_block, cur(v_col)),
                  pl.BlockSpec((GROUP_HEADS, Q_BLOCK, 2 * Q_BLOCK),
                               lambda b, n: (bias_head0 // GROUP_HEADS, 0, 0)),
                  pl.BlockSpec((GROUP_HEADS, 1, 1), lambda b, n: (0, 0, 0))],
        out_specs=pl.BlockSpec((Q_BLOCK, GROUP_WIDTH), lambda b, n: (b * nb + n, 0)),
        compiler_params=_params("parallel", "arbitrary"),
        name=name,
    )(pa, pa, pa, pa, pa, pa, bias, sink_arr)


def _compress_kernel(t_ref, pe_ref, w1_ref, w2_ref, o_ref):
    ncmp = o_ref.shape[3]
    half = CMP_LEN // 2
    pe = pe_ref[0]
    rows = [t_ref[pl.ds(r, ncmp, stride=CMP_STRIDE), :] for r in range(CMP_STRIDE)]
    xa = jnp.concatenate([rows[r] + pe[r:r + 1] for r in range(half)], axis=1).astype(BF16)
    xb = jnp.concatenate([rows[r] + pe[half + r:half + r + 1] for r in range(half)], axis=1).astype(BF16)
    kw = half * HEAD_DIM
    p0 = jnp.dot(xa, w1_ref[0, :kw, :], preferred_element_type=F32)
    p1 = jnp.dot(xb, w1_ref[0, kw:, :], preferred_element_type=F32)
    pre = p0 + pltpu.roll(p1, ncmp - 1, 0)
    hid = pre * jax.nn.sigmoid(pre)
    o_ref[0, 0, 0] = jnp.dot(hid.astype(BF16), w2_ref[0], preferred_element_type=F32).astype(o_ref.dtype)


def _compress(pf, batch, seq, pe, w1, w2):
    ncmp = seq // CMP_STRIDE
    return pl.pallas_call(
        _compress_kernel,
        out_shape=jax.ShapeDtypeStruct((2, batch, KV_HEADS, ncmp, HEAD_DIM), BF16),
        grid=(2, batch, KV_HEADS),
        in_specs=[pl.BlockSpec((seq, HEAD_DIM), lambda kv, b, g: (b, PF_KC // HEAD_DIM + KV_HEADS * kv + g)),
                  pl.BlockSpec((1, CMP_LEN, HEAD_DIM), lambda kv, b, g: (kv, 0, 0)),
                  pl.BlockSpec((1, CMP_LEN * HEAD_DIM, HEAD_DIM), lambda kv, b, g: (kv, 0, 0)),
                  pl.BlockSpec((1, HEAD_DIM, HEAD_DIM), lambda kv, b, g: (kv, 0, 0))],
        out_specs=pl.BlockSpec((1, 1, 1, ncmp, HEAD_DIM), lambda kv, b, g: (kv, b, g, 0, 0)),
        compiler_params=_params("arbitrary", "arbitrary", "arbitrary"),
        name="nsa_compress",
    )(pf, pe, w1.astype(BF16), w2.astype(BF16))


def _cmp_attn_kernel(q_ref, kc_ref, vc_ref, bias_ref, ov_ref, o_ref, sel_ref, *, n_sel):
    n = pl.program_id(2)
    ncmp = kc_ref.shape[3]
    nslc = sel_ref.shape[3]
    q4 = _stack_heads(q_ref[...])
    s = _qk(q4, kc_ref[0, 0, 0]).reshape(REP, Q_BLOCK, ncmp) * SCALE + bias_ref[...]
    t = n * Q_BLOCK + lax.broadcasted_iota(jnp.int32, (Q_BLOCK, ncmp), 0)
    c = lax.broadcasted_iota(jnp.int32, (Q_BLOCK, ncmp), 1)
    vis = (t - c * CMP_STRIDE - (CMP_LEN - 1) >= 0)[None]
    s = jnp.where(vis, s, NEG_INF)
    m = jnp.max(s, axis=-1, keepdims=True)
    m = jnp.where(m == NEG_INF, 0.0, m)
    p = jnp.where(vis, jnp.exp(s - m), 0.0)
    den = jnp.sum(p, axis=-1, keepdims=True)
    p = p / jnp.maximum(den, F32_TINY)
    o = jnp.dot(p.reshape(REP * Q_BLOCK, ncmp).astype(BF16), vc_ref[0, 0, 0], preferred_element_type=F32)
    _store_heads(o_ref, o, Q_BLOCK)

    psum = p[0]
    for r in range(1, REP):
        psum = psum + p[r]
    hi = psum.astype(BF16)
    lo = (psum - hi.astype(F32)).astype(BF16)
    ov = ov_ref[...]
    imp = jnp.dot(hi, ov, preferred_element_type=F32) + jnp.dot(lo, ov, preferred_element_type=F32)
    tq = n * Q_BLOCK + lax.broadcasted_iota(jnp.int32, (Q_BLOCK, nslc), 0)
    blk = lax.broadcasted_iota(jnp.int32, (Q_BLOCK, nslc), 1)
    cur = jnp.right_shift(tq, SLC_LEN.bit_length() - 1)
    forced = (blk == 0) | (blk == cur) | (blk == cur - 1)
    score = jnp.where(forced, jnp.inf, jnp.where(blk <= cur, imp, NEG_INF))
    rank = jnp.zeros((Q_BLOCK, nslc), F32)
    for k in range(nslc):
        col = score[:, k:k + 1]
        ahead = (col > score) | ((col == score) & (blk > k))
        rank = rank + jnp.where(ahead, 1.0, 0.0)
    sel_ref[0, 0] = jnp.where(rank < n_sel, 1.0, 0.0)


def _overlap_matrix(ncmp, nslc):
    cs = np.arange(ncmp)[:, None] * CMP_STRIDE
    ss = np.arange(nslc)[None, :] * SLC_LEN
    return ((cs < ss + SLC_LEN) & (cs + CMP_LEN > ss)).astype(np.float32)


def _cmp_attention(pa, cmp_kv, cbias, batch, seq):
    nb = seq // Q_BLOCK
    ncmp = seq // CMP_STRIDE
    nslc = seq // SLC_LEN
    qw = REP * HEAD_DIM
    ov = jnp.asarray(_overlap_matrix(ncmp, nslc), BF16)
    return pl.pallas_call(
        functools.partial(_cmp_attn_kernel, n_sel=min(N_SELECT, nslc)),
        out_shape=(jax.ShapeDtypeStruct((batch * seq, GROUP_WIDTH), F32),
                   jax.ShapeDtypeStruct((batch, KV_HEADS, seq, nslc), F32)),
        grid=(batch, KV_HEADS, nb),
        in_specs=[pl.BlockSpec((Q_BLOCK, qw), lambda b, g, n: (b * nb + n, PA_CQ // qw + g)),
                  pl.BlockSpec((1, 1, 1, ncmp, HEAD_DIM), lambda b, g, n: (0, b, g, 0, 0)),
                  pl.BlockSpec((1, 1, 1, ncmp, HEAD_DIM), lambda b, g, n: (1, b, g, 0, 0)),
                  pl.BlockSpec((REP, Q_BLOCK, ncmp), lambda b, g, n: (g, n, 0)),
                  pl.BlockSpec((ncmp, nslc), lambda b, g, n: (0, 0))],
        out_specs=(pl.BlockSpec((Q_BLOCK, qw), lambda b, g, n: (b * nb + n, g)),
                   pl.BlockSpec((1, 1, Q_BLOCK, nslc), lambda b, g, n: (b, g, n, 0))),
        compiler_params=_params("parallel", "parallel", "arbitrary"),
        name="nsa_cmp_attn",
    )(pa, cmp_kv, cmp_kv, cbias, ov)


FAR_TILE = 8 * Q_BLOCK


def _slc_attn_kernel(q_ref, k_ref, v_ref, sel_ref, bias_ref, far_bias_ref, efar_ref, enear_ref, o_ref):
    n = pl.program_id(2)
    q4 = _stack_heads(q_ref[...])
    selb = sel_ref[0, 0].astype(BF16)
    far_bias = far_bias_ref[...]
    rows = REP * Q_BLOCK

    def online(carry, s, mask, v):
        m, l, acc = carry
        s = jnp.where(mask[None], s, NEG_INF)
        m_new = jnp.maximum(m, jnp.max(s, axis=-1, keepdims=True))
        m_safe = jnp.where(m_new == NEG_INF, 0.0, m_new)
        alpha = jnp.exp(m - m_safe)
        p = jnp.exp(s - m_safe)
        l = alpha * l + jnp.sum(p, axis=-1, keepdims=True)
        pv = jnp.dot(p.reshape(rows, p.shape[-1]).astype(BF16), v, preferred_element_type=F32)
        acc = alpha * acc + pv.reshape(REP, Q_BLOCK, HEAD_DIM)
        return m_new, l, acc

    far_end = (n - 1) * Q_BLOCK

    def far_step(j, carry):
        k0 = pl.multiple_of(j * FAR_TILE, FAR_TILE)
        k = k_ref[pl.ds(k0, FAR_TILE), :]
        v = v_ref[pl.ds(k0, FAR_TILE), :]
        s = _qk(q4, k).reshape(REP, Q_BLOCK, FAR_TILE) * SCALE + far_bias
        picked = jnp.dot(selb, efar_ref[j], preferred_element_type=F32) > 0.5
        kidx = k0 + lax.broadcasted_iota(jnp.int32, (Q_BLOCK, FAR_TILE), 1)
        return online(carry, s, picked & (kidx < far_end), v)

    init = (jnp.full((REP, Q_BLOCK, 1), NEG_INF, F32), jnp.zeros((REP, Q_BLOCK, 1), F32),
            jnp.zeros((REP, Q_BLOCK, HEAD_DIM), F32))
    n_far = (jnp.maximum(far_end, 0) + FAR_TILE - 1) // FAR_TILE
    carry = lax.fori_loop(0, n_far, far_step, init)

    pb = jnp.maximum(n - 1, 0)
    p0 = pl.multiple_of(pb * Q_BLOCK, Q_BLOCK)
    c0 = pl.multiple_of(n * Q_BLOCK, Q_BLOCK)
    kcat = jnp.concatenate([k_ref[pl.ds(p0, Q_BLOCK), :], k_ref[pl.ds(c0, Q_BLOCK), :]], axis=0)
    vcat = jnp.concatenate([v_ref[pl.ds(p0, Q_BLOCK), :], v_ref[pl.ds(c0, Q_BLOCK), :]], axis=0)
    s = _qk(q4, kcat).reshape(REP, Q_BLOCK, 2 * Q_BLOCK) * SCALE + bias_ref[...]
    picked = jnp.concatenate(
        [jnp.dot(selb, enear_ref[pb], preferred_element_type=F32),
         jnp.dot(selb, enear_ref[n], preferred_element_type=F32)], axis=1) > 0.5
    qi = lax.broadcasted_iota(jnp.int32, (Q_BLOCK, 2 * Q_BLOCK), 0)
    kj = lax.broadcasted_iota(jnp.int32, (Q_BLOCK, 2 * Q_BLOCK), 1)
    causal = (qi + Q_BLOCK - kj >= 0) & ((n > 0) | (kj >= Q_BLOCK))
    m, l, acc = online(carry, s, picked & causal, vcat)
    o = acc / jnp.maximum(l, F32_TINY)
    for r in range(REP):
        o_ref[:, r * HEAD_DIM:(r + 1) * HEAD_DIM] = o[r].astype(o_ref.dtype)


def _expand_matrix(seq, tile):
    nslc = seq // SLC_LEN
    key = np.arange(seq).reshape(seq // tile, 1, tile)
    return (key // SLC_LEN == np.arange(nslc)[None, :, None]).astype(np.float32)


def _slc_attention(pa, sel, wbias, rel_bias_c, batch, seq):
    nb = seq // Q_BLOCK
    nslc = seq // SLC_LEN
    qw = REP * HEAD_DIM
    nfar = max(seq // FAR_TILE, 1)
    efar = jnp.asarray(_expand_matrix(max(seq, FAR_TILE), FAR_TILE)[:, :nslc], BF16)
    enear = jnp.asarray(_expand_matrix(seq, Q_BLOCK), BF16)
    far_bias = rel_bias_c[N_BUCKETS - 1].reshape(GROUP_HEADS, 1, 1)
    return pl.pallas_call(
        _slc_attn_kernel,
        out_shape=jax.ShapeDtypeStruct((batch * seq, GROUP_WIDTH), F32),
        grid=(batch, KV_HEADS, nb),
        in_specs=[pl.BlockSpec((Q_BLOCK, qw), lambda b, g, n: (b * nb + n, PA_CQ // qw + g)),
                  pl.BlockSpec((seq, HEAD_DIM), lambda b, g, n: (b, PA_CKS // HEAD_DIM + g)),
                  pl.BlockSpec((seq, HEAD_DIM), lambda b, g, n: (b, PA_CVS // HEAD_DIM + g)),
                  pl.BlockSpec((1, 1, Q_BLOCK, nslc), lambda b, g, n: (b, g, n, 0)),
                  pl.BlockSpec((REP, Q_BLOCK, 2 * Q_BLOCK),
                               lambda b, g, n: (GROUP_HEADS // REP + g, 0, 0)),
                  pl.BlockSpec((REP, 1, 1), lambda b, g, n: (g, 0, 0)),
                  pl.BlockSpec((nfar, nslc, FAR_TILE), lambda b, g, n: (0, 0, 0)),
                  pl.BlockSpec((nb, nslc, Q_BLOCK), lambda b, g, n: (0, 0, 0))],
        out_specs=pl.BlockSpec((Q_BLOCK, qw), lambda b, g, n: (b * nb + n, g)),
        compiler_params=_params("parallel", "parallel", "arbitrary"),
        name="nsa_slc_attn",
    )(pa, pa, pa, sel, wbias, far_bias, efar, enear)


SB_TK = 256
SB_BLOCKS = 4
SB_STEP = SB_BLOCKS * SB_TK
LOG2E = math.log2(math.e)


def _sb_kernel(q_ref, k_ref, v_ref, u_ref, o_ref):
    n = pl.program_id(2)
    u2 = u_ref[...]
    q = q_ref[...]

    def step(j, carry, masked):
        later, acc = carry
        k0 = pl.multiple_of(j * SB_STEP, SB_STEP)
        k = k_ref[pl.ds(k0, SB_STEP), :]
        v = v_ref[pl.ds(k0, SB_STEP), :]
        z = _qk(q, k) * (SCALE * LOG2E)
        zneg = jnp.minimum(z, 0.0)
        zpos_neg = zneg - z
        t = jnp.log2(1.0 + jnp.exp2(zneg + zpos_neg))
        log_keep = zpos_neg - t
        if masked:
            qi = lax.broadcasted_iota(jnp.int32, (SB_STEP, SB_STEP), 0)
            kj = lax.broadcasted_iota(jnp.int32, (SB_STEP, SB_STEP), 1)
            before = kj < qi
            log_keep = jnp.where(before, log_keep, 0.0)
        hi = log_keep.astype(BF16)
        lo = (log_keep - hi.astype(F32)).astype(BF16)
        blocks = [jnp.concatenate([hi[:, b * SB_TK:(b + 1) * SB_TK], lo[:, b * SB_TK:(b + 1) * SB_TK]],
                                  axis=1) for b in range(SB_BLOCKS)]
        suffix = jnp.dot(jnp.concatenate(blocks, axis=0), u2, preferred_element_type=F32)
        parts = [None] * SB_BLOCKS
        for b in range(SB_BLOCKS - 1, -1, -1):
            sfx = suffix[b * SB_STEP:(b + 1) * SB_STEP]
            parts[b] = jnp.exp2(z[:, b * SB_TK:(b + 1) * SB_TK] + sfx + later)
            later = later + sfx[:, 0:1]
        a = jnp.concatenate(parts, axis=1)
        if masked:
            a = jnp.where(before, a, 0.0)
        acc = acc + jnp.dot(a.astype(BF16), v, preferred_element_type=F32)
        return later, acc

    carry = (jnp.zeros((SB_STEP, 1), F32), jnp.zeros((SB_STEP, HEAD_DIM), F32))
    carry = step(n, carry, True)
    carry = lax.fori_loop(0, n, lambda i, c: step(n - 1 - i, c, False), carry)
    o_ref[...] = carry[1].astype(o_ref.dtype)


def _stick_breaking(pa, batch, seq):
    assert seq % SB_STEP == 0
    nb = seq // SB_STEP
    tri = np.tril(np.ones((SB_TK, SB_TK), np.float32))
    u2 = jnp.asarray(np.concatenate([tri, tri], axis=0), BF16)
    return pl.pallas_call(
        _sb_kernel,
        out_shape=jax.ShapeDtypeStruct((batch * seq, GROUP_WIDTH), F32),
        grid=(batch, GROUP_HEADS, nb),
        in_specs=[pl.BlockSpec((SB_STEP, HEAD_DIM), lambda b, h, n: (b * nb + n, PA_DQ // HEAD_DIM + h)),
                  pl.BlockSpec((seq, HEAD_DIM), lambda b, h, n: (b, PA_DK // HEAD_DIM + h)),
                  pl.BlockSpec((seq, HEAD_DIM), lambda b, h, n: (b, PA_DV // HEAD_DIM + h)),
                  pl.BlockSpec((2 * SB_TK, SB_TK), lambda b, h, n: (0, 0))],
        out_specs=pl.BlockSpec((SB_STEP, HEAD_DIM), lambda b, h, n: (b * nb + n, h)),
        compiler_params=_params("parallel", "parallel", "arbitrary"),
        name="stick_breaking",
    )(pa, pa, pa, u2)


CONV_TILE = 256
CONV_HALO = 32


def _conv_kernel(val_ref, gate_ref, hval_ref, hgate_ref, dw_ref, dwb_ref, lng_ref, lnb_ref, pw_ref,
                 o_ref, ext_ref):
    n = pl.program_id(1)
    halo = hval_ref[...] * jax.nn.sigmoid(hgate_ref[...])
    ext_ref[0:CONV_HALO, :] = jnp.where(n > 0, halo, 0.0)
    ext_ref[CONV_HALO:, :] = val_ref[...] * jax.nn.sigmoid(gate_ref[...])
    first = CONV_HALO - (CONV_WIDTH - 1)
    acc = jnp.zeros((CONV_TILE, GROUP_WIDTH), F32) + dwb_ref[...]
    for w in range(CONV_WIDTH):
        acc = acc + ext_ref[first + w:first + w + CONV_TILE, :] * dw_ref[w:w + 1, :]
    mu = jnp.mean(acc, axis=-1, keepdims=True)
    cen = acc - mu
    var = jnp.mean(cen * cen, axis=-1, keepdims=True)
    y = cen * lax.rsqrt(var + NORM_EPS) * lng_ref[...] + lnb_ref[...]
    y = y * jax.nn.sigmoid(y)
    o_ref[...] = jnp.dot(y.astype(BF16), pw_ref[...], preferred_element_type=F32).astype(o_ref.dtype)


def _conformer_conv(pf, batch, seq, dw, dw_b, ln_g, ln_b, pw):
    nt = seq // CONV_TILE
    hpt = CONV_TILE // CONV_HALO
    c = GROUP_WIDTH
    vec = pl.BlockSpec((1, c), lambda b, n: (0, 0))

    def halo(col):
        return lambda b, n: (jnp.maximum((b * nt + n) * hpt - 1, 0), col // c)

    return pl.pallas_call(
        _conv_kernel,
        out_shape=jax.ShapeDtypeStruct((batch * seq, c), F32),
        grid=(batch, nt),
        in_specs=[pl.BlockSpec((CONV_TILE, c), lambda b, n: (b * nt + n, PF_VAL // c)),
                  pl.BlockSpec((CONV_TILE, c), lambda b, n: (b * nt + n, PF_GATE // c)),
                  pl.BlockSpec((CONV_HALO, c), halo(PF_VAL)),
                  pl.BlockSpec((CONV_HALO, c), halo(PF_GATE)),
                  pl.BlockSpec((CONV_WIDTH, c), lambda b, n: (0, 0)),
                  vec, vec, vec,
                  pl.BlockSpec((c, c), lambda b, n: (0, 0))],
        out_specs=pl.BlockSpec((CONV_TILE, c), lambda b, n: (b * nt + n, 0)),
        scratch_shapes=[pltpu.VMEM((CONV_HALO + CONV_TILE, c), F32)],
        compiler_params=_params("parallel", "arbitrary"),
        name="conformer_conv",
    )(pf, pf, pf, pf, dw, dw_b.reshape(1, c), ln_g.reshape(1, c), ln_b.reshape(1, c), pw.astype(BF16))


MIX_TILE = 256


def _mix_kernel(oa_ref, ob_ref, ocmp_ref, oslc_ref, owin_ref, gl_ref, od_ref, g_ref, o_ref):
    gates = jax.nn.sigmoid(gl_ref[...])

    def norm_store(x, grp):
        y = x * lax.rsqrt(jnp.mean(x * x, axis=-1, keepdims=True) + NORM_EPS)
        sl = slice(grp * GROUP_WIDTH, (grp + 1) * GROUP_WIDTH)
        o_ref[:, sl] = (y * g_ref[:, sl]).astype(o_ref.dtype)

    norm_store(oa_ref[...], 0)
    norm_store(ob_ref[...], 1)
    heads = []
    for h in range(GROUP_HEADS):
        sl = slice(h * HEAD_DIM, (h + 1) * HEAD_DIM)
        c0 = h * N_BRANCH
        heads.append(gates[:, c0:c0 + 1] * ocmp_ref[:, sl] + gates[:, c0 + 1:c0 + 2] * oslc_ref[:, sl]
                     + gates[:, c0 + 2:c0 + 3] * owin_ref[:, sl])
    norm_store(jnp.concatenate(heads, axis=1), 2)
    norm_store(od_ref[...], 3)


def _mix(o_a, o_b, o_cmp, o_slc, o_win, pf, o_d, g):
    t = o_a.shape[0]
    grp = pl.BlockSpec((MIX_TILE, GROUP_WIDTH), lambda i: (i, 0))
    return pl.pallas_call(
        _mix_kernel,
        out_shape=jax.ShapeDtypeStruct((t, D_MODEL), BF16),
        grid=(t // MIX_TILE,),
        in_specs=[grp, grp, grp, grp, grp,
                  pl.BlockSpec((MIX_TILE, LANES), lambda i: (i, PF_G // LANES)),
                  grp,
                  pl.BlockSpec((1, D_MODEL), lambda i: (0, 0))],
        out_specs=pl.BlockSpec((MIX_TILE, D_MODEL), lambda i: (i, 0)),
        compiler_params=_params("parallel"),
        name="mix_norm",
    )(o_a, o_b, o_cmp, o_slc, o_win, pf, o_d, g.reshape(1, D_MODEL))


MOE_TILE = 512
MOE_UP_BN = 512
MOE_DOWN_BN = 512
ROUTE_TILE = 256
GATHER_TILE = 256
COMBINE_TILE = 128


def _norm_route_kernel(x_ref, g_ref, r_ref, h_ref, route_ref):
    x = x_ref[...]
    y = x * lax.rsqrt(jnp.mean(x * x, axis=-1, keepdims=True) + NORM_EPS) * g_ref[...]
    h_ref[...] = y
    logits = jnp.dot(y.astype(BF16), r_ref[...], preferred_element_type=F32)
    lane = lax.broadcasted_iota(jnp.int32, logits.shape, 1)
    logits = jnp.where(lane < N_EXPERTS, logits, NEG_INF)
    e = jnp.exp(logits - jnp.max(logits, axis=-1, keepdims=True))
    probs = e / jnp.sum(e, axis=-1, keepdims=True)
    p1 = jnp.max(probs, axis=-1, keepdims=True)
    i1 = jnp.min(jnp.where(probs == p1, lane, LANES), axis=-1, keepdims=True)
    rest = jnp.where(lane == i1, -1.0, probs)
    p2 = jnp.max(rest, axis=-1, keepdims=True)
    i2 = jnp.min(jnp.where(rest == p2, lane, LANES), axis=-1, keepdims=True)
    tot = p1 + p2
    route_ref[...] = jnp.where(lane == 0, p1 / tot,
                               jnp.where(lane == 1, p2 / tot,
                                         jnp.where(lane == 2, i1.astype(F32),
                                                   jnp.where(lane == 3, i2.astype(F32), 0.0))))


def _norm_route(x, g, router):
    m, d = x.shape
    rpad = jnp.zeros((d, LANES), BF16).at[:, :N_EXPERTS].set(router.astype(BF16))
    return pl.pallas_call(
        _norm_route_kernel,
        out_shape=(jax.ShapeDtypeStruct((m, d), F32), jax.ShapeDtypeStruct((m, LANES), F32)),
        grid=(m // ROUTE_TILE,),
        in_specs=[pl.BlockSpec((ROUTE_TILE, d), lambda i: (i, 0)),
                  pl.BlockSpec((1, d), lambda i: (0, 0)),
                  pl.BlockSpec((d, LANES), lambda i: (0, 0))],
        out_specs=(pl.BlockSpec((ROUTE_TILE, d), lambda i: (i, 0)),
                   pl.BlockSpec((ROUTE_TILE, LANES), lambda i: (i, 0))),
        compiler_params=_params("parallel"),
        name="ffn_norm_route",
    )(x, g.reshape(1, d), rpad)


def _row_copy(src_hbm, dst_ref, sem, src_row, dst_row):
    return pltpu.make_async_copy(src_hbm.at[pl.ds(src_row, 1), :], dst_ref.at[pl.ds(dst_row, 1), :], sem)


def _gather_kernel(tok_ref, nu_ref, h_hbm, o_ref, buf_ref, sem):
    i = pl.program_id(0)
    n_live = nu_ref[0] * (MOE_TILE // GATHER_TILE)

    def issue(tile, slot):
        def start(r, c):
            _row_copy(h_hbm, buf_ref.at[slot], sem.at[slot], tok_ref[tile * GATHER_TILE + r], r).start()
            return c
        lax.fori_loop(0, GATHER_TILE, start, 0)

    @pl.when(i == 0)
    def _():
        issue(0, 0)

    @pl.when(i + 1 < n_live)
    def _():
        issue(i + 1, (i + 1) % 2)

    @pl.when(i < n_live)
    def _():
        slot = i % 2

        def wait(r, c):
            _row_copy(h_hbm, buf_ref.at[slot], sem.at[slot], 0, r).wait()
            return c

        lax.fori_loop(0, GATHER_TILE, wait, 0)
        o_ref[...] = buf_ref[slot].astype(o_ref.dtype)

    @pl.when(i >= n_live)
    def _():
        o_ref[...] = jnp.zeros_like(o_ref)


def _gather_rows(h, row_token, n_used):
    rows = row_token.shape[0]
    d = h.shape[1]
    return pl.pallas_call(
        _gather_kernel,
        out_shape=jax.ShapeDtypeStruct((rows, d), BF16),
        grid_spec=pltpu.PrefetchScalarGridSpec(
            num_scalar_prefetch=2,
            grid=(rows // GATHER_TILE,),
            in_specs=[pl.BlockSpec(memory_space=pl.ANY)],
            out_specs=pl.BlockSpec((GATHER_TILE, d), lambda i, tok, nu: (i, 0)),
            scratch_shapes=[pltpu.VMEM((2, GATHER_TILE, d), F32), pltpu.SemaphoreType.DMA((2,))]),
        compiler_params=_params("arbitrary"),
        name="moe_gather",
    )(row_token, n_used, h)


CAST_ROWS = 128


def _swiglu_tile(x, s_ref):
    g = jnp.dot(x, s_ref[0], preferred_element_type=F32)
    u = jnp.dot(x, s_ref[1], preferred_element_type=F32)
    return g * jax.nn.sigmoid(g) * u


def _plain_tile(x, s_ref):
    return jnp.dot(x, s_ref[0], preferred_element_type=F32)


def _grouped_kernel(te_ref, nu_ref, first_ref, run_ref, nruns_ref, rune_ref, x_ref, *rest,
                    n_weights, bn, n_cols, tile_fn):
    w_hbm = rest[:n_weights]
    o_ref, wbuf_ref, s_ref, sem = rest[n_weights:]
    j = pl.program_id(0)
    i = pl.program_id(1)
    nj = pl.num_programs(0)
    k = s_ref.shape[1]
    last_width = n_cols - (pl.cdiv(n_cols, bn) - 1) * bn
    live = i < nu_ref[0]

    def tile_copy(w, expert, jj, slot, width):
        col0 = pl.multiple_of(jj * bn, bn)
        return pltpu.make_async_copy(w_hbm[w].at[expert, :, pl.ds(col0, width)],
                                     wbuf_ref.at[slot, w, :, pl.ds(0, width)], sem.at[slot, w])

    def by_width(jj, fn):
        if last_width == bn:
            fn(bn)
        else:
            pl.when(jj < nj - 1)(lambda: fn(bn))
            pl.when(jj == nj - 1)(lambda: fn(last_width))

    def issue(expert, jj, slot):
        def go(width):
            for w in range(n_weights):
                tile_copy(w, expert, jj, slot, width).start()
        by_width(jj, go)

    def wait_and_cast(expert, jj, slot):
        def go(width):
            for w in range(n_weights):
                tile_copy(w, expert, jj, slot, width).wait()

                def rows(c, carry, w=w):
                    r0 = pl.multiple_of(c * CAST_ROWS, CAST_ROWS)
                    s_ref[w, pl.ds(r0, CAST_ROWS), 0:width] = (
                        wbuf_ref[slot, w, pl.ds(r0, CAST_ROWS), 0:width].astype(BF16))
                    return carry

                lax.fori_loop(0, k // CAST_ROWS, rows, 0)
        by_width(jj, go)

    @pl.when(live & (first_ref[i] == 1))
    def _():
        run = run_ref[i]
        group = j * nruns_ref[0] + run
        slot = group % 2
        expert = te_ref[i]

        @pl.when(group == 0)
        def _():
            issue(expert, j, 0)

        wait_and_cast(expert, j, slot)
        wrap = run + 1 == nruns_ref[0]
        next_run = jnp.where(wrap, 0, run + 1)
        next_j = jnp.where(wrap, j + 1, j)

        @pl.when(next_j < nj)
        def _():
            issue(rune_ref[next_run], next_j, 1 - slot)

    @pl.when(live)
    def _():
        o_ref[...] = tile_fn(x_ref[...], s_ref).astype(o_ref.dtype)

    @pl.when(jnp.logical_not(live))
    def _():
        o_ref[...] = jnp.zeros_like(o_ref)


def _grouped_call(tile_fn, tables, xs, weights, bn, out_dtype, name):
    rows, k = xs.shape
    n = weights[0].shape[2]
    nt = rows // MOE_TILE
    nw = len(weights)
    assert k % CAST_ROWS == 0 and bn % LANES == 0 and n % LANES == 0

    def used(i, nu):
        return jnp.minimum(i, nu[0] - 1)

    body = functools.partial(_grouped_kernel, n_weights=nw, bn=bn, n_cols=n, tile_fn=tile_fn)
    return pl.pallas_call(
        body,
        out_shape=jax.ShapeDtypeStruct((rows, n), out_dtype),
        grid_spec=pltpu.PrefetchScalarGridSpec(
            num_scalar_prefetch=len(tables),
            grid=(pl.cdiv(n, bn), nt),
            in_specs=[pl.BlockSpec((MOE_TILE, k), lambda j, i, te, nu, *_: (used(i, nu), 0))]
            + [pl.BlockSpec(memory_space=pl.ANY) for _ in weights],
            out_specs=pl.BlockSpec((MOE_TILE, bn), lambda j, i, *_: (i, j)),
            scratch_shapes=[pltpu.VMEM((2, nw, k, bn), F32), pltpu.VMEM((nw, k, bn), BF16),
                            pltpu.SemaphoreType.DMA((2, nw))]),
        compiler_params=_params("arbitrary", "arbitrary"),
        name=name,
    )(*tables, xs, *weights)


def _combine_kernel(dest_ref, x_ref, route_ref, y_hbm, *rest, normalize):
    if normalize:
        g_ref, o_ref, y0_ref, y1_ref, sem = rest
    else:
        o_ref, y0_ref, y1_ref, sem = rest
    i = pl.program_id(0)

    def issue(tile, slot):
        def start(r, c):
            tok = tile * COMBINE_TILE + r
            _row_copy(y_hbm, y0_ref.at[slot], sem.at[slot], dest_ref[2 * tok], r).start()
            _row_copy(y_hbm, y1_ref.at[slot], sem.at[slot], dest_ref[2 * tok + 1], r).start()
            return c
        lax.fori_loop(0, COMBINE_TILE, start, 0)

    @pl.when(i == 0)
    def _():
        issue(0, 0)

    @pl.when(i + 1 < pl.num_programs(0))
    def _():
        issue(i + 1, (i + 1) % 2)

    slot = i % 2

    def wait(r, c):
        _row_copy(y_hbm, y0_ref.at[slot], sem.at[slot], 0, r).wait()
        _row_copy(y_hbm, y1_ref.at[slot], sem.at[slot], 0, r).wait()
        return c

    lax.fori_loop(0, COMBINE_TILE, wait, 0)
    route = route_ref[...]
    out = x_ref[...] + route[:, 0:1] * y0_ref[slot] + route[:, 1:2] * y1_ref[slot]
    if normalize:
        out = out * lax.rsqrt(jnp.mean(out * out, axis=-1, keepdims=True) + NORM_EPS) * g_ref[...]
    o_ref[...] = out


def _moe_combine(x2, route, y, dest, out_norm_g=None):
    t, d = x2.shape
    normalize = out_norm_g is not None
    in_specs = [pl.BlockSpec((COMBINE_TILE, d), lambda i, dest: (i, 0)),
                pl.BlockSpec((COMBINE_TILE, LANES), lambda i, dest: (i, 0)),
                pl.BlockSpec(memory_space=pl.ANY)]
    args = [dest, x2, route, y]
    if normalize:
        in_specs.append(pl.BlockSpec((1, d), lambda i, dest: (0, 0)))
        args.append(out_norm_g.reshape(1, d))
    return pl.pallas_call(
        functools.partial(_combine_kernel, normalize=normalize),
        out_shape=jax.ShapeDtypeStruct((t, d), F32),
        grid_spec=pltpu.PrefetchScalarGridSpec(
            num_scalar_prefetch=1,
            grid=(t // COMBINE_TILE,),
            in_specs=in_specs,
            out_specs=pl.BlockSpec((COMBINE_TILE, d), lambda i, dest: (i, 0)),
            scratch_shapes=[pltpu.VMEM((2, COMBINE_TILE, d), F32), pltpu.VMEM((2, COMBINE_TILE, d), F32),
                            pltpu.SemaphoreType.DMA((2,))]),
        compiler_params=_params("arbitrary"),
        name="moe_combine",
    )(*args)


def _routing_tables(top_i):
    t = top_i.shape[0]
    e_flat = top_i.reshape(-1)
    onehot = (e_flat[:, None] == jnp.arange(N_EXPERTS, dtype=jnp.int32)[None, :]).astype(jnp.int32)
    csum = jnp.cumsum(onehot, axis=0)
    counts = csum[-1]
    pos = jnp.take_along_axis(csum, e_flat[:, None], axis=1)[:, 0] - 1
    padded = ((counts + MOE_TILE - 1) // MOE_TILE) * MOE_TILE
    ends = jnp.cumsum(padded)
    dest = ((ends - padded)[e_flat] + pos).astype(jnp.int32)
    rows = TOP_K * t + N_EXPERTS * MOE_TILE
    row_token = jnp.zeros((rows,), jnp.int32).at[dest].set(jnp.arange(TOP_K * t, dtype=jnp.int32) // TOP_K)
    tile_start = jnp.arange(rows // MOE_TILE, dtype=jnp.int32) * MOE_TILE
    tile_expert = jnp.minimum(jnp.sum(tile_start[:, None] >= ends[None, :], axis=1), N_EXPERTS - 1)
    n_used = (ends[-1:] // MOE_TILE).astype(jnp.int32)
    tile_expert = tile_expert.astype(jnp.int32)
    tiles = jnp.arange(rows // MOE_TILE, dtype=jnp.int32)
    prev = jnp.concatenate([jnp.full((1,), -1, jnp.int32), tile_expert[:-1]])
    first = ((tiles < n_used[0]) & (tile_expert != prev)).astype(jnp.int32)
    run = jnp.maximum(jnp.cumsum(first) - 1, 0).astype(jnp.int32)
    n_runs = jnp.sum(first, keepdims=True).astype(jnp.int32)
    run_expert = jnp.zeros((N_EXPERTS,), jnp.int32).at[
        jnp.where(first == 1, run, N_EXPERTS)].set(tile_expert, mode="drop")
    return dest, row_token, n_used, (tile_expert, n_used, first, run, n_runs, run_expert)


def _moe_swiglu(x2, norm_g, router, w_gate, w_up, w_down, out_norm_g=None):
    h, route = _norm_route(x2, norm_g, router)
    top_i = route[:, 2:4].astype(jnp.int32)
    dest, row_token, n_used, tables = _routing_tables(top_i)
    xs = _gather_rows(h, row_token, n_used)
    hid = _grouped_call(_swiglu_tile, tables, xs, [w_gate, w_up], MOE_UP_BN, BF16, "moe_up")
    y = _grouped_call(_plain_tile, tables, hid, [w_down], MOE_DOWN_BN, F32, "moe_down")
    return _moe_combine(x2, route, y, dest, out_norm_g)


def _split_in_weights(w):
    sizes = (1024, 256, 256, 1024, 1024, 1024, 256, 256, 256, 256, 256, 256, 24, 1024, 1024, 1024)
    offs = np.concatenate([[0], np.cumsum(sizes)])
    (a_q, a_k, a_v, b_val, b_gate, c_q, c_kc, c_vc, c_ks, c_vs, c_kw, c_vw, c_g,
     d_q, d_k, d_v) = [w[:, int(offs[i]):int(offs[i + 1])] for i in range(len(sizes))]
    wa = jnp.concatenate([a_q, a_k, a_v, c_q, c_ks, c_vs, c_kw, c_vw, d_q, d_k, d_v], axis=1)
    pad = jnp.zeros((w.shape[0], PF_COLS - PF_G - c_g.shape[1]), w.dtype)
    wf = jnp.concatenate([b_val, b_gate, c_kc, c_vc, c_g, pad], axis=1)
    return wa.astype(BF16), wf.astype(BF16)


def _mixer_layer(x2, batch, seq, layer, p, wbias, cbias):
    h = _rmsnorm(x2, p["attn_norm_g"][layer], out_dtype=BF16, name="attn_norm")
    wa, wf = _split_in_weights(p["w_in"][layer])
    pa = _matmul(h, wa, bm=1024, bn=512, out_dtype=BF16, name="in_proj_attn")
    pf = _matmul(h, wf, bm=1024, bn=PF_COLS // 3, out_dtype=F32, name="in_proj_f32")

    o_a = _window_attention(pa, batch, seq, PA_AQ, PA_AK, PA_AV, wbias, 0,
                            p["swa_sinks"][layer], "swa_attn")
    o_b = _conformer_conv(pf, batch, seq, p["conv_dw"][layer], p["conv_dw_b"][layer],
                          p["conv_ln_g"][layer], p["conv_ln_b"][layer], p["conv_pw"][layer])
    cmp_kv = _compress(pf, batch, seq, p["nsa_cmp_pe"][layer], p["nsa_cmp_w1"][layer],
                       p["nsa_cmp_w2"][layer])
    o_cmp, sel = _cmp_attention(pa, cmp_kv, cbias, batch, seq)
    o_slc = _slc_attention(pa, sel, wbias, p["rel_bias"][:, GROUP_HEADS:], batch, seq)
    o_win = _window_attention(pa, batch, seq, PA_CQ, PA_CKW, PA_CVW, wbias, GROUP_HEADS, None,
                              "nsa_win_attn")
    o_d = _stick_breaking(pa, batch, seq)
    mixed = _mix(o_a, o_b, o_cmp, o_slc, o_win, pf, o_d, p["mix_norm_g"][layer])
    return _matmul(mixed, p["w_out"][layer].astype(BF16), bm=1024, bn=512, out_dtype=F32,
                   residual=x2, name="out_proj")


def kernel(x, attn_norm_g, ffn_norm_g, final_norm_g, w_in, w_out, mix_norm_g, rel_bias,
           swa_sinks, conv_dw, conv_dw_b, conv_ln_g, conv_ln_b, conv_pw, nsa_cmp_pe,
           nsa_cmp_w1, nsa_cmp_w2, ffn_w_gate, ffn_w_up, ffn_w_down, moe_router,
           moe_w_gate, moe_w_up, moe_w_down):
    batch, seq, _ = x.shape
    t = batch * seq
    p = dict(attn_norm_g=attn_norm_g, w_in=w_in, w_out=w_out, mix_norm_g=mix_norm_g,
             rel_bias=rel_bias, swa_sinks=swa_sinks, conv_dw=conv_dw, conv_dw_b=conv_dw_b,
             conv_ln_g=conv_ln_g, conv_ln_b=conv_ln_b, conv_pw=conv_pw, nsa_cmp_pe=nsa_cmp_pe,
             nsa_cmp_w1=nsa_cmp_w1, nsa_cmp_w2=nsa_cmp_w2)
    wbias = _win_bias(rel_bias)
    cbias = _cmp_bias(rel_bias, seq, GROUP_HEADS)
    x2 = x.reshape(t, D_MODEL)
    for layer in range(DEPTH):
        x2 = _mixer_layer(x2, batch, seq, layer, p, wbias, cbias)
        i = layer // 2
        if layer % 2 == 0:
            hf = _rmsnorm(x2, ffn_norm_g[layer], out_dtype=BF16, name="ffn_norm")
            hid = _swiglu_up(hf, ffn_w_gate[i].astype(BF16), ffn_w_up[i].astype(BF16),
                             bm=1024, bn=256, name="ffn_up")
            x2 = _matmul_ksplit_res(hid, ffn_w_down[i].astype(BF16), x2,
                                    bm=1024, bn=512, bk=D_FF // 2, name="ffn_down")
        else:
            last = layer == DEPTH - 1
            x2 = _moe_swiglu(x2, ffn_norm_g[layer], moe_router[i], moe_w_gate[i], moe_w_up[i],
                             moe_w_down[i], out_norm_g=final_norm_g if last else None)
    if DEPTH % 2 == 1:
        x2 = _rmsnorm(x2, final_norm_g, out_dtype=F32, name="final_norm")
    return x2.reshape(batch, seq, D_MODEL)
```

```python
import functools
import math

import jax
import jax.numpy as jnp
import numpy as np
from jax import lax
from jax.experimental import pallas as pl
from jax.experimental.pallas import tpu as pltpu

D_MODEL = 4096
DEPTH = 2
HEAD_DIM = 128
N_MIXERS = 4
GROUP_WIDTH = D_MODEL // N_MIXERS
GROUP_HEADS = GROUP_WIDTH // HEAD_DIM
KV_HEADS = 2
REP = GROUP_HEADS // KV_HEADS
WINDOW = 128
Q_BLOCK = 128
CONV_WIDTH = 31
CMP_LEN = 32
CMP_STRIDE = 16
SLC_LEN = 64
N_SELECT = 16
N_BRANCH = 3
N_BUCKETS = 32
MAX_DISTANCE = 128
D_FF = 11008
N_EXPERTS = 8
TOP_K = 2
D_EXPERT = D_FF // 2
NORM_EPS = 1e-6
SCALE = HEAD_DIM ** -0.5

VMEM_LIMIT_BYTES = 56 * 1024 * 1024
LANES = 128

F32 = jnp.float32
BF16 = jnp.bfloat16
NEG_INF = float("-inf")
F32_TINY = float(np.finfo(np.float32).tiny)

PA_AQ, PA_AK, PA_AV = 0, 1024, 1280
PA_CQ, PA_CKS, PA_CVS, PA_CKW, PA_CVW = 1536, 2560, 2816, 3072, 3328
PA_DQ, PA_DK, PA_DV = 3584, 4608, 5632
PA_COLS = 6656
PF_VAL, PF_GATE, PF_KC, PF_VC, PF_G = 0, 1024, 2048, 2304, 2560
PF_COLS = 2688


def _params(*sem):
    return pltpu.CompilerParams(dimension_semantics=sem, vmem_limit_bytes=VMEM_LIMIT_BYTES)


def _bucket_thresholds():
    n = np.arange(0, 4 * MAX_DISTANCE)
    max_exact = N_BUCKETS // 2
    nf = np.maximum(n, 1).astype(np.float32)
    large = max_exact + (np.log(nf / max_exact) / math.log(MAX_DISTANCE / max_exact)
                         * (N_BUCKETS - max_exact)).astype(np.int32)
    large = np.minimum(large, N_BUCKETS - 1)
    bucket = np.where(n < max_exact, n, large)
    return [int(np.argmax(bucket >= k)) for k in range(N_BUCKETS)]


BUCKET_THRESHOLDS = _bucket_thresholds()


def _mm_kernel(x_ref, w_ref, o_ref):
    o_ref[...] = jnp.dot(x_ref[...], w_ref[...], preferred_element_type=F32).astype(o_ref.dtype)


def _mm_res_kernel(x_ref, w_ref, r_ref, o_ref):
    acc = jnp.dot(x_ref[...], w_ref[...], preferred_element_type=F32)
    o_ref[...] = (r_ref[...] + acc).astype(o_ref.dtype)


def _mm_res_ksplit_kernel(x_ref, w_ref, r_ref, o_ref):
    @pl.when(pl.program_id(2) == 0)
    def _():
        o_ref[...] = r_ref[...]

    o_ref[...] += jnp.dot(x_ref[...], w_ref[...], preferred_element_type=F32)


def _matmul(x, w, *, bm, bn, out_dtype, residual=None, name="matmul"):
    m, k = x.shape
    _, n = w.shape
    assert m % bm == 0 and n % bn == 0
    in_specs = [pl.BlockSpec((bm, k), lambda i, j: (i, 0)),
                pl.BlockSpec((k, bn), lambda i, j: (0, j))]
    args = [x, w]
    body = _mm_kernel
    if residual is not None:
        in_specs.append(pl.BlockSpec((bm, bn), lambda i, j: (i, j)))
        args.append(residual)
        body = _mm_res_kernel
    return pl.pallas_call(
        body,
        out_shape=jax.ShapeDtypeStruct((m, n), out_dtype),
        grid=(m // bm, n // bn),
        in_specs=in_specs,
        out_specs=pl.BlockSpec((bm, bn), lambda i, j: (i, j)),
        compiler_params=_params("parallel", "arbitrary"),
        name=name,
    )(*args)


def _matmul_ksplit_res(x, w, residual, *, bm, bn, bk, name="matmul_ksplit"):
    m, k = x.shape
    _, n = w.shape
    assert m % bm == 0 and n % bn == 0 and k % bk == 0
    return pl.pallas_call(
        _mm_res_ksplit_kernel,
        out_shape=jax.ShapeDtypeStruct((m, n), F32),
        grid=(m // bm, n // bn, k // bk),
        in_specs=[pl.BlockSpec((bm, bk), lambda i, j, kk: (i, kk)),
                  pl.BlockSpec((bk, bn), lambda i, j, kk: (kk, j)),
                  pl.BlockSpec((bm, bn), lambda i, j, kk: (i, j))],
        out_specs=pl.BlockSpec((bm, bn), lambda i, j, kk: (i, j)),
        compiler_params=_params("parallel", "arbitrary", "arbitrary"),
        name=name,
    )(x, w, residual)


def _swiglu_kernel(x_ref, wg_ref, wu_ref, o_ref):
    x = x_ref[...]
    g = jnp.dot(x, wg_ref[...], preferred_element_type=F32)
    u = jnp.dot(x, wu_ref[...], preferred_element_type=F32)
    o_ref[...] = (g * jax.nn.sigmoid(g) * u).astype(o_ref.dtype)


def _swiglu_up(x, wg, wu, *, bm, bn, name="swiglu_up"):
    m, k = x.shape
    _, n = wg.shape
    assert m % bm == 0 and n % bn == 0
    return pl.pallas_call(
        _swiglu_kernel,
        out_shape=jax.ShapeDtypeStruct((m, n), BF16),
        grid=(m // bm, n // bn),
        in_specs=[pl.BlockSpec((bm, k), lambda i, j: (i, 0)),
                  pl.BlockSpec((k, bn), lambda i, j: (0, j)),
                  pl.BlockSpec((k, bn), lambda i, j: (0, j))],
        out_specs=pl.BlockSpec((bm, bn), lambda i, j: (i, j)),
        compiler_params=_params("parallel", "arbitrary"),
        name=name,
    )(x, wg, wu)


def _rmsnorm_kernel(x_ref, g_ref, o_ref):
    x = x_ref[...]
    y = x * lax.rsqrt(jnp.mean(x * x, axis=-1, keepdims=True) + NORM_EPS)
    o_ref[...] = (y * g_ref[...]).astype(o_ref.dtype)


def _rmsnorm(x, g, *, out_dtype, bm=256, name="rmsnorm"):
    m, d = x.shape
    return pl.pallas_call(
        _rmsnorm_kernel,
        out_shape=jax.ShapeDtypeStruct((m, d), out_dtype),
        grid=(m // bm,),
        in_specs=[pl.BlockSpec((bm, d), lambda i: (i, 0)),
                  pl.BlockSpec((1, d), lambda i: (0, 0))],
        out_specs=pl.BlockSpec((bm, d), lambda i: (i, 0)),
        compiler_params=_params("parallel"),
        name=name,
    )(x, g.reshape(1, d))


def _bias_of_dist(dist, tab_ref, head):
    out = jnp.full(dist.shape, tab_ref[0, head], F32)
    for k in range(1, N_BUCKETS):
        out = jnp.where(dist >= BUCKET_THRESHOLDS[k], tab_ref[k, head], out)
    return out


def _win_bias_kernel(tab_ref, o_ref):
    h = pl.program_id(0)
    qi = lax.broadcasted_iota(jnp.int32, (Q_BLOCK, 2 * Q_BLOCK), 0)
    kj = lax.broadcasted_iota(jnp.int32, (Q_BLOCK, 2 * Q_BLOCK), 1)
    o_ref[0] = _bias_of_dist(qi + Q_BLOCK - kj, tab_ref, h)


def _win_bias(rel_bias):
    nh = rel_bias.shape[1]
    return pl.pallas_call(
        _win_bias_kernel,
        out_shape=jax.ShapeDtypeStruct((nh, Q_BLOCK, 2 * Q_BLOCK), F32),
        grid=(nh,),
        in_specs=[pl.BlockSpec(memory_space=pltpu.SMEM)],
        out_specs=pl.BlockSpec((1, Q_BLOCK, 2 * Q_BLOCK), lambda h: (h, 0, 0)),
        compiler_params=_params("arbitrary"),
        name="win_bias",
    )(rel_bias)


def _cmp_bias_kernel(tab_ref, o_ref, *, head0, rows):
    h = pl.program_id(0) + head0
    n = pl.program_id(1)
    ncmp = o_ref.shape[2]
    t = n * rows + lax.broadcasted_iota(jnp.int32, (rows, ncmp), 0)
    c = lax.broadcasted_iota(jnp.int32, (rows, ncmp), 1)
    o_ref[0] = _bias_of_dist(t - c * CMP_STRIDE - (CMP_LEN - 1), tab_ref, h)


def _cmp_bias(rel_bias, seq, head0, rows=512):
    ncmp = seq // CMP_STRIDE
    rows = min(rows, seq)
    return pl.pallas_call(
        functools.partial(_cmp_bias_kernel, head0=head0, rows=rows),
        out_shape=jax.ShapeDtypeStruct((GROUP_HEADS, seq, ncmp), F32),
        grid=(GROUP_HEADS, seq // rows),
        in_specs=[pl.BlockSpec(memory_space=pltpu.SMEM)],
        out_specs=pl.BlockSpec((1, rows, ncmp), lambda h, n: (h, n, 0)),
        compiler_params=_params("arbitrary", "arbitrary"),
        name="cmp_bias",
    )(rel_bias)


def _stack_heads(q):
    return jnp.concatenate([q[:, r * HEAD_DIM:(r + 1) * HEAD_DIM] for r in range(REP)], axis=0)


def _store_heads(o_ref, o, rows):
    for r in range(REP):
        o_ref[:, r * HEAD_DIM:(r + 1) * HEAD_DIM] = o[r * rows:(r + 1) * rows].astype(o_ref.dtype)


def _qk(q, k):
    return lax.dot_general(q, k, (((1,), (1,)), ((), ())), preferred_element_type=F32)


def _window_kernel(q0_ref, q1_ref, kp_ref, kc_ref, vp_ref, vc_ref, bias_ref, sink_ref, o_ref, *,
                   has_sink):
    n = pl.program_id(1)
    qi = lax.broadcasted_iota(jnp.int32, (Q_BLOCK, 2 * Q_BLOCK), 0)
    kj = lax.broadcasted_iota(jnp.int32, (Q_BLOCK, 2 * Q_BLOCK), 1)
    dist = qi + Q_BLOCK - kj
    mask = ((dist >= 0) & (dist < WINDOW) & ((n > 0) | (kj >= Q_BLOCK)))[None]
    sink_col = (dist == WINDOW)[None]
    for g, q_ref in enumerate((q0_ref, q1_ref)):
        cols = slice(g * HEAD_DIM, (g + 1) * HEAD_DIM)
        heads = slice(g * REP, (g + 1) * REP)
        q4 = _stack_heads(q_ref[...])
        kcat = jnp.concatenate([kp_ref[:, cols], kc_ref[:, cols]], axis=0)
        vcat = jnp.concatenate([vp_ref[:, cols], vc_ref[:, cols]], axis=0)
        s = _qk(q4, kcat).reshape(REP, Q_BLOCK, 2 * Q_BLOCK) * SCALE + bias_ref[heads]
        if has_sink:
            s = jnp.where(sink_col, sink_ref[heads], s)
            live = mask | sink_col
        else:
            live = mask
        s = jnp.where(live, s, NEG_INF)
        m = jnp.max(s, axis=-1, keepdims=True)
        p = jnp.where(live, jnp.exp(s - m), 0.0)
        den = jnp.sum(p, axis=-1, keepdims=True)
        p = p / jnp.maximum(den, F32_TINY)
        if has_sink:
            p = jnp.where(sink_col, 0.0, p)
        o = jnp.dot(p.reshape(REP * Q_BLOCK, 2 * Q_BLOCK).astype(BF16), vcat,
                    preferred_element_type=F32)
        for r in range(REP):
            c0 = (g * REP + r) * HEAD_DIM
            o_ref[:, c0:c0 + HEAD_DIM] = o[r * Q_BLOCK:(r + 1) * Q_BLOCK].astype(o_ref.dtype)


def _window_attention(pa, batch, seq, q_col, k_col, v_col, bias, bias_head0, sinks, name):
    assert KV_HEADS == 2
    nb = seq // Q_BLOCK
    qw = REP * HEAD_DIM
    kvw = KV_HEADS * HEAD_DIM
    has_sink = sinks is not None
    sink_arr = jnp.broadcast_to(
        (sinks if has_sink else jnp.zeros((GROUP_HEADS,), F32)).reshape(GROUP_HEADS, 1, 1),
        (GROUP_HEADS, Q_BLOCK, 1))

    def cur(col):
        return lambda b, n: (b * nb + n, col // kvw)

    def prev(col):
        return lambda b, n: (b * nb + jnp.maximum(n - 1, 0), col // kvw)

    kv_block = (Q_BLOCK, kvw)
    return pl.pallas_call(
        functools.partial(_window_kernel, has_sink=has_sink),
        out_shape=jax.ShapeDtypeStruct((batch * seq, GROUP_WIDTH), F32),
        grid=(batch, nb),
        in_specs=[pl.BlockSpec((Q_BLOCK, qw), lambda b, n: (b * nb + n, q_col // qw)),
                  pl.BlockSpec((Q_BLOCK, qw), lambda b, n: (b * nb + n, q_col // qw + 1)),
                  pl.BlockSpec(kv_block, prev(k_col)), pl.BlockSpec(kv_block, cur(k_col)),
                  pl.BlockSpec(kv_block, prev(v_col)), pl.BlockSpec(kv_block, cur(v_col)),
                  pl.BlockSpec((GROUP_HEADS, Q_BLOCK, 2 * Q_BLOCK),
                               lambda b, n: (bias_head0 // GROUP_HEADS, 0, 0)),
                  pl.BlockSpec((GROUP_HEADS, Q_BLOCK, 1), lambda b, n: (0, 0, 0))],
        out_specs=pl.BlockSpec((Q_BLOCK, GROUP_WIDTH), lambda b, n: (b * nb + n, 0)),
        compiler_params=_params("parallel", "arbitrary"),
        name=name,
    )(pa, pa, pa, pa, pa, pa, bias, sink_arr)


def _compress_kernel(t_ref, pe_ref, w1_ref, w2_ref, o_ref):
    ncmp = o_ref.shape[3]
    half = CMP_LEN // 2
    pe = pe_ref[0]
    rows = [t_ref[pl.ds(r, ncmp, stride=CMP_STRIDE), :] for r in range(CMP_STRIDE)]
    xa = jnp.concatenate([rows[r] + pe[r:r + 1] for r in range(half)], axis=1).astype(BF16)
    xb = jnp.concatenate([rows[r] + pe[half + r:half + r + 1] for r in range(half)], axis=1).astype(BF16)
    kw = half * HEAD_DIM
    p0 = jnp.dot(xa, w1_ref[0, :kw, :], preferred_element_type=F32)
    p1 = jnp.dot(xb, w1_ref[0, kw:, :], preferred_element_type=F32)
    pre = p0 + pltpu.roll(p1, ncmp - 1, 0)
    hid = pre * jax.nn.sigmoid(pre)
    o_ref[0, 0, 0] = jnp.dot(hid.astype(BF16), w2_ref[0], preferred_element_type=F32).astype(o_ref.dtype)


def _compress(pf, batch, seq, pe, w1, w2):
    ncmp = seq // CMP_STRIDE
    return pl.pallas_call(
        _compress_kernel,
        out_shape=jax.ShapeDtypeStruct((2, batch, KV_HEADS, ncmp, HEAD_DIM), BF16),
        grid=(2, batch, KV_HEADS),
        in_specs=[pl.BlockSpec((seq, HEAD_DIM), lambda kv, b, g: (b, PF_KC // HEAD_DIM + KV_HEADS * kv + g)),
                  pl.BlockSpec((1, CMP_LEN, HEAD_DIM), lambda kv, b, g: (kv, 0, 0)),
                  pl.BlockSpec((1, CMP_LEN * HEAD_DIM, HEAD_DIM), lambda kv, b, g: (kv, 0, 0)),
                  pl.BlockSpec((1, HEAD_DIM, HEAD_DIM), lambda kv, b, g: (kv, 0, 0))],
        out_specs=pl.BlockSpec((1, 1, 1, ncmp, HEAD_DIM), lambda kv, b, g: (kv, b, g, 0, 0)),
        compiler_params=_params("arbitrary", "arbitrary", "arbitrary"),
        name="nsa_compress",
    )(pf, pe, w1.astype(BF16), w2.astype(BF16))


def _cmp_attn_kernel(q_ref, kc_ref, vc_ref, bias_ref, ov_ref, o_ref, sel_ref, *, n_sel):
    n = pl.program_id(2)
    ncmp = kc_ref.shape[3]
    nslc = sel_ref.shape[3]
    q4 = _stack_heads(q_ref[...])
    s = _qk(q4, kc_ref[0, 0, 0]).reshape(REP, Q_BLOCK, ncmp) * SCALE + bias_ref[...]
    t = n * Q_BLOCK + lax.broadcasted_iota(jnp.int32, (Q_BLOCK, ncmp), 0)
    c = lax.broadcasted_iota(jnp.int32, (Q_BLOCK, ncmp), 1)
    vis = (t - c * CMP_STRIDE - (CMP_LEN - 1) >= 0)[None]
    s = jnp.where(vis, s, NEG_INF)
    m = jnp.max(s, axis=-1, keepdims=True)
    m = jnp.where(m == NEG_INF, 0.0, m)
    p = jnp.where(vis, jnp.exp(s - m), 0.0)
    den = jnp.sum(p, axis=-1, keepdims=True)
    p = p / jnp.maximum(den, F32_TINY)
    o = jnp.dot(p.reshape(REP * Q_BLOCK, ncmp).astype(BF16), vc_ref[0, 0, 0], preferred_element_type=F32)
    _store_heads(o_ref, o, Q_BLOCK)

    psum = p[0]
    for r in range(1, REP):
        psum = psum + p[r]
    hi = psum.astype(BF16)
    lo = (psum - hi.astype(F32)).astype(BF16)
    ov = ov_ref[...]
    imp = jnp.dot(hi, ov, preferred_element_type=F32) + jnp.dot(lo, ov, preferred_element_type=F32)
    tq = n * Q_BLOCK + lax.broadcasted_iota(jnp.int32, (Q_BLOCK, nslc), 0)
    blk = lax.broadcasted_iota(jnp.int32, (Q_BLOCK, nslc), 1)
    cur = jnp.right_shift(tq, SLC_LEN.bit_length() - 1)
    forced = (blk == 0) | (blk == cur) | (blk == cur - 1)
    score = jnp.where(forced, jnp.inf, jnp.where(blk <= cur, imp, NEG_INF))
    rank = jnp.zeros((Q_BLOCK, nslc), F32)
    for k in range(nslc):
        col = score[:, k:k + 1]
        ahead = (col > score) | ((col == score) & (blk > k))
        rank = rank + jnp.where(ahead, 1.0, 0.0)
    sel_ref[0, 0] = jnp.where(rank < n_sel, 1.0, 0.0)


def _overlap_matrix(ncmp, nslc):
    cs = np.arange(ncmp)[:, None] * CMP_STRIDE
    ss = np.arange(nslc)[None, :] * SLC_LEN
    return ((cs < ss + SLC_LEN) & (cs + CMP_LEN > ss)).astype(np.float32)


def _cmp_attention(pa, cmp_kv, cbias, batch, seq):
    nb = seq // Q_BLOCK
    ncmp = seq // CMP_STRIDE
    nslc = seq // SLC_LEN
    qw = REP * HEAD_DIM
    ov = jnp.asarray(_overlap_matrix(ncmp, nslc), BF16)
    return pl.pallas_call(
        functools.partial(_cmp_attn_kernel, n_sel=min(N_SELECT, nslc)),
        out_shape=(jax.ShapeDtypeStruct((batch * seq, GROUP_WIDTH), F32),
                   jax.ShapeDtypeStruct((batch, KV_HEADS, seq, nslc), F32)),
        grid=(batch, KV_HEADS, nb),
        in_specs=[pl.BlockSpec((Q_BLOCK, qw), lambda b, g, n: (b * nb + n, PA_CQ // qw + g)),
                  pl.BlockSpec((1, 1, 1, ncmp, HEAD_DIM), lambda b, g, n: (0, b, g, 0, 0)),
                  pl.BlockSpec((1, 1, 1, ncmp, HEAD_DIM), lambda b, g, n: (1, b, g, 0, 0)),
                  pl.BlockSpec((REP, Q_BLOCK, ncmp), lambda b, g, n: (g, n, 0)),
                  pl.BlockSpec((ncmp, nslc), lambda b, g, n: (0, 0))],
        out_specs=(pl.BlockSpec((Q_BLOCK, qw), lambda b, g, n: (b * nb + n, g)),
                   pl.BlockSpec((1, 1, Q_BLOCK, nslc), lambda b, g, n: (b, g, n, 0))),
        compiler_params=_params("parallel", "parallel", "arbitrary"),
        name="nsa_cmp_attn",
    )(pa, cmp_kv, cmp_kv, cbias, ov)


FAR_TILE = 8 * Q_BLOCK


def _slc_attn_kernel(q_ref, k_ref, v_ref, sel_ref, bias_ref, far_bias_ref, efar_ref, enear_ref, o_ref):
    n = pl.program_id(2)
    q4 = _stack_heads(q_ref[...])
    selb = sel_ref[0, 0].astype(BF16)
    far_bias = far_bias_ref[...]
    rows = REP * Q_BLOCK

    def online(carry, s, mask, v):
        m, l, acc = carry
        s = jnp.where(mask[None], s, NEG_INF)
        m_new = jnp.maximum(m, jnp.max(s, axis=-1, keepdims=True))
        m_safe = jnp.where(m_new == NEG_INF, 0.0, m_new)
        alpha = jnp.exp(m - m_safe)
        p = jnp.exp(s - m_safe)
        l = alpha * l + jnp.sum(p, axis=-1, keepdims=True)
        pv = jnp.dot(p.reshape(rows, p.shape[-1]).astype(BF16), v, preferred_element_type=F32)
        acc = alpha * acc + pv.reshape(REP, Q_BLOCK, HEAD_DIM)
        return m_new, l, acc

    far_end = (n - 1) * Q_BLOCK

    def far_step(j, carry):
        k0 = pl.multiple_of(j * FAR_TILE, FAR_TILE)
        k = k_ref[pl.ds(k0, FAR_TILE), :]
        v = v_ref[pl.ds(k0, FAR_TILE), :]
        s = _qk(q4, k).reshape(REP, Q_BLOCK, FAR_TILE) * SCALE + far_bias
        picked = jnp.dot(selb, efar_ref[j], preferred_element_type=F32) > 0.5
        kidx = k0 + lax.broadcasted_iota(jnp.int32, (Q_BLOCK, FAR_TILE), 1)
        return online(carry, s, picked & (kidx < far_end), v)

    init = (jnp.full((REP, Q_BLOCK, 1), NEG_INF, F32), jnp.zeros((REP, Q_BLOCK, 1), F32),
            jnp.zeros((REP, Q_BLOCK, HEAD_DIM), F32))
    n_far = (jnp.maximum(far_end, 0) + FAR_TILE - 1) // FAR_TILE
    carry = lax.fori_loop(0, n_far, far_step, init)

    pb = jnp.maximum(n - 1, 0)
    p0 = pl.multiple_of(pb * Q_BLOCK, Q_BLOCK)
    c0 = pl.multiple_of(n * Q_BLOCK, Q_BLOCK)
    kcat = jnp.concatenate([k_ref[pl.ds(p0, Q_BLOCK), :], k_ref[pl.ds(c0, Q_BLOCK), :]], axis=0)
    vcat = jnp.concatenate([v_ref[pl.ds(p0, Q_BLOCK), :], v_ref[pl.ds(c0, Q_BLOCK), :]], axis=0)
    s = _qk(q4, kcat).reshape(REP, Q_BLOCK, 2 * Q_BLOCK) * SCALE + bias_ref[...]
    picked = jnp.concatenate(
        [jnp.dot(selb, enear_ref[pb], preferred_element_type=F32),
         jnp.dot(selb, enear_ref[n], preferred_element_type=F32)], axis=1) > 0.5
    qi = lax.broadcasted_iota(jnp.int32, (Q_BLOCK, 2 * Q_BLOCK), 0)
    kj = lax.broadcasted_iota(jnp.int32, (Q_BLOCK, 2 * Q_BLOCK), 1)
    causal = (qi + Q_BLOCK - kj >= 0) & ((n > 0) | (kj >= Q_BLOCK))
    m, l, acc = online(carry, s, picked & causal, vcat)
    o = acc / jnp.maximum(l, F32_TINY)
    for r in range(REP):
        o_ref[:, r * HEAD_DIM:(r + 1) * HEAD_DIM] = o[r].astype(o_ref.dtype)


def _expand_matrix(seq, tile):
    nslc = seq // SLC_LEN
    key = np.arange(seq).reshape(seq // tile, 1, tile)
    return (key // SLC_LEN == np.arange(nslc)[None, :, None]).astype(np.float32)


def _slc_attention(pa, sel, wbias, rel_bias_c, batch, seq):
    nb = seq // Q_BLOCK
    nslc = seq // SLC_LEN
    qw = REP * HEAD_DIM
    nfar = max(seq // FAR_TILE, 1)
    efar = jnp.asarray(_expand_matrix(max(seq, FAR_TILE), FAR_TILE)[:, :nslc], BF16)
    enear = jnp.asarray(_expand_matrix(seq, Q_BLOCK), BF16)
    far_bias = rel_bias_c[N_BUCKETS - 1].reshape(GROUP_HEADS, 1, 1)
    return pl.pallas_call(
        _slc_attn_kernel,
        out_shape=jax.ShapeDtypeStruct((batch * seq, GROUP_WIDTH), F32),
        grid=(batch, KV_HEADS, nb),
        in_specs=[pl.BlockSpec((Q_BLOCK, qw), lambda b, g, n: (b * nb + n, PA_CQ // qw + g)),
                  pl.BlockSpec((seq, HEAD_DIM), lambda b, g, n: (b, PA_CKS // HEAD_DIM + g)),
                  pl.BlockSpec((seq, HEAD_DIM), lambda b, g, n: (b, PA_CVS // HEAD_DIM + g)),
                  pl.BlockSpec((1, 1, Q_BLOCK, nslc), lambda b, g, n: (b, g, n, 0)),
                  pl.BlockSpec((REP, Q_BLOCK, 2 * Q_BLOCK),
                               lambda b, g, n: (GROUP_HEADS // REP + g, 0, 0)),
                  pl.BlockSpec((REP, 1, 1), lambda b, g, n: (g, 0, 0)),
                  pl.BlockSpec((nfar, nslc, FAR_TILE), lambda b, g, n: (0, 0, 0)),
                  pl.BlockSpec((nb, nslc, Q_BLOCK), lambda b, g, n: (0, 0, 0))],
        out_specs=pl.BlockSpec((Q_BLOCK, qw), lambda b, g, n: (b * nb + n, g)),
        compiler_params=_params("parallel", "parallel", "arbitrary"),
        name="nsa_slc_attn",
    )(pa, pa, pa, sel, wbias, far_bias, efar, enear)


SB_TK = 256
SB_BLOCKS = 4
SB_STEP = SB_BLOCKS * SB_TK
LOG2E = math.log2(math.e)


def _sb_kernel(q_ref, k_ref, v_ref, u_ref, o_ref):
    n = pl.program_id(2)
    u2 = u_ref[...]
    q = q_ref[...]

    def step(j, carry, masked):
        later, acc = carry
        k0 = pl.multiple_of(j * SB_STEP, SB_STEP)
        k = k_ref[pl.ds(k0, SB_STEP), :]
        v = v_ref[pl.ds(k0, SB_STEP), :]
        z = _qk(q, k) * (SCALE * LOG2E)
        zneg = jnp.minimum(z, 0.0)
        zpos_neg = zneg - z
        t = jnp.log2(1.0 + jnp.exp2(zneg + zpos_neg))
        log_keep = zpos_neg - t
        if masked:
            qi = lax.broadcasted_iota(jnp.int32, (SB_STEP, SB_STEP), 0)
            kj = lax.broadcasted_iota(jnp.int32, (SB_STEP, SB_STEP), 1)
            before = kj < qi
            log_keep = jnp.where(before, log_keep, 0.0)
        hi = log_keep.astype(BF16)
        lo = (log_keep - hi.astype(F32)).astype(BF16)
        blocks = [jnp.concatenate([hi[:, b * SB_TK:(b + 1) * SB_TK], lo[:, b * SB_TK:(b + 1) * SB_TK]],
                                  axis=1) for b in range(SB_BLOCKS)]
        suffix = jnp.dot(jnp.concatenate(blocks, axis=0), u2, preferred_element_type=F32)
        parts = [None] * SB_BLOCKS
        for b in range(SB_BLOCKS - 1, -1, -1):
            sfx = suffix[b * SB_STEP:(b + 1) * SB_STEP]
            parts[b] = jnp.exp2(z[:, b * SB_TK:(b + 1) * SB_TK] + sfx + later)
            later = later + sfx[:, 0:1]
        a = jnp.concatenate(parts, axis=1)
        if masked:
            a = jnp.where(before, a, 0.0)
        acc = acc + jnp.dot(a.astype(BF16), v, preferred_element_type=F32)
        return later, acc

    carry = (jnp.zeros((SB_STEP, 1), F32), jnp.zeros((SB_STEP, HEAD_DIM), F32))
    carry = step(n, carry, True)
    carry = lax.fori_loop(0, n, lambda i, c: step(n - 1 - i, c, False), carry)
    o_ref[...] = carry[1].astype(o_ref.dtype)


def _stick_breaking(pa, batch, seq):
    assert seq % SB_STEP == 0
    nb = seq // SB_STEP
    tri = np.tril(np.ones((SB_TK, SB_TK), np.float32))
    u2 = jnp.asarray(np.concatenate([tri, tri], axis=0), BF16)
    return pl.pallas_call(
        _sb_kernel,
        out_shape=jax.ShapeDtypeStruct((batch * seq, GROUP_WIDTH), F32),
        grid=(batch, GROUP_HEADS, nb),
        in_specs=[pl.BlockSpec((SB_STEP, HEAD_DIM), lambda b, h, n: (b * nb + n, PA_DQ // HEAD_DIM + h)),
                  pl.BlockSpec((seq, HEAD_DIM), lambda b, h, n: (b, PA_DK // HEAD_DIM + h)),
                  pl.BlockSpec((seq, HEAD_DIM), lambda b, h, n: (b, PA_DV // HEAD_DIM + h)),
                  pl.BlockSpec((2 * SB_TK, SB_TK), lambda b, h, n: (0, 0))],
        out_specs=pl.BlockSpec((SB_STEP, HEAD_DIM), lambda b, h, n: (b * nb + n, h)),
        compiler_params=_params("parallel", "parallel", "arbitrary"),
        name="stick_breaking",
    )(pa, pa, pa, u2)


CONV_TILE = 256
CONV_HALO = 32


def _conv_kernel(val_ref, gate_ref, hval_ref, hgate_ref, dw_ref, dwb_ref, lng_ref, lnb_ref, pw_ref,
                 o_ref, ext_ref):
    n = pl.program_id(1)
    halo = hval_ref[...] * jax.nn.sigmoid(hgate_ref[...])
    ext_ref[0:CONV_HALO, :] = jnp.where(n > 0, halo, 0.0)
    ext_ref[CONV_HALO:, :] = val_ref[...] * jax.nn.sigmoid(gate_ref[...])
    first = CONV_HALO - (CONV_WIDTH - 1)
    acc = jnp.zeros((CONV_TILE, GROUP_WIDTH), F32) + dwb_ref[...]
    for w in range(CONV_WIDTH):
        acc = acc + ext_ref[first + w:first + w + CONV_TILE, :] * dw_ref[w:w + 1, :]
    mu = jnp.mean(acc, axis=-1, keepdims=True)
    cen = acc - mu
    var = jnp.mean(cen * cen, axis=-1, keepdims=True)
    y = cen * lax.rsqrt(var + NORM_EPS) * lng_ref[...] + lnb_ref[...]
    y = y * jax.nn.sigmoid(y)
    o_ref[...] = jnp.dot(y.astype(BF16), pw_ref[...], preferred_element_type=F32).astype(o_ref.dtype)


def _conformer_conv(pf, batch, seq, dw, dw_b, ln_g, ln_b, pw):
    nt = seq // CONV_TILE
    hpt = CONV_TILE // CONV_HALO
    c = GROUP_WIDTH
    vec = pl.BlockSpec((1, c), lambda b, n: (0, 0))

    def halo(col):
        return lambda b, n: (jnp.maximum((b * nt + n) * hpt - 1, 0), col // c)

    return pl.pallas_call(
        _conv_kernel,
        out_shape=jax.ShapeDtypeStruct((batch * seq, c), F32),
        grid=(batch, nt),
        in_specs=[pl.BlockSpec((CONV_TILE, c), lambda b, n: (b * nt + n, PF_VAL // c)),
                  pl.BlockSpec((CONV_TILE, c), lambda b, n: (b * nt + n, PF_GATE // c)),
                  pl.BlockSpec((CONV_HALO, c), halo(PF_VAL)),
                  pl.BlockSpec((CONV_HALO, c), halo(PF_GATE)),
                  pl.BlockSpec((CONV_WIDTH, c), lambda b, n: (0, 0)),
                  vec, vec, vec,
                  pl.BlockSpec((c, c), lambda b, n: (0, 0))],
        out_specs=pl.BlockSpec((CONV_TILE, c), lambda b, n: (b * nt + n, 0)),
        scratch_shapes=[pltpu.VMEM((CONV_HALO + CONV_TILE, c), F32)],
        compiler_params=_params("parallel", "arbitrary"),
        name="conformer_conv",
    )(pf, pf, pf, pf, dw, dw_b.reshape(1, c), ln_g.reshape(1, c), ln_b.reshape(1, c), pw.astype(BF16))


MIX_TILE = 256


def _mix_kernel(oa_ref, ob_ref, ocmp_ref, oslc_ref, owin_ref, gl_ref, od_ref, g_ref, o_ref):
    gates = jax.nn.sigmoid(gl_ref[...])

    def norm_store(x, grp):
        y = x * lax.rsqrt(jnp.mean(x * x, axis=-1, keepdims=True) + NORM_EPS)
        sl = slice(grp * GROUP_WIDTH, (grp + 1) * GROUP_WIDTH)
        o_ref[:, sl] = (y * g_ref[:, sl]).astype(o_ref.dtype)

    norm_store(oa_ref[...], 0)
    norm_store(ob_ref[...], 1)
    heads = []
    for h in range(GROUP_HEADS):
        sl = slice(h * HEAD_DIM, (h + 1) * HEAD_DIM)
        c0 = h * N_BRANCH
        heads.append(gates[:, c0:c0 + 1] * ocmp_ref[:, sl] + gates[:, c0 + 1:c0 + 2] * oslc_ref[:, sl]
                     + gates[:, c0 + 2:c0 + 3] * owin_ref[:, sl])
    norm_store(jnp.concatenate(heads, axis=1), 2)
    norm_store(od_ref[...], 3)


def _mix(o_a, o_b, o_cmp, o_slc, o_win, pf, o_d, g):
    t = o_a.shape[0]
    grp = pl.BlockSpec((MIX_TILE, GROUP_WIDTH), lambda i: (i, 0))
    return pl.pallas_call(
        _mix_kernel,
        out_shape=jax.ShapeDtypeStruct((t, D_MODEL), BF16),
        grid=(t // MIX_TILE,),
        in_specs=[grp, grp, grp, grp, grp,
                  pl.BlockSpec((MIX_TILE, LANES), lambda i: (i, PF_G // LANES)),
                  grp,
                  pl.BlockSpec((1, D_MODEL), lambda i: (0, 0))],
        out_specs=pl.BlockSpec((MIX_TILE, D_MODEL), lambda i: (i, 0)),
        compiler_params=_params("parallel"),
        name="mix_norm",
    )(o_a, o_b, o_cmp, o_slc, o_win, pf, o_d, g.reshape(1, D_MODEL))


MOE_TILE = 512
MOE_UP_BN = 512
MOE_DOWN_BN = 512
ROUTE_TILE = 256
GATHER_TILE = 256
COMBINE_TILE = 128


def _norm_route_kernel(x_ref, g_ref, r_ref, h_ref, route_ref):
    x = x_ref[...]
    y = x * lax.rsqrt(jnp.mean(x * x, axis=-1, keepdims=True) + NORM_EPS) * g_ref[...]
    h_ref[...] = y
    logits = jnp.dot(y.astype(BF16), r_ref[...], preferred_element_type=F32)
    lane = lax.broadcasted_iota(jnp.int32, logits.shape, 1)
    logits = jnp.where(lane < N_EXPERTS, logits, NEG_INF)
    e = jnp.exp(logits - jnp.max(logits, axis=-1, keepdims=True))
    probs = e / jnp.sum(e, axis=-1, keepdims=True)
    p1 = jnp.max(probs, axis=-1, keepdims=True)
    i1 = jnp.min(jnp.where(probs == p1, lane, LANES), axis=-1, keepdims=True)
    rest = jnp.where(lane == i1, -1.0, probs)
    p2 = jnp.max(rest, axis=-1, keepdims=True)
    i2 = jnp.min(jnp.where(rest == p2, lane, LANES), axis=-1, keepdims=True)
    tot = p1 + p2
    route_ref[...] = jnp.where(lane == 0, p1 / tot,
                               jnp.where(lane == 1, p2 / tot,
                                         jnp.where(lane == 2, i1.astype(F32),
                                                   jnp.where(lane == 3, i2.astype(F32), 0.0))))


def _norm_route(x, g, router):
    m, d = x.shape
    rpad = jnp.zeros((d, LANES), BF16).at[:, :N_EXPERTS].set(router.astype(BF16))
    return pl.pallas_call(
        _norm_route_kernel,
        out_shape=(jax.ShapeDtypeStruct((m, d), F32), jax.ShapeDtypeStruct((m, LANES), F32)),
        grid=(m // ROUTE_TILE,),
        in_specs=[pl.BlockSpec((ROUTE_TILE, d), lambda i: (i, 0)),
                  pl.BlockSpec((1, d), lambda i: (0, 0)),
                  pl.BlockSpec((d, LANES), lambda i: (0, 0))],
        out_specs=(pl.BlockSpec((ROUTE_TILE, d), lambda i: (i, 0)),
                   pl.BlockSpec((ROUTE_TILE, LANES), lambda i: (i, 0))),
        compiler_params=_params("parallel"),
        name="ffn_norm_route",
    )(x, g.reshape(1, d), rpad)


def _row_copy(src_hbm, dst_ref, sem, src_row, dst_row):
    return pltpu.make_async_copy(src_hbm.at[pl.ds(src_row, 1), :], dst_ref.at[pl.ds(dst_row, 1), :], sem)


def _gather_kernel(tok_ref, nu_ref, h_hbm, o_ref, buf_ref, sem):
    i = pl.program_id(0)
    n_live = nu_ref[0] * (MOE_TILE // GATHER_TILE)

    def issue(tile, slot):
        def start(r, c):
            _row_copy(h_hbm, buf_ref.at[slot], sem.at[slot], tok_ref[tile * GATHER_TILE + r], r).start()
            return c
        lax.fori_loop(0, GATHER_TILE, start, 0)

    @pl.when(i == 0)
    def _():
        issue(0, 0)

    @pl.when(i + 1 < n_live)
    def _():
        issue(i + 1, (i + 1) % 2)

    @pl.when(i < n_live)
    def _():
        slot = i % 2

        def wait(r, c):
            _row_copy(h_hbm, buf_ref.at[slot], sem.at[slot], 0, r).wait()
            return c

        lax.fori_loop(0, GATHER_TILE, wait, 0)
        o_ref[...] = buf_ref[slot].astype(o_ref.dtype)

    @pl.when(i >= n_live)
    def _():
        o_ref[...] = jnp.zeros_like(o_ref)


def _gather_rows(h, row_token, n_used):
    rows = row_token.shape[0]
    d = h.shape[1]
    return pl.pallas_call(
        _gather_kernel,
        out_shape=jax.ShapeDtypeStruct((rows, d), BF16),
        grid_spec=pltpu.PrefetchScalarGridSpec(
            num_scalar_prefetch=2,
            grid=(rows // GATHER_TILE,),
            in_specs=[pl.BlockSpec(memory_space=pl.ANY)],
            out_specs=pl.BlockSpec((GATHER_TILE, d), lambda i, tok, nu: (i, 0)),
            scratch_shapes=[pltpu.VMEM((2, GATHER_TILE, d), F32), pltpu.SemaphoreType.DMA((2,))]),
        compiler_params=_params("arbitrary"),
        name="moe_gather",
    )(row_token, n_used, h)


CAST_ROWS = 128


def _swiglu_tile(x, s_ref):
    g = jnp.dot(x, s_ref[0], preferred_element_type=F32)
    u = jnp.dot(x, s_ref[1], preferred_element_type=F32)
    return g * jax.nn.sigmoid(g) * u


def _plain_tile(x, s_ref):
    return jnp.dot(x, s_ref[0], preferred_element_type=F32)


def _grouped_kernel(te_ref, nu_ref, first_ref, run_ref, nruns_ref, rune_ref, x_ref, *rest,
                    n_weights, bn, n_cols, tile_fn):
    w_hbm = rest[:n_weights]
    o_ref, wbuf_ref, s_ref, sem = rest[n_weights:]
    j = pl.program_id(0)
    i = pl.program_id(1)
    nj = pl.num_programs(0)
    k = s_ref.shape[1]
    last_width = n_cols - (pl.cdiv(n_cols, bn) - 1) * bn
    live = i < nu_ref[0]

    def tile_copy(w, expert, jj, slot, width):
        col0 = pl.multiple_of(jj * bn, bn)
        return pltpu.make_async_copy(w_hbm[w].at[expert, :, pl.ds(col0, width)],
                                     wbuf_ref.at[slot, w, :, pl.ds(0, width)], sem.at[slot, w])

    def by_width(jj, fn):
        if last_width == bn:
            fn(bn)
        else:
            pl.when(jj < nj - 1)(lambda: fn(bn))
            pl.when(jj == nj - 1)(lambda: fn(last_width))

    def issue(expert, jj, slot):
        def go(width):
            for w in range(n_weights):
                tile_copy(w, expert, jj, slot, width).start()
        by_width(jj, go)

    def wait_and_cast(expert, jj, slot):
        def go(width):
            for w in range(n_weights):
                tile_copy(w, expert, jj, slot, width).wait()

                def rows(c, carry, w=w):
                    r0 = pl.multiple_of(c * CAST_ROWS, CAST_ROWS)
                    s_ref[w, pl.ds(r0, CAST_ROWS), 0:width] = (
                        wbuf_ref[slot, w, pl.ds(r0, CAST_ROWS), 0:width].astype(BF16))
                    return carry

                lax.fori_loop(0, k // CAST_ROWS, rows, 0)
        by_width(jj, go)

    @pl.when(live & (first_ref[i] == 1))
    def _():
        run = run_ref[i]
        group = j * nruns_ref[0] + run
        slot = group % 2
        expert = te_ref[i]

        @pl.when(group == 0)
        def _():
            issue(expert, j, 0)

        wait_and_cast(expert, j, slot)
        wrap = run + 1 == nruns_ref[0]
        next_run = jnp.where(wrap, 0, run + 1)
        next_j = jnp.where(wrap, j + 1, j)

        @pl.when(next_j < nj)
        def _():
            issue(rune_ref[next_run], next_j, 1 - slot)

    @pl.when(live)
    def _():
        o_ref[...] = tile_fn(x_ref[...], s_ref).astype(o_ref.dtype)

    @pl.when(jnp.logical_not(live))
    def _():
        o_ref[...] = jnp.zeros_like(o_ref)


def _grouped_call(tile_fn, tables, xs, weights, bn, out_dtype, name):
    rows, k = xs.shape
    n = weights[0].shape[2]
    nt = rows // MOE_TILE
    nw = len(weights)
    assert k % CAST_ROWS == 0 and bn % LANES == 0 and n % LANES == 0

    def used(i, nu):
        return jnp.minimum(i, nu[0] - 1)

    body = functools.partial(_grouped_kernel, n_weights=nw, bn=bn, n_cols=n, tile_fn=tile_fn)
    return pl.pallas_call(
        body,
        out_shape=jax.ShapeDtypeStruct((rows, n), out_dtype),
        grid_spec=pltpu.PrefetchScalarGridSpec(
            num_scalar_prefetch=len(tables),
            grid=(pl.cdiv(n, bn), nt),
            in_specs=[pl.BlockSpec((MOE_TILE, k), lambda j, i, te, nu, *_: (used(i, nu), 0))]
            + [pl.BlockSpec(memory_space=pl.ANY) for _ in weights],
            out_specs=pl.BlockSpec((MOE_TILE, bn), lambda j, i, *_: (i, j)),
            scratch_shapes=[pltpu.VMEM((2, nw, k, bn), F32), pltpu.VMEM((nw, k, bn), BF16),
                            pltpu.SemaphoreType.DMA((2, nw))]),
        compiler_params=_params("arbitrary", "arbitrary"),
        name=name,
    )(*tables, xs, *weights)


def _combine_kernel(dest_ref, x_ref, route_ref, y_hbm, *rest, normalize):
    if normalize:
        g_ref, o_ref, y0_ref, y1_ref, sem = rest
    else:
        o_ref, y0_ref, y1_ref, sem = rest
    i = pl.program_id(0)

    def issue(tile, slot):
        def start(r, c):
            tok = tile * COMBINE_TILE + r
            _row_copy(y_hbm, y0_ref.at[slot], sem.at[slot], dest_ref[2 * tok], r).start()
            _row_copy(y_hbm, y1_ref.at[slot], sem.at[slot], dest_ref[2 * tok + 1], r).start()
            return c
        lax.fori_loop(0, COMBINE_TILE, start, 0)

    @pl.when(i == 0)
    def _():
        issue(0, 0)

    @pl.when(i + 1 < pl.num_programs(0))
    def _():
        issue(i + 1, (i + 1) % 2)

    slot = i % 2

    def wait(r, c):
        _row_copy(y_hbm, y0_ref.at[slot], sem.at[slot], 0, r).wait()
        _row_copy(y_hbm, y1_ref.at[slot], sem.at[slot], 0, r).wait()
        return c

    lax.fori_loop(0, COMBINE_TILE, wait, 0)
    route = route_ref[...]
    out = x_ref[...] + route[:, 0:1] * y0_ref[slot] + route[:, 1:2] * y1_ref[slot]
    if normalize:
        out = out * lax.rsqrt(jnp.mean(out * out, axis=-1, keepdims=True) + NORM_EPS) * g_ref[...]
    o_ref[...] = out


def _moe_combine(x2, route, y, dest, out_norm_g=None):
    t, d = x2.shape
    normalize = out_norm_g is not None
    in_specs = [pl.BlockSpec((COMBINE_TILE, d), lambda i, dest: (i, 0)),
                pl.BlockSpec((COMBINE_TILE, LANES), lambda i, dest: (i, 0)),
                pl.BlockSpec(memory_space=pl.ANY)]
    args = [dest, x2, route, y]
    if normalize:
        in_specs.append(pl.BlockSpec((1, d), lambda i, dest: (0, 0)))
        args.append(out_norm_g.reshape(1, d))
    return pl.pallas_call(
        functools.partial(_combine_kernel, normalize=normalize),
        out_shape=jax.ShapeDtypeStruct((t, d), F32),
        grid_spec=pltpu.PrefetchScalarGridSpec(
            num_scalar_prefetch=1,
            grid=(t // COMBINE_TILE,),
            in_specs=in_specs,
            out_specs=pl.BlockSpec((COMBINE_TILE, d), lambda i, dest: (i, 0)),
            scratch_shapes=[pltpu.VMEM((2, COMBINE_TILE, d), F32), pltpu.VMEM((2, COMBINE_TILE, d), F32),
                            pltpu.SemaphoreType.DMA((2,))]),
        compiler_params=_params("arbitrary"),
        name="moe_combine",
    )(*args)


def _routing_tables(top_i):
    t = top_i.shape[0]
    e_flat = top_i.reshape(-1)
    onehot = (e_flat[:, None] == jnp.arange(N_EXPERTS, dtype=jnp.int32)[None, :]).astype(jnp.int32)
    csum = jnp.cumsum(onehot, axis=0)
    counts = csum[-1]
    pos = jnp.take_along_axis(csum, e_flat[:, None], axis=1)[:, 0] - 1
    padded = ((counts + MOE_TILE - 1) // MOE_TILE) * MOE_TILE
    ends = jnp.cumsum(padded)
    dest = ((ends - padded)[e_flat] + pos).astype(jnp.int32)
    rows = TOP_K * t + N_EXPERTS * MOE_TILE
    row_token = jnp.zeros((rows,), jnp.int32).at[dest].set(jnp.arange(TOP_K * t, dtype=jnp.int32) // TOP_K)
    tile_start = jnp.arange(rows // MOE_TILE, dtype=jnp.int32) * MOE_TILE
    tile_expert = jnp.minimum(jnp.sum(tile_start[:, None] >= ends[None, :], axis=1), N_EXPERTS - 1)
    n_used = (ends[-1:] // MOE_TILE).astype(jnp.int32)
    tile_expert = tile_expert.astype(jnp.int32)
    tiles = jnp.arange(rows // MOE_TILE, dtype=jnp.int32)
    prev = jnp.concatenate([jnp.full((1,), -1, jnp.int32), tile_expert[:-1]])
    first = ((tiles < n_used[0]) & (tile_expert != prev)).astype(jnp.int32)
    run = jnp.maximum(jnp.cumsum(first) - 1, 0).astype(jnp.int32)
    n_runs = jnp.sum(first, keepdims=True).astype(jnp.int32)
    run_expert = jnp.zeros((N_EXPERTS,), jnp.int32).at[
        jnp.where(first == 1, run, N_EXPERTS)].set(tile_expert, mode="drop")
    return dest, row_token, n_used, (tile_expert, n_used, first, run, n_runs, run_expert)


def _moe_swiglu(x2, norm_g, router, w_gate, w_up, w_down, out_norm_g=None):
    h, route = _norm_route(x2, norm_g, router)
    top_i = route[:, 2:4].astype(jnp.int32)
    dest, row_token, n_used, tables = _routing_tables(top_i)
    xs = _gather_rows(h, row_token, n_used)
    hid = _grouped_call(_swiglu_tile, tables, xs, [w_gate, w_up], MOE_UP_BN, BF16, "moe_up")
    y = _grouped_call(_plain_tile, tables, hid, [w_down], MOE_DOWN_BN, F32, "moe_down")
    return _moe_combine(x2, route, y, dest, out_norm_g)


def _split_in_weights(w):
    sizes = (1024, 256, 256, 1024, 1024, 1024, 256, 256, 256, 256, 256, 256, 24, 1024, 1024, 1024)
    offs = np.concatenate([[0], np.cumsum(sizes)])
    w = w.astype(BF16)
    (a_q, a_k, a_v, b_val, b_gate, c_q, c_kc, c_vc, c_ks, c_vs, c_kw, c_vw, c_g,
     d_q, d_k, d_v) = [w[:, int(offs[i]):int(offs[i + 1])] for i in range(len(sizes))]
    wa = jnp.concatenate([a_q, a_k, a_v, c_q, c_ks, c_vs, c_kw, c_vw, d_q, d_k, d_v], axis=1)
    pad = jnp.zeros((w.shape[0], PF_COLS - PF_G - c_g.shape[1]), w.dtype)
    wf = jnp.concatenate([b_val, b_gate, c_kc, c_vc, c_g, pad], axis=1)
    return wa, wf


def _mixer_layer(x2, batch, seq, layer, p, wbias, cbias):
    h = _rmsnorm(x2, p["attn_norm_g"][layer], out_dtype=BF16, name="attn_norm")
    wa, wf = _split_in_weights(p["w_in"][layer])
    pa = _matmul(h, wa, bm=1024, bn=512, out_dtype=BF16, name="in_proj_attn")
    pf = _matmul(h, wf, bm=1024, bn=PF_COLS // 3, out_dtype=F32, name="in_proj_f32")

    o_a = _window_attention(pa, batch, seq, PA_AQ, PA_AK, PA_AV, wbias, 0,
                            p["swa_sinks"][layer], "swa_attn")
    o_b = _conformer_conv(pf, batch, seq, p["conv_dw"][layer], p["conv_dw_b"][layer],
                          p["conv_ln_g"][layer], p["conv_ln_b"][layer], p["conv_pw"][layer])
    cmp_kv = _compress(pf, batch, seq, p["nsa_cmp_pe"][layer], p["nsa_cmp_w1"][layer],
                       p["nsa_cmp_w2"][layer])
    o_cmp, sel = _cmp_attention(pa, cmp_kv, cbias, batch, seq)
    o_slc = _slc_attention(pa, sel, wbias, p["rel_bias"][:, GROUP_HEADS:], batch, seq)
    o_win = _window_attention(pa, batch, seq, PA_CQ, PA_CKW, PA_CVW, wbias, GROUP_HEADS, None,
                              "nsa_win_attn")
    o_d = _stick_breaking(pa, batch, seq)
    mixed = _mix(o_a, o_b, o_cmp, o_slc, o_win, pf, o_d, p["mix_norm_g"][layer])
    return _matmul(mixed, p["w_out"][layer].astype(BF16), bm=1024, bn=512, out_dtype=F32,
                   residual=x2, name="out_proj")


def kernel(x, attn_norm_g, ffn_norm_g, final_norm_g, w_in, w_out, mix_norm_g, rel_bias,
           swa_sinks, conv_dw, conv_dw_b, conv_ln_g, conv_ln_b, conv_pw, nsa_cmp_pe,
           nsa_cmp_w1, nsa_cmp_w2, ffn_w_gate, ffn_w_up, ffn_w_down, moe_router,
           moe_w_gate, moe_w_up, moe_w_down):
    batch, seq, _ = x.shape
    t = batch * seq
    p = dict(attn_norm_g=attn_norm_g, w_in=w_in, w_out=w_out, mix_norm_g=mix_norm_g,
             rel_bias=rel_bias, swa_sinks=swa_sinks, conv_dw=conv_dw, conv_dw_b=conv_dw_b,
             conv_ln_g=conv_ln_g, conv_ln_b=conv_ln_b, conv_pw=conv_pw, nsa_cmp_pe=nsa_cmp_pe,
             nsa_cmp_w1=nsa_cmp_w1, nsa_cmp_w2=nsa_cmp_w2)
    wbias = _win_bias(rel_bias)
    cbias = _cmp_bias(rel_bias, seq, GROUP_HEADS)
    x2 = x.reshape(t, D_MODEL)
    for layer in range(DEPTH):
        x2 = _mixer_layer(x2, batch, seq, layer, p, wbias, cbias)
        i = layer // 2
        if layer % 2 == 0:
            hf = _rmsnorm(x2, ffn_norm_g[layer], out_dtype=BF16, name="ffn_norm")
            hid = _swiglu_up(hf, ffn_w_gate[i].astype(BF16), ffn_w_up[i].astype(BF16),
                             bm=1024, bn=256, name="ffn_up")
            x2 = _matmul_ksplit_res(hid, ffn_w_down[i].astype(BF16), x2,
                                    bm=1024, bn=512, bk=D_FF // 2, name="ffn_down")
        else:
            last = layer == DEPTH - 1
            x2 = _moe_swiglu(x2, ffn_norm_g[layer], moe_router[i], moe_w_gate[i], moe_w_up[i],
                             moe_w_down[i], out_norm_g=final_norm_g if last else None)
    if DEPTH % 2 == 1:
        x2 = _rmsnorm(x2, final_norm_g, out_dtype=F32, name="final_norm")
    return x2.reshape(batch, seq, D_MODEL)
```

```python
import functools
import math

import jax
import jax.numpy as jnp
import numpy as np
from jax import lax
from jax.experimental import pallas as pl
from jax.experimental.pallas import tpu as pltpu

D_MODEL = 4096
DEPTH = 2
HEAD_DIM = 128
N_MIXERS = 4
GROUP_WIDTH = D_MODEL // N_MIXERS
GROUP_HEADS = GROUP_WIDTH // HEAD_DIM
KV_HEADS = 2
REP = GROUP_HEADS // KV_HEADS
WINDOW = 128
Q_BLOCK = 128
CONV_WIDTH = 31
CMP_LEN = 32
CMP_STRIDE = 16
SLC_LEN = 64
N_SELECT = 16
N_BRANCH = 3
N_BUCKETS = 32
MAX_DISTANCE = 128
D_FF = 11008
N_EXPERTS = 8
TOP_K = 2
D_EXPERT = D_FF // 2
NORM_EPS = 1e-6
SCALE = HEAD_DIM ** -0.5

VMEM_LIMIT_BYTES = 56 * 1024 * 1024
LANES = 128

F32 = jnp.float32
BF16 = jnp.bfloat16
NEG_INF = float("-inf")
F32_TINY = float(np.finfo(np.float32).tiny)

PA_AQ, PA_AK, PA_AV = 0, 1024, 1280
PA_CQ, PA_CKS, PA_CVS, PA_CKW, PA_CVW = 1536, 2560, 2816, 3072, 3328
PA_DQ, PA_DK, PA_DV = 3584, 4608, 5632
PA_COLS = 6656
PF_VAL, PF_GATE, PF_KC, PF_VC, PF_G = 0, 1024, 2048, 2304, 2560
PF_COLS = 2688


def _params(*sem):
    return pltpu.CompilerParams(dimension_semantics=sem, vmem_limit_bytes=VMEM_LIMIT_BYTES)


def _bucket_thresholds():
    n = np.arange(0, 4 * MAX_DISTANCE)
    max_exact = N_BUCKETS // 2
    nf = np.maximum(n, 1).astype(np.float32)
    large = max_exact + (np.log(nf / max_exact) / math.log(MAX_DISTANCE / max_exact)
                         * (N_BUCKETS - max_exact)).astype(np.int32)
    large = np.minimum(large, N_BUCKETS - 1)
    bucket = np.where(n < max_exact, n, large)
    return [int(np.argmax(bucket >= k)) for k in range(N_BUCKETS)]


BUCKET_THRESHOLDS = _bucket_thresholds()


def _mm_kernel(x_ref, w_ref, o_ref):
    o_ref[...] = jnp.dot(x_ref[...], w_ref[...], preferred_element_type=F32).astype(o_ref.dtype)


def _mm_res_kernel(x_ref, w_ref, r_ref, o_ref):
    acc = jnp.dot(x_ref[...], w_ref[...], preferred_element_type=F32)
    o_ref[...] = (r_ref[...] + acc).astype(o_ref.dtype)


def _mm_res_ksplit_kernel(x_ref, w_ref, r_ref, o_ref):
    @pl.when(pl.program_id(2) == 0)
    def _():
        o_ref[...] = r_ref[...]

    o_ref[...] += jnp.dot(x_ref[...], w_ref[...], preferred_element_type=F32)


def _matmul(x, w, *, bm, bn, out_dtype, residual=None, name="matmul"):
    m, k = x.shape
    _, n = w.shape
    assert m % bm == 0 and n % bn == 0
    in_specs = [pl.BlockSpec((bm, k), lambda i, j: (i, 0)),
                pl.BlockSpec((k, bn), lambda i, j: (0, j))]
    args = [x, w]
    body = _mm_kernel
    if residual is not None:
        in_specs.append(pl.BlockSpec((bm, bn), lambda i, j: (i, j)))
        args.append(residual)
        body = _mm_res_kernel
    return pl.pallas_call(
        body,
        out_shape=jax.ShapeDtypeStruct((m, n), out_dtype),
        grid=(m // bm, n // bn),
        in_specs=in_specs,
        out_specs=pl.BlockSpec((bm, bn), lambda i, j: (i, j)),
        compiler_params=_params("parallel", "arbitrary"),
        name=name,
    )(*args)


def _matmul_ksplit_res(x, w, residual, *, bm, bn, bk, name="matmul_ksplit"):
    m, k = x.shape
    _, n = w.shape
    assert m % bm == 0 and n % bn == 0 and k % bk == 0
    return pl.pallas_call(
        _mm_res_ksplit_kernel,
        out_shape=jax.ShapeDtypeStruct((m, n), F32),
        grid=(m // bm, n // bn, k // bk),
        in_specs=[pl.BlockSpec((bm, bk), lambda i, j, kk: (i, kk)),
                  pl.BlockSpec((bk, bn), lambda i, j, kk: (kk, j)),
                  pl.BlockSpec((bm, bn), lambda i, j, kk: (i, j))],
        out_specs=pl.BlockSpec((bm, bn), lambda i, j, kk: (i, j)),
        compiler_params=_params("parallel", "arbitrary", "arbitrary"),
        name=name,
    )(x, w, residual)


def _swiglu_kernel(x_ref, wg_ref, wu_ref, o_ref):
    x = x_ref[...]
    g = jnp.dot(x, wg_ref[...], preferred_element_type=F32)
    u = jnp.dot(x, wu_ref[...], preferred_element_type=F32)
    o_ref[...] = (g * jax.nn.sigmoid(g) * u).astype(o_ref.dtype)


def _swiglu_up(x, wg, wu, *, bm, bn, name="swiglu_up"):
    m, k = x.shape
    _, n = wg.shape
    assert m % bm == 0 and n % bn == 0
    return pl.pallas_call(
        _swiglu_kernel,
        out_shape=jax.ShapeDtypeStruct((m, n), BF16),
        grid=(m // bm, n // bn),
        in_specs=[pl.BlockSpec((bm, k), lambda i, j: (i, 0)),
                  pl.BlockSpec((k, bn), lambda i, j: (0, j)),
                  pl.BlockSpec((k, bn), lambda i, j: (0, j))],
        out_specs=pl.BlockSpec((bm, bn), lambda i, j: (i, j)),
        compiler_params=_params("parallel", "arbitrary"),
        name=name,
    )(x, wg, wu)


def _rmsnorm_kernel(x_ref, g_ref, o_ref):
    x = x_ref[...]
    y = x * lax.rsqrt(jnp.mean(x * x, axis=-1, keepdims=True) + NORM_EPS)
    o_ref[...] = (y * g_ref[...]).astype(o_ref.dtype)


def _rmsnorm(x, g, *, out_dtype, bm=256, name="rmsnorm"):
    m, d = x.shape
    return pl.pallas_call(
        _rmsnorm_kernel,
        out_shape=jax.ShapeDtypeStruct((m, d), out_dtype),
        grid=(m // bm,),
        in_specs=[pl.BlockSpec((bm, d), lambda i: (i, 0)),
                  pl.BlockSpec((1, d), lambda i: (0, 0))],
        out_specs=pl.BlockSpec((bm, d), lambda i: (i, 0)),
        compiler_params=_params("parallel"),
        name=name,
    )(x, g.reshape(1, d))


def _bias_of_dist(dist, tab_ref, head):
    out = jnp.full(dist.shape, tab_ref[0, head], F32)
    for k in range(1, N_BUCKETS):
        out = jnp.where(dist >= BUCKET_THRESHOLDS[k], tab_ref[k, head], out)
    return out


def _win_bias_kernel(tab_ref, o_ref):
    h = pl.program_id(0)
    qi = lax.broadcasted_iota(jnp.int32, (Q_BLOCK, 2 * Q_BLOCK), 0)
    kj = lax.broadcasted_iota(jnp.int32, (Q_BLOCK, 2 * Q_BLOCK), 1)
    o_ref[0] = _bias_of_dist(qi + Q_BLOCK - kj, tab_ref, h)


def _win_bias(rel_bias):
    nh = rel_bias.shape[1]
    return pl.pallas_call(
        _win_bias_kernel,
        out_shape=jax.ShapeDtypeStruct((nh, Q_BLOCK, 2 * Q_BLOCK), F32),
        grid=(nh,),
        in_specs=[pl.BlockSpec(memory_space=pltpu.SMEM)],
        out_specs=pl.BlockSpec((1, Q_BLOCK, 2 * Q_BLOCK), lambda h: (h, 0, 0)),
        compiler_params=_params("arbitrary"),
        name="win_bias",
    )(rel_bias)


def _cmp_bias_kernel(tab_ref, o_ref, *, head0, rows):
    h = pl.program_id(0) + head0
    n = pl.program_id(1)
    ncmp = o_ref.shape[2]
    t = n * rows + lax.broadcasted_iota(jnp.int32, (rows, ncmp), 0)
    c = lax.broadcasted_iota(jnp.int32, (rows, ncmp), 1)
    o_ref[0] = _bias_of_dist(t - c * CMP_STRIDE - (CMP_LEN - 1), tab_ref, h)


def _cmp_bias(rel_bias, seq, head0, rows=512):
    ncmp = seq // CMP_STRIDE
    rows = min(rows, seq)
    return pl.pallas_call(
        functools.partial(_cmp_bias_kernel, head0=head0, rows=rows),
        out_shape=jax.ShapeDtypeStruct((GROUP_HEADS, seq, ncmp), F32),
        grid=(GROUP_HEADS, seq // rows),
        in_specs=[pl.BlockSpec(memory_space=pltpu.SMEM)],
        out_specs=pl.BlockSpec((1, rows, ncmp), lambda h, n: (h, n, 0)),
        compiler_params=_params("arbitrary", "arbitrary"),
        name="cmp_bias",
    )(rel_bias)


def _stack_heads(q):
    return jnp.concatenate([q[:, r * HEAD_DIM:(r + 1) * HEAD_DIM] for r in range(REP)], axis=0)


def _store_heads(o_ref, o, rows):
    for r in range(REP):
        o_ref[:, r * HEAD_DIM:(r + 1) * HEAD_DIM] = o[r * rows:(r + 1) * rows].astype(o_ref.dtype)


def _qk(q, k):
    return lax.dot_general(q, k, (((1,), (1,)), ((), ())), preferred_element_type=F32)


def _window_kernel(q0_ref, q1_ref, kp_ref, kc_ref, vp_ref, vc_ref, bias_ref, sink_ref, o_ref, *,
                   has_sink):
    n = pl.program_id(1)
    qi = lax.broadcasted_iota(jnp.int32, (Q_BLOCK, 2 * Q_BLOCK), 0)
    kj = lax.broadcasted_iota(jnp.int32, (Q_BLOCK, 2 * Q_BLOCK), 1)
    dist = qi + Q_BLOCK - kj
    mask = ((dist >= 0) & (dist < WINDOW) & ((n > 0) | (kj >= Q_BLOCK)))[None]
    sink_col = (dist == WINDOW)[None]
    for g, q_ref in enumerate((q0_ref, q1_ref)):
        cols = slice(g * HEAD_DIM, (g + 1) * HEAD_DIM)
        heads = slice(g * REP, (g + 1) * REP)
        q4 = _stack_heads(q_ref[...])
        kcat = jnp.concatenate([kp_ref[:, cols], kc_ref[:, cols]], axis=0)
        vcat = jnp.concatenate([vp_ref[:, cols], vc_ref[:, cols]], axis=0)
        s = _qk(q4, kcat).reshape(REP, Q_BLOCK, 2 * Q_BLOCK) * SCALE + bias_ref[heads]
        if has_sink:
            s = jnp.where(sink_col, sink_ref[heads], s)
            live = mask | sink_col
        else:
            live = mask
        s = jnp.where(live, s, NEG_INF)
        m = jnp.max(s, axis=-1, keepdims=True)
        p = jnp.where(live, jnp.exp(s - m), 0.0)
        den = jnp.sum(p, axis=-1, keepdims=True)
        p = p / jnp.maximum(den, F32_TINY)
        if has_sink:
            p = jnp.where(sink_col, 0.0, p)
        o = jnp.dot(p.reshape(REP * Q_BLOCK, 2 * Q_BLOCK).astype(BF16), vcat,
                    preferred_element_type=F32)
        for r in range(REP):
            c0 = (g * REP + r) * HEAD_DIM
            o_ref[:, c0:c0 + HEAD_DIM] = o[r * Q_BLOCK:(r + 1) * Q_BLOCK].astype(o_ref.dtype)


def _window_attention(pa, batch, seq, q_col, k_col, v_col, bias, bias_head0, sinks, name):
    assert KV_HEADS == 2
    nb = seq // Q_BLOCK
    qw = REP * HEAD_DIM
    kvw = KV_HEADS * HEAD_DIM
    has_sink = sinks is not None
    sink_arr = jnp.broadcast_to(
        (sinks if has_sink else jnp.zeros((GROUP_HEADS,), F32)).reshape(GROUP_HEADS, 1, 1),
        (GROUP_HEADS, Q_BLOCK, 1))

    def cur(col):
        return lambda b, n: (b * nb + n, col // kvw)

    def prev(col):
        return lambda b, n: (b * nb + jnp.maximum(n - 1, 0), col // kvw)

    kv_block = (Q_BLOCK, kvw)
    return pl.pallas_call(
        functools.partial(_window_kernel, has_sink=has_sink),
        out_shape=jax.ShapeDtypeStruct((batch * seq, GROUP_WIDTH), F32),
        grid=(batch, nb),
        in_specs=[pl.BlockSpec((Q_BLOCK, qw), lambda b, n: (b * nb + n, q_col // qw)),
                  pl.BlockSpec((Q_BLOCK, qw), lambda b, n: (b * nb + n, q_col // qw + 1)),
                  pl.BlockSpec(kv_block, prev(k_col)), pl.BlockSpec(kv_block, cur(k_col)),
                  pl.BlockSpec(kv_block, prev(v_col)), pl.BlockSpec(kv_block, cur(v_col)),
                  pl.BlockSpec((GROUP_HEADS, Q_BLOCK, 2 * Q_BLOCK),
                               lambda b, n: (bias_head0 // GROUP_HEADS, 0, 0)),
                  pl.BlockSpec((GROUP_HEADS, Q_BLOCK, 1), lambda b, n: (0, 0, 0))],
        out_specs=pl.BlockSpec((Q_BLOCK, GROUP_WIDTH), lambda b, n: (b * nb + n, 0)),
        compiler_params=_params("parallel", "arbitrary"),
        name=name,
    )(pa, pa, pa, pa, pa, pa, bias, sink_arr)


def _compress_kernel(t_ref, pe_ref, w1_ref, w2_ref, o_ref):
    ncmp = o_ref.shape[3]
    half = CMP_LEN // 2
    pe = pe_ref[0]
    rows = [t_ref[pl.ds(r, ncmp, stride=CMP_STRIDE), :] for r in range(CMP_STRIDE)]
    xa = jnp.concatenate([rows[r] + pe[r:r + 1] for r in range(half)], axis=1).astype(BF16)
    xb = jnp.concatenate([rows[r] + pe[half + r:half + r + 1] for r in range(half)], axis=1).astype(BF16)
    kw = half * HEAD_DIM
    p0 = jnp.dot(xa, w1_ref[0, :kw, :], preferred_element_type=F32)
    p1 = jnp.dot(xb, w1_ref[0, kw:, :], preferred_element_type=F32)
    pre = p0 + pltpu.roll(p1, ncmp - 1, 0)
    hid = pre * jax.nn.sigmoid(pre)
    o_ref[0, 0, 0] = jnp.dot(hid.astype(BF16), w2_ref[0], preferred_element_type=F32).astype(o_ref.dtype)


def _compress(pf, batch, seq, pe, w1, w2):
    ncmp = seq // CMP_STRIDE
    return pl.pallas_call(
        _compress_kernel,
        out_shape=jax.ShapeDtypeStruct((2, batch, KV_HEADS, ncmp, HEAD_DIM), BF16),
        grid=(2, batch, KV_HEADS),
        in_specs=[pl.BlockSpec((seq, HEAD_DIM), lambda kv, b, g: (b, PF_KC // HEAD_DIM + KV_HEADS * kv + g)),
                  pl.BlockSpec((1, CMP_LEN, HEAD_DIM), lambda kv, b, g: (kv, 0, 0)),
                  pl.BlockSpec((1, CMP_LEN * HEAD_DIM, HEAD_DIM), lambda kv, b, g: (kv, 0, 0)),
                  pl.BlockSpec((1, HEAD_DIM, HEAD_DIM), lambda kv, b, g: (kv, 0, 0))],
        out_specs=pl.BlockSpec((1, 1, 1, ncmp, HEAD_DIM), lambda kv, b, g: (kv, b, g, 0, 0)),
        compiler_params=_params("arbitrary", "arbitrary", "arbitrary"),
        name="nsa_compress",
    )(pf, pe, w1.astype(BF16), w2.astype(BF16))


def _cmp_attn_kernel(q_ref, kc_ref, vc_ref, bias_ref, ov_ref, o_ref, sel_ref, *, n_sel):
    n = pl.program_id(2)
    ncmp = kc_ref.shape[3]
    nslc = sel_ref.shape[3]
    q4 = _stack_heads(q_ref[...])
    s = _qk(q4, kc_ref[0, 0, 0]).reshape(REP, Q_BLOCK, ncmp) * SCALE + bias_ref[...]
    t = n * Q_BLOCK + lax.broadcasted_iota(jnp.int32, (Q_BLOCK, ncmp), 0)
    c = lax.broadcasted_iota(jnp.int32, (Q_BLOCK, ncmp), 1)
    vis = (t - c * CMP_STRIDE - (CMP_LEN - 1) >= 0)[None]
    s = jnp.where(vis, s, NEG_INF)
    m = jnp.max(s, axis=-1, keepdims=True)
    m = jnp.where(m == NEG_INF, 0.0, m)
    p = jnp.where(vis, jnp.exp(s - m), 0.0)
    den = jnp.sum(p, axis=-1, keepdims=True)
    p = p / jnp.maximum(den, F32_TINY)
    o = jnp.dot(p.reshape(REP * Q_BLOCK, ncmp).astype(BF16), vc_ref[0, 0, 0], preferred_element_type=F32)
    _store_heads(o_ref, o, Q_BLOCK)

    psum = p[0]
    for r in range(1, REP):
        psum = psum + p[r]
    hi = psum.astype(BF16)
    lo = (psum - hi.astype(F32)).astype(BF16)
    ov = ov_ref[...]
    imp = jnp.dot(hi, ov, preferred_element_type=F32) + jnp.dot(lo, ov, preferred_element_type=F32)
    tq = n * Q_BLOCK + lax.broadcasted_iota(jnp.int32, (Q_BLOCK, nslc), 0)
    blk = lax.broadcasted_iota(jnp.int32, (Q_BLOCK, nslc), 1)
    cur = jnp.right_shift(tq, SLC_LEN.bit_length() - 1)
    forced = (blk == 0) | (blk == cur) | (blk == cur - 1)
    score = jnp.where(forced, jnp.inf, jnp.where(blk <= cur, imp, NEG_INF))
    rank = jnp.zeros((Q_BLOCK, nslc), F32)
    for k in range(nslc):
        col = score[:, k:k + 1]
        ahead = (col > score) | ((col == score) & (blk > k))
        rank = rank + jnp.where(ahead, 1.0, 0.0)
    sel_ref[0, 0] = jnp.where(rank < n_sel, 1.0, 0.0)


def _overlap_matrix(ncmp, nslc):
    cs = np.arange(ncmp)[:, None] * CMP_STRIDE
    ss = np.arange(nslc)[None, :] * SLC_LEN
    return ((cs < ss + SLC_LEN) & (cs + CMP_LEN > ss)).astype(np.float32)


def _cmp_attention(pa, cmp_kv, cbias, batch, seq):
    nb = seq // Q_BLOCK
    ncmp = seq // CMP_STRIDE
    nslc = seq // SLC_LEN
    qw = REP * HEAD_DIM
    ov = jnp.asarray(_overlap_matrix(ncmp, nslc), BF16)
    return pl.pallas_call(
        functools.partial(_cmp_attn_kernel, n_sel=min(N_SELECT, nslc)),
        out_shape=(jax.ShapeDtypeStruct((batch * seq, GROUP_WIDTH), F32),
                   jax.ShapeDtypeStruct((batch, KV_HEADS, seq, nslc), F32)),
        grid=(batch, KV_HEADS, nb),
        in_specs=[pl.BlockSpec((Q_BLOCK, qw), lambda b, g, n: (b * nb + n, PA_CQ // qw + g)),
                  pl.BlockSpec((1, 1, 1, ncmp, HEAD_DIM), lambda b, g, n: (0, b, g, 0, 0)),
                  pl.BlockSpec((1, 1, 1, ncmp, HEAD_DIM), lambda b, g, n: (1, b, g, 0, 0)),
                  pl.BlockSpec((REP, Q_BLOCK, ncmp), lambda b, g, n: (g, n, 0)),
                  pl.BlockSpec((ncmp, nslc), lambda b, g, n: (0, 0))],
        out_specs=(pl.BlockSpec((Q_BLOCK, qw), lambda b, g, n: (b * nb + n, g)),
                   pl.BlockSpec((1, 1, Q_BLOCK, nslc), lambda b, g, n: (b, g, n, 0))),
        compiler_params=_params("parallel", "parallel", "arbitrary"),
        name="nsa_cmp_attn",
    )(pa, cmp_kv, cmp_kv, cbias, ov)


FAR_TILE = 8 * Q_BLOCK


def _slc_attn_kernel(q_ref, k_ref, v_ref, sel_ref, bias_ref, far_bias_ref, efar_ref, enear_ref, o_ref):
    n = pl.program_id(2)
    q4 = _stack_heads(q_ref[...])
    selb = sel_ref[0, 0].astype(BF16)
    far_bias = far_bias_ref[...]
    rows = REP * Q_BLOCK

    def online(carry, s, mask, v):
        m, l, acc = carry
        s = jnp.where(mask[None], s, NEG_INF)
        m_new = jnp.maximum(m, jnp.max(s, axis=-1, keepdims=True))
        m_safe = jnp.where(m_new == NEG_INF, 0.0, m_new)
        alpha = jnp.exp(m - m_safe)
        p = jnp.exp(s - m_safe)
        l = alpha * l + jnp.sum(p, axis=-1, keepdims=True)
        pv = jnp.dot(p.reshape(rows, p.shape[-1]).astype(BF16), v, preferred_element_type=F32)
        acc = alpha * acc + pv.reshape(REP, Q_BLOCK, HEAD_DIM)
        return m_new, l, acc

    far_end = (n - 1) * Q_BLOCK

    def far_step(j, carry):
        k0 = pl.multiple_of(j * FAR_TILE, FAR_TILE)
        k = k_ref[pl.ds(k0, FAR_TILE), :]
        v = v_ref[pl.ds(k0, FAR_TILE), :]
        s = _qk(q4, k).reshape(REP, Q_BLOCK, FAR_TILE) * SCALE + far_bias
        picked = jnp.dot(selb, efar_ref[j], preferred_element_type=F32) > 0.5
        kidx = k0 + lax.broadcasted_iota(jnp.int32, (Q_BLOCK, FAR_TILE), 1)
        return online(carry, s, picked & (kidx < far_end), v)

    init = (jnp.full((REP, Q_BLOCK, 1), NEG_INF, F32), jnp.zeros((REP, Q_BLOCK, 1), F32),
            jnp.zeros((REP, Q_BLOCK, HEAD_DIM), F32))
    n_far = (jnp.maximum(far_end, 0) + FAR_TILE - 1) // FAR_TILE
    carry = lax.fori_loop(0, n_far, far_step, init)

    pb = jnp.maximum(n - 1, 0)
    p0 = pl.multiple_of(pb * Q_BLOCK, Q_BLOCK)
    c0 = pl.multiple_of(n * Q_BLOCK, Q_BLOCK)
    kcat = jnp.concatenate([k_ref[pl.ds(p0, Q_BLOCK), :], k_ref[pl.ds(c0, Q_BLOCK), :]], axis=0)
    vcat = jnp.concatenate([v_ref[pl.ds(p0, Q_BLOCK), :], v_ref[pl.ds(c0, Q_BLOCK), :]], axis=0)
    s = _qk(q4, kcat).reshape(REP, Q_BLOCK, 2 * Q_BLOCK) * SCALE + bias_ref[...]
    picked = jnp.concatenate(
        [jnp.dot(selb, enear_ref[pb], preferred_element_type=F32),
         jnp.dot(selb, enear_ref[n], preferred_element_type=F32)], axis=1) > 0.5
    qi = lax.broadcasted_iota(jnp.int32, (Q_BLOCK, 2 * Q_BLOCK), 0)
    kj = lax.broadcasted_iota(jnp.int32, (Q_BLOCK, 2 * Q_BLOCK), 1)
    causal = (qi + Q_BLOCK - kj >= 0) & ((n > 0) | (kj >= Q_BLOCK))
    m, l, acc = online(carry, s, picked & causal, vcat)
    o = acc / jnp.maximum(l, F32_TINY)
    for r in range(REP):
        o_ref[:, r * HEAD_DIM:(r + 1) * HEAD_DIM] = o[r].astype(o_ref.dtype)


def _expand_matrix(seq, tile):
    nslc = seq // SLC_LEN
    key = np.arange(seq).reshape(seq // tile, 1, tile)
    return (key // SLC_LEN == np.arange(nslc)[None, :, None]).astype(np.float32)


def _slc_attention(pa, sel, wbias, rel_bias_c, batch, seq):
    nb = seq // Q_BLOCK
    nslc = seq // SLC_LEN
    qw = REP * HEAD_DIM
    nfar = max(seq // FAR_TILE, 1)
    efar = jnp.asarray(_expand_matrix(max(seq, FAR_TILE), FAR_TILE)[:, :nslc], BF16)
    enear = jnp.asarray(_expand_matrix(seq, Q_BLOCK), BF16)
    far_bias = rel_bias_c[N_BUCKETS - 1].reshape(GROUP_HEADS, 1, 1)
    return pl.pallas_call(
        _slc_attn_kernel,
        out_shape=jax.ShapeDtypeStruct((batch * seq, GROUP_WIDTH), F32),
        grid=(batch, KV_HEADS, nb),
        in_specs=[pl.BlockSpec((Q_BLOCK, qw), lambda b, g, n: (b * nb + n, PA_CQ // qw + g)),
                  pl.BlockSpec((seq, HEAD_DIM), lambda b, g, n: (b, PA_CKS // HEAD_DIM + g)),
                  pl.BlockSpec((seq, HEAD_DIM), lambda b, g, n: (b, PA_CVS // HEAD_DIM + g)),
                  pl.BlockSpec((1, 1, Q_BLOCK, nslc), lambda b, g, n: (b, g, n, 0)),
                  pl.BlockSpec((REP, Q_BLOCK, 2 * Q_BLOCK),
                               lambda b, g, n: (GROUP_HEADS // REP + g, 0, 0)),
                  pl.BlockSpec((REP, 1, 1), lambda b, g, n: (g, 0, 0)),
                  pl.BlockSpec((nfar, nslc, FAR_TILE), lambda b, g, n: (0, 0, 0)),
                  pl.BlockSpec((nb, nslc, Q_BLOCK), lambda b, g, n: (0, 0, 0))],
        out_specs=pl.BlockSpec((Q_BLOCK, qw), lambda b, g, n: (b * nb + n, g)),
        compiler_params=_params("parallel", "parallel", "arbitrary"),
        name="nsa_slc_attn",
    )(pa, pa, pa, sel, wbias, far_bias, efar, enear)


SB_TK = 256
SB_BLOCKS = 4
SB_STEP = SB_BLOCKS * SB_TK
LOG2E = math.log2(math.e)


def _sb_kernel(q_ref, k_ref, v_ref, u_ref, o_ref):
    n = pl.program_id(2)
    u2 = u_ref[...]
    q = q_ref[...]

    def step(j, carry, masked):
        later, acc = carry
        k0 = pl.multiple_of(j * SB_STEP, SB_STEP)
        k = k_ref[pl.ds(k0, SB_STEP), :]
        v = v_ref[pl.ds(k0, SB_STEP), :]
        z = _qk(q, k) * (SCALE * LOG2E)
        zneg = jnp.minimum(z, 0.0)
        zpos_neg = zneg - z
        t = jnp.log2(1.0 + jnp.exp2(zneg + zpos_neg))
        log_keep = zpos_neg - t
        if masked:
            qi = lax.broadcasted_iota(jnp.int32, (SB_STEP, SB_STEP), 0)
            kj = lax.broadcasted_iota(jnp.int32, (SB_STEP, SB_STEP), 1)
            before = kj < qi
            log_keep = jnp.where(before, log_keep, 0.0)
        hi = log_keep.astype(BF16)
        lo = (log_keep - hi.astype(F32)).astype(BF16)
        blocks = [jnp.concatenate([hi[:, b * SB_TK:(b + 1) * SB_TK], lo[:, b * SB_TK:(b + 1) * SB_TK]],
                                  axis=1) for b in range(SB_BLOCKS)]
        suffix = jnp.dot(jnp.concatenate(blocks, axis=0), u2, preferred_element_type=F32)
        parts = [None] * SB_BLOCKS
        for b in range(SB_BLOCKS - 1, -1, -1):
            sfx = suffix[b * SB_STEP:(b + 1) * SB_STEP]
            parts[b] = jnp.exp2(z[:, b * SB_TK:(b + 1) * SB_TK] + sfx + later)
            later = later + sfx[:, 0:1]
        a = jnp.concatenate(parts, axis=1)
        if masked:
            a = jnp.where(before, a, 0.0)
        acc = acc + jnp.dot(a.astype(BF16), v, preferred_element_type=F32)
        return later, acc

    carry = (jnp.zeros((SB_STEP, 1), F32), jnp.zeros((SB_STEP, HEAD_DIM), F32))
    carry = step(n, carry, True)
    carry = lax.fori_loop(0, n, lambda i, c: step(n - 1 - i, c, False), carry)
    o_ref[...] = carry[1].astype(o_ref.dtype)


def _stick_breaking(pa, batch, seq):
    assert seq % SB_STEP == 0
    nb = seq // SB_STEP
    tri = np.tril(np.ones((SB_TK, SB_TK), np.float32))
    u2 = jnp.asarray(np.concatenate([tri, tri], axis=0), BF16)
    return pl.pallas_call(
        _sb_kernel,
        out_shape=jax.ShapeDtypeStruct((batch * seq, GROUP_WIDTH), F32),
        grid=(batch, GROUP_HEADS, nb),
        in_specs=[pl.BlockSpec((SB_STEP, HEAD_DIM), lambda b, h, n: (b * nb + n, PA_DQ // HEAD_DIM + h)),
                  pl.BlockSpec((seq, HEAD_DIM), lambda b, h, n: (b, PA_DK // HEAD_DIM + h)),
                  pl.BlockSpec((seq, HEAD_DIM), lambda b, h, n: (b, PA_DV // HEAD_DIM + h)),
                  pl.BlockSpec((2 * SB_TK, SB_TK), lambda b, h, n: (0, 0))],
        out_specs=pl.BlockSpec((SB_STEP, HEAD_DIM), lambda b, h, n: (b * nb + n, h)),
        compiler_params=_params("parallel", "parallel", "arbitrary"),
        name="stick_breaking",
    )(pa, pa, pa, u2)


CONV_TILE = 256
CONV_HALO = 32


def _conv_kernel(val_ref, gate_ref, hval_ref, hgate_ref, dw_ref, dwb_ref, lng_ref, lnb_ref, pw_ref,
                 o_ref, ext_ref):
    n = pl.program_id(1)
    halo = hval_ref[...] * jax.nn.sigmoid(hgate_ref[...])
    ext_ref[0:CONV_HALO, :] = jnp.where(n > 0, halo, 0.0)
    ext_ref[CONV_HALO:, :] = val_ref[...] * jax.nn.sigmoid(gate_ref[...])
    first = CONV_HALO - (CONV_WIDTH - 1)
    acc = jnp.zeros((CONV_TILE, GROUP_WIDTH), F32) + dwb_ref[...]
    for w in range(CONV_WIDTH):
        acc = acc + ext_ref[first + w:first + w + CONV_TILE, :] * dw_ref[w:w + 1, :]
    mu = jnp.mean(acc, axis=-1, keepdims=True)
    cen = acc - mu
    var = jnp.mean(cen * cen, axis=-1, keepdims=True)
    y = cen * lax.rsqrt(var + NORM_EPS) * lng_ref[...] + lnb_ref[...]
    y = y * jax.nn.sigmoid(y)
    o_ref[...] = jnp.dot(y.astype(BF16), pw_ref[...], preferred_element_type=F32).astype(o_ref.dtype)


def _conformer_conv(pf, batch, seq, dw, dw_b, ln_g, ln_b, pw):
    nt = seq // CONV_TILE
    hpt = CONV_TILE // CONV_HALO
    c = GROUP_WIDTH
    vec = pl.BlockSpec((1, c), lambda b, n: (0, 0))

    def halo(col):
        return lambda b, n: (jnp.maximum((b * nt + n) * hpt - 1, 0), col // c)

    return pl.pallas_call(
        _conv_kernel,
        out_shape=jax.ShapeDtypeStruct((batch * seq, c), F32),
        grid=(batch, nt),
        in_specs=[pl.BlockSpec((CONV_TILE, c), lambda b, n: (b * nt + n, PF_VAL // c)),
                  pl.BlockSpec((CONV_TILE, c), lambda b, n: (b * nt + n, PF_GATE // c)),
                  pl.BlockSpec((CONV_HALO, c), halo(PF_VAL)),
                  pl.BlockSpec((CONV_HALO, c), halo(PF_GATE)),
                  pl.BlockSpec((CONV_WIDTH, c), lambda b, n: (0, 0)),
                  vec, vec, vec,
                  pl.BlockSpec((c, c), lambda b, n: (0, 0))],
        out_specs=pl.BlockSpec((CONV_TILE, c), lambda b, n: (b * nt + n, 0)),
        scratch_shapes=[pltpu.VMEM((CONV_HALO + CONV_TILE, c), F32)],
        compiler_params=_params("parallel", "arbitrary"),
        name="conformer_conv",
    )(pf, pf, pf, pf, dw, dw_b.reshape(1, c), ln_g.reshape(1, c), ln_b.reshape(1, c), pw.astype(BF16))


MIX_TILE = 256


def _mix_kernel(oa_ref, ob_ref, ocmp_ref, oslc_ref, owin_ref, gl_ref, od_ref, g_ref, o_ref):
    gates = jax.nn.sigmoid(gl_ref[...])

    def norm_store(x, grp):
        y = x * lax.rsqrt(jnp.mean(x * x, axis=-1, keepdims=True) + NORM_EPS)
        sl = slice(grp * GROUP_WIDTH, (grp + 1) * GROUP_WIDTH)
        o_ref[:, sl] = (y * g_ref[:, sl]).astype(o_ref.dtype)

    norm_store(oa_ref[...], 0)
    norm_store(ob_ref[...], 1)
    heads = []
    for h in range(GROUP_HEADS):
        sl = slice(h * HEAD_DIM, (h + 1) * HEAD_DIM)
        c0 = h * N_BRANCH
        heads.append(gates[:, c0:c0 + 1] * ocmp_ref[:, sl] + gates[:, c0 + 1:c0 + 2] * oslc_ref[:, sl]
                     + gates[:, c0 + 2:c0 + 3] * owin_ref[:, sl])
    norm_store(jnp.concatenate(heads, axis=1), 2)
    norm_store(od_ref[...], 3)


def _mix(o_a, o_b, o_cmp, o_slc, o_win, pf, o_d, g):
    t = o_a.shape[0]
    grp = pl.BlockSpec((MIX_TILE, GROUP_WIDTH), lambda i: (i, 0))
    return pl.pallas_call(
        _mix_kernel,
        out_shape=jax.ShapeDtypeStruct((t, D_MODEL), BF16),
        grid=(t // MIX_TILE,),
        in_specs=[grp, grp, grp, grp, grp,
                  pl.BlockSpec((MIX_TILE, LANES), lambda i: (i, PF_G // LANES)),
                  grp,
                  pl.BlockSpec((1, D_MODEL), lambda i: (0, 0))],
        out_specs=pl.BlockSpec((MIX_TILE, D_MODEL), lambda i: (i, 0)),
        compiler_params=_params("parallel"),
        name="mix_norm",
    )(o_a, o_b, o_cmp, o_slc, o_win, pf, o_d, g.reshape(1, D_MODEL))


MOE_TILE = 512
MOE_UP_BN = 512
MOE_DOWN_BN = 512
ROUTE_TILE = 256
GATHER_TILE = 256
COMBINE_TILE = 128


def _norm_route_kernel(x_ref, g_ref, r_ref, h_ref, route_ref):
    x = x_ref[...]
    y = x * lax.rsqrt(jnp.mean(x * x, axis=-1, keepdims=True) + NORM_EPS) * g_ref[...]
    h_ref[...] = y
    logits = jnp.dot(y.astype(BF16), r_ref[...], preferred_element_type=F32)
    lane = lax.broadcasted_iota(jnp.int32, logits.shape, 1)
    logits = jnp.where(lane < N_EXPERTS, logits, NEG_INF)
    e = jnp.exp(logits - jnp.max(logits, axis=-1, keepdims=True))
    probs = e / jnp.sum(e, axis=-1, keepdims=True)
    p1 = jnp.max(probs, axis=-1, keepdims=True)
    i1 = jnp.min(jnp.where(probs == p1, lane, LANES), axis=-1, keepdims=True)
    rest = jnp.where(lane == i1, -1.0, probs)
    p2 = jnp.max(rest, axis=-1, keepdims=True)
    i2 = jnp.min(jnp.where(rest == p2, lane, LANES), axis=-1, keepdims=True)
    tot = p1 + p2
    route_ref[...] = jnp.where(lane == 0, p1 / tot,
                               jnp.where(lane == 1, p2 / tot,
                                         jnp.where(lane == 2, i1.astype(F32),
                                                   jnp.where(lane == 3, i2.astype(F32), 0.0))))


def _norm_route(x, g, router):
    m, d = x.shape
    rpad = jnp.zeros((d, LANES), BF16).at[:, :N_EXPERTS].set(router.astype(BF16))
    return pl.pallas_call(
        _norm_route_kernel,
        out_shape=(jax.ShapeDtypeStruct((m, d), F32), jax.ShapeDtypeStruct((m, LANES), F32)),
        grid=(m // ROUTE_TILE,),
        in_specs=[pl.BlockSpec((ROUTE_TILE, d), lambda i: (i, 0)),
                  pl.BlockSpec((1, d), lambda i: (0, 0)),
                  pl.BlockSpec((d, LANES), lambda i: (0, 0))],
        out_specs=(pl.BlockSpec((ROUTE_TILE, d), lambda i: (i, 0)),
                   pl.BlockSpec((ROUTE_TILE, LANES), lambda i: (i, 0))),
        compiler_params=_params("parallel"),
        name="ffn_norm_route",
    )(x, g.reshape(1, d), rpad)


def _row_copy(src_hbm, dst_ref, sem, src_row, dst_row):
    return pltpu.make_async_copy(src_hbm.at[pl.ds(src_row, 1), :], dst_ref.at[pl.ds(dst_row, 1), :], sem)


def _gather_kernel(tok_ref, nu_ref, h_hbm, o_ref, buf_ref, sem):
    i = pl.program_id(0)
    n_live = nu_ref[0] * (MOE_TILE // GATHER_TILE)

    def issue(tile, slot):
        def start(r, c):
            _row_copy(h_hbm, buf_ref.at[slot], sem.at[slot], tok_ref[tile * GATHER_TILE + r], r).start()
            return c
        lax.fori_loop(0, GATHER_TILE, start, 0)

    @pl.when(i == 0)
    def _():
        issue(0, 0)

    @pl.when(i + 1 < n_live)
    def _():
        issue(i + 1, (i + 1) % 2)

    @pl.when(i < n_live)
    def _():
        slot = i % 2

        def wait(r, c):
            _row_copy(h_hbm, buf_ref.at[slot], sem.at[slot], 0, r).wait()
            return c

        lax.fori_loop(0, GATHER_TILE, wait, 0)
        o_ref[...] = buf_ref[slot].astype(o_ref.dtype)

    @pl.when(i >= n_live)
    def _():
        o_ref[...] = jnp.zeros_like(o_ref)


def _gather_rows(h, row_token, n_used):
    rows = row_token.shape[0]
    d = h.shape[1]
    return pl.pallas_call(
        _gather_kernel,
        out_shape=jax.ShapeDtypeStruct((rows, d), BF16),
        grid_spec=pltpu.PrefetchScalarGridSpec(
            num_scalar_prefetch=2,
            grid=(rows // GATHER_TILE,),
            in_specs=[pl.BlockSpec(memory_space=pl.ANY)],
            out_specs=pl.BlockSpec((GATHER_TILE, d), lambda i, tok, nu: (i, 0)),
            scratch_shapes=[pltpu.VMEM((2, GATHER_TILE, d), F32), pltpu.SemaphoreType.DMA((2,))]),
        compiler_params=_params("arbitrary"),
        name="moe_gather",
    )(row_token, n_used, h)


CAST_ROWS = 128


def _swiglu_tile(x, s_ref):
    g = jnp.dot(x, s_ref[0], preferred_element_type=F32)
    u = jnp.dot(x, s_ref[1], preferred_element_type=F32)
    return g * jax.nn.sigmoid(g) * u


def _plain_tile(x, s_ref):
    return jnp.dot(x, s_ref[0], preferred_element_type=F32)


def _grouped_kernel(te_ref, nu_ref, first_ref, run_ref, nruns_ref, rune_ref, x_ref, *rest,
                    n_weights, bn, n_cols, tile_fn):
    w_hbm = rest[:n_weights]
    o_ref, wbuf_ref, s_ref, sem = rest[n_weights:]
    j = pl.program_id(0)
    i = pl.program_id(1)
    nj = pl.num_programs(0)
    k = s_ref.shape[1]
    last_width = n_cols - (pl.cdiv(n_cols, bn) - 1) * bn
    live = i < nu_ref[0]

    def tile_copy(w, expert, jj, slot, width):
        col0 = pl.multiple_of(jj * bn, bn)
        return pltpu.make_async_copy(w_hbm[w].at[expert, :, pl.ds(col0, width)],
                                     wbuf_ref.at[slot, w, :, pl.ds(0, width)], sem.at[slot, w])

    def by_width(jj, fn):
        if last_width == bn:
            fn(bn)
        else:
            pl.when(jj < nj - 1)(lambda: fn(bn))
            pl.when(jj == nj - 1)(lambda: fn(last_width))

    def issue(expert, jj, slot):
        def go(width):
            for w in range(n_weights):
                tile_copy(w, expert, jj, slot, width).start()
        by_width(jj, go)

    def wait_and_cast(expert, jj, slot):
        def go(width):
            for w in range(n_weights):
                tile_copy(w, expert, jj, slot, width).wait()

                def rows(c, carry, w=w):
                    r0 = pl.multiple_of(c * CAST_ROWS, CAST_ROWS)
                    s_ref[w, pl.ds(r0, CAST_ROWS), 0:width] = (
                        wbuf_ref[slot, w, pl.ds(r0, CAST_ROWS), 0:width].astype(BF16))
                    return carry

                lax.fori_loop(0, k // CAST_ROWS, rows, 0)
        by_width(jj, go)

    @pl.when(live & (first_ref[i] == 1))
    def _():
        run = run_ref[i]
        group = j * nruns_ref[0] + run
        slot = group % 2
        expert = te_ref[i]

        @pl.when(group == 0)
        def _():
            issue(expert, j, 0)

        wait_and_cast(expert, j, slot)
        wrap = run + 1 == nruns_ref[0]
        next_run = jnp.where(wrap, 0, run + 1)
        next_j = jnp.where(wrap, j + 1, j)

        @pl.when(next_j < nj)
        def _():
            issue(rune_ref[next_run], next_j, 1 - slot)

    @pl.when(live)
    def _():
        o_ref[...] = tile_fn(x_ref[...], s_ref).astype(o_ref.dtype)

    @pl.when(jnp.logical_not(live))
    def _():
        o_ref[...] = jnp.zeros_like(o_ref)


def _grouped_call(tile_fn, tables, xs, weights, bn, out_dtype, name):
    rows, k = xs.shape
    n = weights[0].shape[2]
    nt = rows // MOE_TILE
    nw = len(weights)
    assert k % CAST_ROWS == 0 and bn % LANES == 0 and n % LANES == 0

    def used(i, nu):
        return jnp.minimum(i, nu[0] - 1)

    body = functools.partial(_grouped_kernel, n_weights=nw, bn=bn, n_cols=n, tile_fn=tile_fn)
    return pl.pallas_call(
        body,
        out_shape=jax.ShapeDtypeStruct((rows, n), out_dtype),
        grid_spec=pltpu.PrefetchScalarGridSpec(
            num_scalar_prefetch=len(tables),
            grid=(pl.cdiv(n, bn), nt),
            in_specs=[pl.BlockSpec((MOE_TILE, k), lambda j, i, te, nu, *_: (used(i, nu), 0))]
            + [pl.BlockSpec(memory_space=pl.ANY) for _ in weights],
            out_specs=pl.BlockSpec((MOE_TILE, bn), lambda j, i, *_: (i, j)),
            scratch_shapes=[pltpu.VMEM((2, nw, k, bn), F32), pltpu.VMEM((nw, k, bn), BF16),
                            pltpu.SemaphoreType.DMA((2, nw))]),
        compiler_params=_params("arbitrary", "arbitrary"),
        name=name,
    )(*tables, xs, *weights)


def _combine_kernel(dest_ref, x_ref, route_ref, y_hbm, *rest, normalize):
    if normalize:
        g_ref, o_ref, y0_ref, y1_ref, sem = rest
    else:
        o_ref, y0_ref, y1_ref, sem = rest
    i = pl.program_id(0)

    def issue(tile, slot):
        def start(r, c):
            tok = tile * COMBINE_TILE + r
            _row_copy(y_hbm, y0_ref.at[slot], sem.at[slot], dest_ref[2 * tok], r).start()
            _row_copy(y_hbm, y1_ref.at[slot], sem.at[slot], dest_ref[2 * tok + 1], r).start()
            return c
        lax.fori_loop(0, COMBINE_TILE, start, 0)

    @pl.when(i == 0)
    def _():
        issue(0, 0)

    @pl.when(i + 1 < pl.num_programs(0))
    def _():
        issue(i + 1, (i + 1) % 2)

    slot = i % 2

    def wait(r, c):
        _row_copy(y_hbm, y0_ref.at[slot], sem.at[slot], 0, r).wait()
        _row_copy(y_hbm, y1_ref.at[slot], sem.at[slot], 0, r).wait()
        return c

    lax.fori_loop(0, COMBINE_TILE, wait, 0)
    route = route_ref[...]
    out = x_ref[...] + route[:, 0:1] * y0_ref[slot] + route[:, 1:2] * y1_ref[slot]
    if normalize:
        out = out * lax.rsqrt(jnp.mean(out * out, axis=-1, keepdims=True) + NORM_EPS) * g_ref[...]
    o_ref[...] = out


def _moe_combine(x2, route, y, dest, out_norm_g=None):
    t, d = x2.shape
    normalize = out_norm_g is not None
    in_specs = [pl.BlockSpec((COMBINE_TILE, d), lambda i, dest: (i, 0)),
                pl.BlockSpec((COMBINE_TILE, LANES), lambda i, dest: (i, 0)),
                pl.BlockSpec(memory_space=pl.ANY)]
    args = [dest, x2, route, y]
    if normalize:
        in_specs.append(pl.BlockSpec((1, d), lambda i, dest: (0, 0)))
        args.append(out_norm_g.reshape(1, d))
    return pl.pallas_call(
        functools.partial(_combine_kernel, normalize=normalize),
        out_shape=jax.ShapeDtypeStruct((t, d), F32),
        grid_spec=pltpu.PrefetchScalarGridSpec(
            num_scalar_prefetch=1,
            grid=(t // COMBINE_TILE,),
            in_specs=in_specs,
            out_specs=pl.BlockSpec((COMBINE_TILE, d), lambda i, dest: (i, 0)),
            scratch_shapes=[pltpu.VMEM((2, COMBINE_TILE, d), F32), pltpu.VMEM((2, COMBINE_TILE, d), F32),
                            pltpu.SemaphoreType.DMA((2,))]),
        compiler_params=_params("arbitrary"),
        name="moe_combine",
    )(*args)


def _routing_tables(top_i):
    t = top_i.shape[0]
    e_flat = top_i.reshape(-1)
    onehot = (e_flat[:, None] == jnp.arange(N_EXPERTS, dtype=jnp.int32)[None, :]).astype(jnp.int32)
    csum = jnp.cumsum(onehot, axis=0)
    counts = csum[-1]
    pos = jnp.take_along_axis(csum, e_flat[:, None], axis=1)[:, 0] - 1
    padded = ((counts + MOE_TILE - 1) // MOE_TILE) * MOE_TILE
    ends = jnp.cumsum(padded)
    dest = ((ends - padded)[e_flat] + pos).astype(jnp.int32)
    rows = TOP_K * t + N_EXPERTS * MOE_TILE
    row_token = jnp.zeros((rows,), jnp.int32).at[dest].set(jnp.arange(TOP_K * t, dtype=jnp.int32) // TOP_K)
    tile_start = jnp.arange(rows // MOE_TILE, dtype=jnp.int32) * MOE_TILE
    tile_expert = jnp.minimum(jnp.sum(tile_start[:, None] >= ends[None, :], axis=1), N_EXPERTS - 1)
    n_used = (ends[-1:] // MOE_TILE).astype(jnp.int32)
    tile_expert = tile_expert.astype(jnp.int32)
    tiles = jnp.arange(rows // MOE_TILE, dtype=jnp.int32)
    prev = jnp.concatenate([jnp.full((1,), -1, jnp.int32), tile_expert[:-1]])
    first = ((tiles < n_used[0]) & (tile_expert != prev)).astype(jnp.int32)
    run = jnp.maximum(jnp.cumsum(first) - 1, 0).astype(jnp.int32)
    n_runs = jnp.sum(first, keepdims=True).astype(jnp.int32)
    run_expert = jnp.zeros((N_EXPERTS,), jnp.int32).at[
        jnp.where(first == 1, run, N_EXPERTS)].set(tile_expert, mode="drop")
    return dest, row_token, n_used, (tile_expert, n_used, first, run, n_runs, run_expert)


def _moe_swiglu(x2, norm_g, router, w_gate, w_up, w_down, out_norm_g=None):
    h, route = _norm_route(x2, norm_g, router)
    top_i = route[:, 2:4].astype(jnp.int32)
    dest, row_token, n_used, tables = _routing_tables(top_i)
    xs = _gather_rows(h, row_token, n_used)
    hid = _grouped_call(_swiglu_tile, tables, xs, [w_gate, w_up], MOE_UP_BN, BF16, "moe_up")
    y = _grouped_call(_plain_tile, tables, hid, [w_down], MOE_DOWN_BN, F32, "moe_down")
    return _moe_combine(x2, route, y, dest, out_norm_g)


def _split_in_weights(w):
    sizes = (1024, 256, 256, 1024, 1024, 1024, 256, 256, 256, 256, 256, 256, 24, 1024, 1024, 1024)
    offs = np.concatenate([[0], np.cumsum(sizes)])
    (a_q, a_k, a_v, b_val, b_gate, c_q, c_kc, c_vc, c_ks, c_vs, c_kw, c_vw, c_g,
     d_q, d_k, d_v) = [w[:, int(offs[i]):int(offs[i + 1])] for i in range(len(sizes))]
    wa = jnp.concatenate([a_q, a_k, a_v, c_q, c_ks, c_vs, c_kw, c_vw, d_q, d_k, d_v], axis=1)
    pad = jnp.zeros((w.shape[0], PF_COLS - PF_G - c_g.shape[1]), w.dtype)
    wf = jnp.concatenate([b_val, b_gate, c_kc, c_vc, c_g, pad], axis=1)
    return wa.astype(BF16), wf.astype(BF16)


def _mixer_layer(x2, batch, seq, layer, p, wbias, cbias):
    h = _rmsnorm(x2, p["attn_norm_g"][layer], out_dtype=BF16, name="attn_norm")
    wa, wf = _split_in_weights(p["w_in"][layer])
    pa = _matmul(h, wa, bm=1024, bn=512, out_dtype=BF16, name="in_proj_attn")
    pf = _matmul(h, wf, bm=1024, bn=PF_COLS // 3, out_dtype=F32, name="in_proj_f32")

    o_a = _window_attention(pa, batch, seq, PA_AQ, PA_AK, PA_AV, wbias, 0,
                            p["swa_sinks"][layer], "swa_attn")
    o_b = _conformer_conv(pf, batch, seq, p["conv_dw"][layer], p["conv_dw_b"][layer],
                          p["conv_ln_g"][layer], p["conv_ln_b"][layer], p["conv_pw"][layer])
    cmp_kv = _compress(pf, batch, seq, p["nsa_cmp_pe"][layer], p["nsa_cmp_w1"][layer],
                       p["nsa_cmp_w2"][layer])
    o_cmp, sel = _cmp_attention(pa, cmp_kv, cbias, batch, seq)
    o_slc = _slc_attention(pa, sel, wbias, p["rel_bias"][:, GROUP_HEADS:], batch, seq)
    o_win = _window_attention(pa, batch, seq, PA_CQ, PA_CKW, PA_CVW, wbias, GROUP_HEADS, None,
                              "nsa_win_attn")
    o_d = _stick_breaking(pa, batch, seq)
    mixed = _mix(o_a, o_b, o_cmp, o_slc, o_win, pf, o_d, p["mix_norm_g"][layer])
    return _matmul(mixed, p["w_out"][layer].astype(BF16), bm=1024, bn=512, out_dtype=F32,
                   residual=x2, name="out_proj")


def kernel(x, attn_norm_g, ffn_norm_g, final_norm_g, w_in, w_out, mix_norm_g, rel_bias,
           swa_sinks, conv_dw, conv_dw_b, conv_ln_g, conv_ln_b, conv_pw, nsa_cmp_pe,
           nsa_cmp_w1, nsa_cmp_w2, ffn_w_gate, ffn_w_up, ffn_w_down, moe_router,
           moe_w_gate, moe_w_up, moe_w_down):
    batch, seq, _ = x.shape
    t = batch * seq
    p = dict(attn_norm_g=attn_norm_g, w_in=w_in, w_out=w_out, mix_norm_g=mix_norm_g,
             rel_bias=rel_bias, swa_sinks=swa_sinks, conv_dw=conv_dw, conv_dw_b=conv_dw_b,
             conv_ln_g=conv_ln_g, conv_ln_b=conv_ln_b, conv_pw=conv_pw, nsa_cmp_pe=nsa_cmp_pe,
             nsa_cmp_w1=nsa_cmp_w1, nsa_cmp_w2=nsa_cmp_w2)
    wbias = _win_bias(rel_bias)
    cbias = _cmp_bias(rel_bias, seq, GROUP_HEADS)
    x2 = x.reshape(t, D_MODEL)
    for layer in range(DEPTH):
        x2 = _mixer_layer(x2, batch, seq, layer, p, wbias, cbias)
        i = layer // 2
        if layer % 2 == 0:
            hf = _rmsnorm(x2, ffn_norm_g[layer], out_dtype=BF16, name="ffn_norm")
            hid = _swiglu_up(hf, ffn_w_gate[i].astype(BF16), ffn_w_up[i].astype(BF16),
                             bm=1024, bn=256, name="ffn_up")
            x2 = _matmul_ksplit_res(hid, ffn_w_down[i].astype(BF16), x2,
                                    bm=1024, bn=512, bk=D_FF // 2, name="ffn_down")
        else:
            last = layer == DEPTH - 1
            x2 = _moe_swiglu(x2, ffn_norm_g[layer], moe_router[i], moe_w_gate[i], moe_w_up[i],
                             moe_w_down[i], out_norm_g=final_norm_g if last else None)
    if DEPTH % 2 == 1:
        x2 = _rmsnorm(x2, final_norm_g, out_dtype=F32, name="final_norm")
    return x2.reshape(batch, seq, D_MODEL)
```

```python
import functools
import math

import jax
import jax.numpy as jnp
import numpy as np
from jax import lax
from jax.experimental import pallas as pl
from jax.experimental.pallas import tpu as pltpu

D_MODEL = 4096
DEPTH = 2
HEAD_DIM = 128
N_MIXERS = 4
GROUP_WIDTH = D_MODEL // N_MIXERS
GROUP_HEADS = GROUP_WIDTH // HEAD_DIM
KV_HEADS = 2
REP = GROUP_HEADS // KV_HEADS
WINDOW = 128
Q_BLOCK = 128
CONV_WIDTH = 31
CMP_LEN = 32
CMP_STRIDE = 16
SLC_LEN = 64
N_SELECT = 16
N_BRANCH = 3
N_BUCKETS = 32
MAX_DISTANCE = 128
D_FF = 11008
N_EXPERTS = 8
TOP_K = 2
D_EXPERT = D_FF // 2
NORM_EPS = 1e-6
SCALE = HEAD_DIM ** -0.5

VMEM_LIMIT_BYTES = 56 * 1024 * 1024
LANES = 128

F32 = jnp.float32
BF16 = jnp.bfloat16
NEG_INF = float("-inf")
F32_TINY = float(np.finfo(np.float32).tiny)

PA_AQ, PA_AK, PA_AV = 0, 1024, 1280
PA_CQ, PA_CKS, PA_CVS, PA_CKW, PA_CVW = 1536, 2560, 2816, 3072, 3328
PA_DQ, PA_DK, PA_DV = 3584, 4608, 5632
PA_COLS = 6656
PF_VAL, PF_GATE, PF_KC, PF_VC, PF_G = 0, 1024, 2048, 2304, 2560
PF_COLS = 2688


def _params(*sem):
    return pltpu.CompilerParams(dimension_semantics=sem, vmem_limit_bytes=VMEM_LIMIT_BYTES)


def _bucket_thresholds():
    n = np.arange(0, 4 * MAX_DISTANCE)
    max_exact = N_BUCKETS // 2
    nf = np.maximum(n, 1).astype(np.float32)
    large = max_exact + (np.log(nf / max_exact) / math.log(MAX_DISTANCE / max_exact)
                         * (N_BUCKETS - max_exact)).astype(np.int32)
    large = np.minimum(large, N_BUCKETS - 1)
    bucket = np.where(n < max_exact, n, large)
    return [int(np.argmax(bucket >= k)) for k in range(N_BUCKETS)]


BUCKET_THRESHOLDS = _bucket_thresholds()


def _mm_kernel(x_ref, w_ref, o_ref):
    o_ref[...] = jnp.dot(x_ref[...], w_ref[...], preferred_element_type=F32).astype(o_ref.dtype)


def _mm_res_kernel(x_ref, w_ref, r_ref, o_ref):
    acc = jnp.dot(x_ref[...], w_ref[...], preferred_element_type=F32)
    o_ref[...] = (r_ref[...] + acc).astype(o_ref.dtype)


def _mm_res_ksplit_kernel(x_ref, w_ref, r_ref, o_ref):
    @pl.when(pl.program_id(2) == 0)
    def _():
        o_ref[...] = r_ref[...]

    o_ref[...] += jnp.dot(x_ref[...], w_ref[...], preferred_element_type=F32)


def _matmul(x, w, *, bm, bn, out_dtype, residual=None, name="matmul"):
    m, k = x.shape
    _, n = w.shape
    assert m % bm == 0 and n % bn == 0
    in_specs = [pl.BlockSpec((bm, k), lambda i, j: (i, 0)),
                pl.BlockSpec((k, bn), lambda i, j: (0, j))]
    args = [x, w]
    body = _mm_kernel
    if residual is not None:
        in_specs.append(pl.BlockSpec((bm, bn), lambda i, j: (i, j)))
        args.append(residual)
        body = _mm_res_kernel
    return pl.pallas_call(
        body,
        out_shape=jax.ShapeDtypeStruct((m, n), out_dtype),
        grid=(m // bm, n // bn),
        in_specs=in_specs,
        out_specs=pl.BlockSpec((bm, bn), lambda i, j: (i, j)),
        compiler_params=_params("parallel", "arbitrary"),
        name=name,
    )(*args)


def _matmul_ksplit_res(x, w, residual, *, bm, bn, bk, name="matmul_ksplit"):
    m, k = x.shape
    _, n = w.shape
    assert m % bm == 0 and n % bn == 0 and k % bk == 0
    return pl.pallas_call(
        _mm_res_ksplit_kernel,
        out_shape=jax.ShapeDtypeStruct((m, n), F32),
        grid=(m // bm, n // bn, k // bk),
        in_specs=[pl.BlockSpec((bm, bk), lambda i, j, kk: (i, kk)),
                  pl.BlockSpec((bk, bn), lambda i, j, kk: (kk, j)),
                  pl.BlockSpec((bm, bn), lambda i, j, kk: (i, j))],
        out_specs=pl.BlockSpec((bm, bn), lambda i, j, kk: (i, j)),
        compiler_params=_params("parallel", "arbitrary", "arbitrary"),
        name=name,
    )(x, w, residual)


def _swiglu_kernel(x_ref, wg_ref, wu_ref, o_ref):
    x = x_ref[...]
    g = jnp.dot(x, wg_ref[...], preferred_element_type=F32)
    u = jnp.dot(x, wu_ref[...], preferred_element_type=F32)
    o_ref[...] = (g * jax.nn.sigmoid(g) * u).astype(o_ref.dtype)


def _swiglu_up(x, wg, wu, *, bm, bn, name="swiglu_up"):
    m, k = x.shape
    _, n = wg.shape
    assert m % bm == 0 and n % bn == 0
    return pl.pallas_call(
        _swiglu_kernel,
        out_shape=jax.ShapeDtypeStruct((m, n), BF16),
        grid=(m // bm, n // bn),
        in_specs=[pl.BlockSpec((bm, k), lambda i, j: (i, 0)),
                  pl.BlockSpec((k, bn), lambda i, j: (0, j)),
                  pl.BlockSpec((k, bn), lambda i, j: (0, j))],
        out_specs=pl.BlockSpec((bm, bn), lambda i, j: (i, j)),
        compiler_params=_params("parallel", "arbitrary"),
        name=name,
    )(x, wg, wu)


def _rmsnorm_kernel(x_ref, g_ref, o_ref):
    x = x_ref[...]
    y = x * lax.rsqrt(jnp.mean(x * x, axis=-1, keepdims=True) + NORM_EPS)
    o_ref[...] = (y * g_ref[...]).astype(o_ref.dtype)


def _rmsnorm(x, g, *, out_dtype, bm=256, name="rmsnorm"):
    m, d = x.shape
    return pl.pallas_call(
        _rmsnorm_kernel,
        out_shape=jax.ShapeDtypeStruct((m, d), out_dtype),
        grid=(m // bm,),
        in_specs=[pl.BlockSpec((bm, d), lambda i: (i, 0)),
                  pl.BlockSpec((1, d), lambda i: (0, 0))],
        out_specs=pl.BlockSpec((bm, d), lambda i: (i, 0)),
        compiler_params=_params("parallel"),
        name=name,
    )(x, g.reshape(1, d))


def _bias_of_dist(dist, tab_ref, head):
    out = jnp.full(dist.shape, tab_ref[0, head], F32)
    for k in range(1, N_BUCKETS):
        out = jnp.where(dist >= BUCKET_THRESHOLDS[k], tab_ref[k, head], out)
    return out


def _win_bias_kernel(tab_ref, o_ref):
    h = pl.program_id(0)
    qi = lax.broadcasted_iota(jnp.int32, (Q_BLOCK, 2 * Q_BLOCK), 0)
    kj = lax.broadcasted_iota(jnp.int32, (Q_BLOCK, 2 * Q_BLOCK), 1)
    o_ref[0] = _bias_of_dist(qi + Q_BLOCK - kj, tab_ref, h)


def _win_bias(rel_bias):
    nh = rel_bias.shape[1]
    return pl.pallas_call(
        _win_bias_kernel,
        out_shape=jax.ShapeDtypeStruct((nh, Q_BLOCK, 2 * Q_BLOCK), F32),
        grid=(nh,),
        in_specs=[pl.BlockSpec(memory_space=pltpu.SMEM)],
        out_specs=pl.BlockSpec((1, Q_BLOCK, 2 * Q_BLOCK), lambda h: (h, 0, 0)),
        compiler_params=_params("arbitrary"),
        name="win_bias",
    )(rel_bias)


def _cmp_bias_kernel(tab_ref, o_ref, *, head0, rows):
    h = pl.program_id(0) + head0
    n = pl.program_id(1)
    ncmp = o_ref.shape[2]
    t = n * rows + lax.broadcasted_iota(jnp.int32, (rows, ncmp), 0)
    c = lax.broadcasted_iota(jnp.int32, (rows, ncmp), 1)
    o_ref[0] = _bias_of_dist(t - c * CMP_STRIDE - (CMP_LEN - 1), tab_ref, h)


def _cmp_bias(rel_bias, seq, head0, rows=512):
    ncmp = seq // CMP_STRIDE
    rows = min(rows, seq)
    return pl.pallas_call(
        functools.partial(_cmp_bias_kernel, head0=head0, rows=rows),
        out_shape=jax.ShapeDtypeStruct((GROUP_HEADS, seq, ncmp), F32),
        grid=(GROUP_HEADS, seq // rows),
        in_specs=[pl.BlockSpec(memory_space=pltpu.SMEM)],
        out_specs=pl.BlockSpec((1, rows, ncmp), lambda h, n: (h, n, 0)),
        compiler_params=_params("arbitrary", "arbitrary"),
        name="cmp_bias",
    )(rel_bias)


def _stack_heads(q):
    return jnp.concatenate([q[:, r * HEAD_DIM:(r + 1) * HEAD_DIM] for r in range(REP)], axis=0)


def _store_heads(o_ref, o, rows):
    for r in range(REP):
        o_ref[:, r * HEAD_DIM:(r + 1) * HEAD_DIM] = o[r * rows:(r + 1) * rows].astype(o_ref.dtype)


def _qk(q, k):
    return lax.dot_general(q, k, (((1,), (1,)), ((), ())), preferred_element_type=F32)


def _window_kernel(q0_ref, q1_ref, kp_ref, kc_ref, vp_ref, vc_ref, bias_ref, sink_ref, o_ref, *,
                   has_sink):
    n = pl.program_id(1)
    qi = lax.broadcasted_iota(jnp.int32, (Q_BLOCK, 2 * Q_BLOCK), 0)
    kj = lax.broadcasted_iota(jnp.int32, (Q_BLOCK, 2 * Q_BLOCK), 1)
    dist = qi + Q_BLOCK - kj
    mask = ((dist >= 0) & (dist < WINDOW) & ((n > 0) | (kj >= Q_BLOCK)))[None]
    sink_col = (dist == WINDOW)[None]
    for g, q_ref in enumerate((q0_ref, q1_ref)):
        cols = slice(g * HEAD_DIM, (g + 1) * HEAD_DIM)
        heads = slice(g * REP, (g + 1) * REP)
        q4 = _stack_heads(q_ref[...])
        kcat = jnp.concatenate([kp_ref[:, cols], kc_ref[:, cols]], axis=0)
        vcat = jnp.concatenate([vp_ref[:, cols], vc_ref[:, cols]], axis=0)
        s = _qk(q4, kcat).reshape(REP, Q_BLOCK, 2 * Q_BLOCK) * SCALE + bias_ref[heads]
        if has_sink:
            s = jnp.where(sink_col, sink_ref[heads], s)
            live = mask | sink_col
        else:
            live = mask
        s = jnp.where(live, s, NEG_INF)
        m = jnp.max(s, axis=-1, keepdims=True)
        p = jnp.where(live, jnp.exp(s - m), 0.0)
        den = jnp.sum(p, axis=-1, keepdims=True)
        p = p / jnp.maximum(den, F32_TINY)
        if has_sink:
            p = jnp.where(sink_col, 0.0, p)
        o = jnp.dot(p.reshape(REP * Q_BLOCK, 2 * Q_BLOCK).astype(BF16), vcat,
                    preferred_element_type=F32)
        for r in range(REP):
            c0 = (g * REP + r) * HEAD_DIM
            o_ref[:, c0:c0 + HEAD_DIM] = o[r * Q_BLOCK:(r + 1) * Q_BLOCK].astype(o_ref.dtype)


def _window_attention(pa, batch, seq, q_col, k_col, v_col, bias, bias_head0, sinks, name):
    assert KV_HEADS == 2
    nb = seq // Q_BLOCK
    qw = REP * HEAD_DIM
    kvw = KV_HEADS * HEAD_DIM
    has_sink = sinks is not None
    sink_arr = jnp.broadcast_to(
        (sinks if has_sink else jnp.zeros((GROUP_HEADS,), F32)).reshape(GROUP_HEADS, 1, 1),
        (GROUP_HEADS, Q_BLOCK, 1))

    def cur(col):
        return lambda b, n: (b * nb + n, col // kvw)

    def prev(col):
        return lambda b, n: (b * nb + jnp.maximum(n - 1, 0), col // kvw)

    kv_block = (Q_BLOCK, kvw)
    return pl.pallas_call(
        functools.partial(_window_kernel, has_sink=has_sink),
        out_shape=jax.ShapeDtypeStruct((batch * seq, GROUP_WIDTH), F32),
        grid=(batch, nb),
        in_specs=[pl.BlockSpec((Q_BLOCK, qw), lambda b, n: (b * nb + n, q_col // qw)),
                  pl.BlockSpec((Q_BLOCK, qw), lambda b, n: (b * nb + n, q_col // qw + 1)),
                  pl.BlockSpec(kv_block, prev(k_col)), pl.BlockSpec(kv_block, cur(k_col)),
                  pl.BlockSpec(kv_block, prev(v_col)), pl.BlockSpec(kv_block, cur(v_col)),
                  pl.BlockSpec((GROUP_HEADS, Q_BLOCK, 2 * Q_BLOCK),
                               lambda b, n: (bias_head0 // GROUP_HEADS, 0, 0)),
                  pl.BlockSpec((GROUP_HEADS, Q_BLOCK, 1), lambda b, n: (0, 0, 0))],
        out_specs=pl.BlockSpec((Q_BLOCK, GROUP_WIDTH), lambda b, n: (b * nb + n, 0)),
        compiler_params=_params("parallel", "arbitrary"),
        name=name,
    )(pa, pa, pa, pa, pa, pa, bias, sink_arr)


def _compress_kernel(t_ref, pe_ref, w1_ref, w2_ref, o_ref):
    ncmp = o_ref.shape[3]
    half = CMP_LEN // 2
    pe = pe_ref[0]
    rows = [t_ref[pl.ds(r, ncmp, stride=CMP_STRIDE), :] for r in range(CMP_STRIDE)]
    xa = jnp.concatenate([rows[r] + pe[r:r + 1] for r in range(half)], axis=1).astype(BF16)
    xb = jnp.concatenate([rows[r] + pe[half + r:half + r + 1] for r in range(half)], axis=1).astype(BF16)
    kw = half * HEAD_DIM
    p0 = jnp.dot(xa, w1_ref[0, :kw, :], preferred_element_type=F32)
    p1 = jnp.dot(xb, w1_ref[0, kw:, :], preferred_element_type=F32)
    pre = p0 + pltpu.roll(p1, ncmp - 1, 0)
    hid = pre * jax.nn.sigmoid(pre)
    o_ref[0, 0, 0] = jnp.dot(hid.astype(BF16), w2_ref[0], preferred_element_type=F32).astype(o_ref.dtype)


def _compress(pf, batch, seq, pe, w1, w2):
    ncmp = seq // CMP_STRIDE
    return pl.pallas_call(
        _compress_kernel,
        out_shape=jax.ShapeDtypeStruct((2, batch, KV_HEADS, ncmp, HEAD_DIM), BF16),
        grid=(2, batch, KV_HEADS),
        in_specs=[pl.BlockSpec((seq, HEAD_DIM), lambda kv, b, g: (b, PF_KC // HEAD_DIM + KV_HEADS * kv + g)),
                  pl.BlockSpec((1, CMP_LEN, HEAD_DIM), lambda kv, b, g: (kv, 0, 0)),
                  pl.BlockSpec((1, CMP_LEN * HEAD_DIM, HEAD_DIM), lambda kv, b, g: (kv, 0, 0)),
                  pl.BlockSpec((1, HEAD_DIM, HEAD_DIM), lambda kv, b, g: (kv, 0, 0))],
        out_specs=pl.BlockSpec((1, 1, 1, ncmp, HEAD_DIM), lambda kv, b, g: (kv, b, g, 0, 0)),
        compiler_params=_params("arbitrary", "arbitrary", "arbitrary"),
        name="nsa_compress",
    )(pf, pe, w1.astype(BF16), w2.astype(BF16))


def _cmp_attn_kernel(q_ref, kc_ref, vc_ref, bias_ref, ov_ref, o_ref, sel_ref, *, n_sel):
    n = pl.program_id(2)
    ncmp = kc_ref.shape[3]
    nslc = sel_ref.shape[3]
    q4 = _stack_heads(q_ref[...])
    s = _qk(q4, kc_ref[0, 0, 0]).reshape(REP, Q_BLOCK, ncmp) * SCALE + bias_ref[...]
    t = n * Q_BLOCK + lax.broadcasted_iota(jnp.int32, (Q_BLOCK, ncmp), 0)
    c = lax.broadcasted_iota(jnp.int32, (Q_BLOCK, ncmp), 1)
    vis = (t - c * CMP_STRIDE - (CMP_LEN - 1) >= 0)[None]
    s = jnp.where(vis, s, NEG_INF)
    m = jnp.max(s, axis=-1, keepdims=True)
    m = jnp.where(m == NEG_INF, 0.0, m)
    p = jnp.where(vis, jnp.exp(s - m), 0.0)
    den = jnp.sum(p, axis=-1, keepdims=True)
    p = p / jnp.maximum(den, F32_TINY)
    o = jnp.dot(p.reshape(REP * Q_BLOCK, ncmp).astype(BF16), vc_ref[0, 0, 0], preferred_element_type=F32)
    _store_heads(o_ref, o, Q_BLOCK)

    psum = p[0]
    for r in range(1, REP):
        psum = psum + p[r]
    hi = psum.astype(BF16)
    lo = (psum - hi.astype(F32)).astype(BF16)
    ov = ov_ref[...]
    imp = jnp.dot(hi, ov, preferred_element_type=F32) + jnp.dot(lo, ov, preferred_element_type=F32)
    tq = n * Q_BLOCK + lax.broadcasted_iota(jnp.int32, (Q_BLOCK, nslc), 0)
    blk = lax.broadcasted_iota(jnp.int32, (Q_BLOCK, nslc), 1)
    cur = jnp.right_shift(tq, SLC_LEN.bit_length() - 1)
    forced = (blk == 0) | (blk == cur) | (blk == cur - 1)
    score = jnp.where(forced, jnp.inf, jnp.where(blk <= cur, imp, NEG_INF))
    rank = jnp.zeros((Q_BLOCK, nslc), F32)
    for k in range(nslc):
        col = score[:, k:k + 1]
        ahead = (col > score) | ((col == score) & (blk > k))
        rank = rank + jnp.where(ahead, 1.0, 0.0)
    sel_ref[0, 0] = jnp.where(rank < n_sel, 1.0, 0.0)


def _overlap_matrix(ncmp, nslc):
    cs = np.arange(ncmp)[:, None] * CMP_STRIDE
    ss = np.arange(nslc)[None, :] * SLC_LEN
    return ((cs < ss + SLC_LEN) & (cs + CMP_LEN > ss)).astype(np.float32)


def _cmp_attention(pa, cmp_kv, cbias, batch, seq):
    nb = seq // Q_BLOCK
    ncmp = seq // CMP_STRIDE
    nslc = seq // SLC_LEN
    qw = REP * HEAD_DIM
    ov = jnp.asarray(_overlap_matrix(ncmp, nslc), BF16)
    return pl.pallas_call(
        functools.partial(_cmp_attn_kernel, n_sel=min(N_SELECT, nslc)),
        out_shape=(jax.ShapeDtypeStruct((batch * seq, GROUP_WIDTH), F32),
                   jax.ShapeDtypeStruct((batch, KV_HEADS, seq, nslc), F32)),
        grid=(batch, KV_HEADS, nb),
        in_specs=[pl.BlockSpec((Q_BLOCK, qw), lambda b, g, n: (b * nb + n, PA_CQ // qw + g)),
                  pl.BlockSpec((1, 1, 1, ncmp, HEAD_DIM), lambda b, g, n: (0, b, g, 0, 0)),
                  pl.BlockSpec((1, 1, 1, ncmp, HEAD_DIM), lambda b, g, n: (1, b, g, 0, 0)),
                  pl.BlockSpec((REP, Q_BLOCK, ncmp), lambda b, g, n: (g, n, 0)),
                  pl.BlockSpec((ncmp, nslc), lambda b, g, n: (0, 0))],
        out_specs=(pl.BlockSpec((Q_BLOCK, qw), lambda b, g, n: (b * nb + n, g)),
                   pl.BlockSpec((1, 1, Q_BLOCK, nslc), lambda b, g, n: (b, g, n, 0))),
        compiler_params=_params("parallel", "parallel", "arbitrary"),
        name="nsa_cmp_attn",
    )(pa, cmp_kv, cmp_kv, cbias, ov)


FAR_TILE = 8 * Q_BLOCK


def _slc_attn_kernel(q_ref, k_ref, v_ref, sel_ref, bias_ref, far_bias_ref, efar_ref, enear_ref, o_ref):
    n = pl.program_id(2)
    q4 = _stack_heads(q_ref[...])
    selb = sel_ref[0, 0].astype(BF16)
    far_bias = far_bias_ref[...]
    rows = REP * Q_BLOCK

    def weights_times_values(p, v):
        pv = jnp.dot(p.reshape(rows, p.shape[-1]).astype(BF16), v, preferred_element_type=F32)
        return pv.reshape(REP, Q_BLOCK, HEAD_DIM)

    pb = jnp.maximum(n - 1, 0)
    p0 = pl.multiple_of(pb * Q_BLOCK, Q_BLOCK)
    c0 = pl.multiple_of(n * Q_BLOCK, Q_BLOCK)
    kcat = jnp.concatenate([k_ref[pl.ds(p0, Q_BLOCK), :], k_ref[pl.ds(c0, Q_BLOCK), :]], axis=0)
    vcat = jnp.concatenate([v_ref[pl.ds(p0, Q_BLOCK), :], v_ref[pl.ds(c0, Q_BLOCK), :]], axis=0)
    s = _qk(q4, kcat).reshape(REP, Q_BLOCK, 2 * Q_BLOCK) * SCALE + bias_ref[...]
    picked = jnp.concatenate(
        [jnp.dot(selb, enear_ref[pb], preferred_element_type=F32),
         jnp.dot(selb, enear_ref[n], preferred_element_type=F32)], axis=1) > 0.5
    qi = lax.broadcasted_iota(jnp.int32, (Q_BLOCK, 2 * Q_BLOCK), 0)
    kj = lax.broadcasted_iota(jnp.int32, (Q_BLOCK, 2 * Q_BLOCK), 1)
    causal = (qi + Q_BLOCK - kj >= 0) & ((n > 0) | (kj >= Q_BLOCK))
    s = jnp.where((picked & causal)[None], s, NEG_INF)
    m = jnp.max(s, axis=-1, keepdims=True)
    p = jnp.exp(s - m)
    carry = (m, jnp.sum(p, axis=-1, keepdims=True), weights_times_values(p, vcat))

    far_end = (n - 1) * Q_BLOCK

    def far_step(j, carry):
        m, l, acc = carry
        k0 = pl.multiple_of(j * FAR_TILE, FAR_TILE)
        k = k_ref[pl.ds(k0, FAR_TILE), :]
        v = v_ref[pl.ds(k0, FAR_TILE), :]
        s = _qk(q4, k).reshape(REP, Q_BLOCK, FAR_TILE) * SCALE + far_bias
        picked = jnp.dot(selb, efar_ref[j], preferred_element_type=F32) > 0.5
        kidx = k0 + lax.broadcasted_iota(jnp.int32, (Q_BLOCK, FAR_TILE), 1)
        s = jnp.where((picked & (kidx < far_end))[None], s, NEG_INF)
        m_new = jnp.maximum(m, jnp.max(s, axis=-1, keepdims=True))
        alpha = jnp.exp(m - m_new)
        p = jnp.exp(s - m_new)
        l = alpha * l + jnp.sum(p, axis=-1, keepdims=True)
        acc = alpha * acc + weights_times_values(p, v)
        return m_new, l, acc

    n_far = (jnp.maximum(far_end, 0) + FAR_TILE - 1) // FAR_TILE
    m, l, acc = lax.fori_loop(0, n_far, far_step, carry)
    o = acc / jnp.maximum(l, F32_TINY)
    for r in range(REP):
        o_ref[:, r * HEAD_DIM:(r + 1) * HEAD_DIM] = o[r].astype(o_ref.dtype)


def _expand_matrix(seq, tile):
    nslc = seq // SLC_LEN
    key = np.arange(seq).reshape(seq // tile, 1, tile)
    return (key // SLC_LEN == np.arange(nslc)[None, :, None]).astype(np.float32)


def _slc_attention(pa, sel, wbias, rel_bias_c, batch, seq):
    nb = seq // Q_BLOCK
    nslc = seq // SLC_LEN
    qw = REP * HEAD_DIM
    nfar = max(seq // FAR_TILE, 1)
    efar = jnp.asarray(_expand_matrix(max(seq, FAR_TILE), FAR_TILE)[:, :nslc], BF16)
    enear = jnp.asarray(_expand_matrix(seq, Q_BLOCK), BF16)
    far_bias = rel_bias_c[N_BUCKETS - 1].reshape(GROUP_HEADS, 1, 1)
    return pl.pallas_call(
        _slc_attn_kernel,
        out_shape=jax.ShapeDtypeStruct((batch * seq, GROUP_WIDTH), F32),
        grid=(batch, KV_HEADS, nb),
        in_specs=[pl.BlockSpec((Q_BLOCK, qw), lambda b, g, n: (b * nb + n, PA_CQ // qw + g)),
                  pl.BlockSpec((seq, HEAD_DIM), lambda b, g, n: (b, PA_CKS // HEAD_DIM + g)),
                  pl.BlockSpec((seq, HEAD_DIM), lambda b, g, n: (b, PA_CVS // HEAD_DIM + g)),
                  pl.BlockSpec((1, 1, Q_BLOCK, nslc), lambda b, g, n: (b, g, n, 0)),
                  pl.BlockSpec((REP, Q_BLOCK, 2 * Q_BLOCK),
                               lambda b, g, n: (GROUP_HEADS // REP + g, 0, 0)),
                  pl.BlockSpec((REP, 1, 1), lambda b, g, n: (g, 0, 0)),
                  pl.BlockSpec((nfar, nslc, FAR_TILE), lambda b, g, n: (0, 0, 0)),
                  pl.BlockSpec((nb, nslc, Q_BLOCK), lambda b, g, n: (0, 0, 0))],
        out_specs=pl.BlockSpec((Q_BLOCK, qw), lambda b, g, n: (b * nb + n, g)),
        compiler_params=_params("parallel", "parallel", "arbitrary"),
        name="nsa_slc_attn",
    )(pa, pa, pa, sel, wbias, far_bias, efar, enear)


SB_TK = 256
SB_BLOCKS = 4
SB_STEP = SB_BLOCKS * SB_TK
LOG2E = math.log2(math.e)


def _sb_kernel(q_ref, k_ref, v_ref, u_ref, o_ref):
    n = pl.program_id(2)
    u2 = u_ref[...]
    q = q_ref[...]

    def step(j, carry, masked):
        later, acc = carry
        k0 = pl.multiple_of(j * SB_STEP, SB_STEP)
        k = k_ref[pl.ds(k0, SB_STEP), :]
        v = v_ref[pl.ds(k0, SB_STEP), :]
        z = _qk(q, k) * (SCALE * LOG2E)
        zneg = jnp.minimum(z, 0.0)
        zpos_neg = zneg - z
        t = jnp.log2(1.0 + jnp.exp2(zneg + zpos_neg))
        log_keep = zpos_neg - t
        if masked:
            qi = lax.broadcasted_iota(jnp.int32, (SB_STEP, SB_STEP), 0)
            kj = lax.broadcasted_iota(jnp.int32, (SB_STEP, SB_STEP), 1)
            before = kj < qi
            log_keep = jnp.where(before, log_keep, 0.0)
        hi = log_keep.astype(BF16)
        lo = (log_keep - hi.astype(F32)).astype(BF16)
        blocks = [jnp.concatenate([hi[:, b * SB_TK:(b + 1) * SB_TK], lo[:, b * SB_TK:(b + 1) * SB_TK]],
                                  axis=1) for b in range(SB_BLOCKS)]
        suffix = jnp.dot(jnp.concatenate(blocks, axis=0), u2, preferred_element_type=F32)
        parts = [None] * SB_BLOCKS
        for b in range(SB_BLOCKS - 1, -1, -1):
            sfx = suffix[b * SB_STEP:(b + 1) * SB_STEP]
            parts[b] = jnp.exp2(z[:, b * SB_TK:(b + 1) * SB_TK] + sfx + later)
            later = later + sfx[:, 0:1]
        a = jnp.concatenate(parts, axis=1)
        if masked:
            a = jnp.where(before, a, 0.0)
        acc = acc + jnp.dot(a.astype(BF16), v, preferred_element_type=F32)
        return later, acc

    carry = (jnp.zeros((SB_STEP, 1), F32), jnp.zeros((SB_STEP, HEAD_DIM), F32))
    carry = step(n, carry, True)
    carry = lax.fori_loop(0, n, lambda i, c: step(n - 1 - i, c, False), carry)
    o_ref[...] = carry[1].astype(o_ref.dtype)


def _stick_breaking(pa, batch, seq):
    assert seq % SB_STEP == 0
    nb = seq // SB_STEP
    tri = np.tril(np.ones((SB_TK, SB_TK), np.float32))
    u2 = jnp.asarray(np.concatenate([tri, tri], axis=0), BF16)
    return pl.pallas_call(
        _sb_kernel,
        out_shape=jax.ShapeDtypeStruct((batch * seq, GROUP_WIDTH), F32),
        grid=(batch, GROUP_HEADS, nb),
        in_specs=[pl.BlockSpec((SB_STEP, HEAD_DIM), lambda b, h, n: (b * nb + n, PA_DQ // HEAD_DIM + h)),
                  pl.BlockSpec((seq, HEAD_DIM), lambda b, h, n: (b, PA_DK // HEAD_DIM + h)),
                  pl.BlockSpec((seq, HEAD_DIM), lambda b, h, n: (b, PA_DV // HEAD_DIM + h)),
                  pl.BlockSpec((2 * SB_TK, SB_TK), lambda b, h, n: (0, 0))],
        out_specs=pl.BlockSpec((SB_STEP, HEAD_DIM), lambda b, h, n: (b * nb + n, h)),
        compiler_params=_params("parallel", "parallel", "arbitrary"),
        name="stick_breaking",
    )(pa, pa, pa, u2)


CONV_TILE = 256
CONV_HALO = 32


def _conv_kernel(val_ref, gate_ref, hval_ref, hgate_ref, dw_ref, dwb_ref, lng_ref, lnb_ref, pw_ref,
                 o_ref, ext_ref):
    n = pl.program_id(1)
    halo = hval_ref[...] * jax.nn.sigmoid(hgate_ref[...])
    ext_ref[0:CONV_HALO, :] = jnp.where(n > 0, halo, 0.0)
    ext_ref[CONV_HALO:, :] = val_ref[...] * jax.nn.sigmoid(gate_ref[...])
    first = CONV_HALO - (CONV_WIDTH - 1)
    acc = jnp.zeros((CONV_TILE, GROUP_WIDTH), F32) + dwb_ref[...]
    for w in range(CONV_WIDTH):
        acc = acc + ext_ref[first + w:first + w + CONV_TILE, :] * dw_ref[w:w + 1, :]
    mu = jnp.mean(acc, axis=-1, keepdims=True)
    cen = acc - mu
    var = jnp.mean(cen * cen, axis=-1, keepdims=True)
    y = cen * lax.rsqrt(var + NORM_EPS) * lng_ref[...] + lnb_ref[...]
    y = y * jax.nn.sigmoid(y)
    o_ref[...] = jnp.dot(y.astype(BF16), pw_ref[...], preferred_element_type=F32).astype(o_ref.dtype)


def _conformer_conv(pf, batch, seq, dw, dw_b, ln_g, ln_b, pw):
    nt = seq // CONV_TILE
    hpt = CONV_TILE // CONV_HALO
    c = GROUP_WIDTH
    vec = pl.BlockSpec((1, c), lambda b, n: (0, 0))

    def halo(col):
        return lambda b, n: (jnp.maximum((b * nt + n) * hpt - 1, 0), col // c)

    return pl.pallas_call(
        _conv_kernel,
        out_shape=jax.ShapeDtypeStruct((batch * seq, c), F32),
        grid=(batch, nt),
        in_specs=[pl.BlockSpec((CONV_TILE, c), lambda b, n: (b * nt + n, PF_VAL // c)),
                  pl.BlockSpec((CONV_TILE, c), lambda b, n: (b * nt + n, PF_GATE // c)),
                  pl.BlockSpec((CONV_HALO, c), halo(PF_VAL)),
                  pl.BlockSpec((CONV_HALO, c), halo(PF_GATE)),
                  pl.BlockSpec((CONV_WIDTH, c), lambda b, n: (0, 0)),
                  vec, vec, vec,
                  pl.BlockSpec((c, c), lambda b, n: (0, 0))],
        out_specs=pl.BlockSpec((CONV_TILE, c), lambda b, n: (b * nt + n, 0)),
        scratch_shapes=[pltpu.VMEM((CONV_HALO + CONV_TILE, c), F32)],
        compiler_params=_params("parallel", "arbitrary"),
        name="conformer_conv",
    )(pf, pf, pf, pf, dw, dw_b.reshape(1, c), ln_g.reshape(1, c), ln_b.reshape(1, c), pw.astype(BF16))


MIX_TILE = 256


def _mix_kernel(oa_ref, ob_ref, ocmp_ref, oslc_ref, owin_ref, gl_ref, od_ref, g_ref, o_ref):
    gates = jax.nn.sigmoid(gl_ref[...])

    def norm_store(x, grp):
        y = x * lax.rsqrt(jnp.mean(x * x, axis=-1, keepdims=True) + NORM_EPS)
        sl = slice(grp * GROUP_WIDTH, (grp + 1) * GROUP_WIDTH)
        o_ref[:, sl] = (y * g_ref[:, sl]).astype(o_ref.dtype)

    norm_store(oa_ref[...], 0)
    norm_store(ob_ref[...], 1)
    heads = []
    for h in range(GROUP_HEADS):
        sl = slice(h * HEAD_DIM, (h + 1) * HEAD_DIM)
        c0 = h * N_BRANCH
        heads.append(gates[:, c0:c0 + 1] * ocmp_ref[:, sl] + gates[:, c0 + 1:c0 + 2] * oslc_ref[:, sl]
                     + gates[:, c0 + 2:c0 + 3] * owin_ref[:, sl])
    norm_store(jnp.concatenate(heads, axis=1), 2)
    norm_store(od_ref[...], 3)


def _mix(o_a, o_b, o_cmp, o_slc, o_win, pf, o_d, g):
    t = o_a.shape[0]
    grp = pl.BlockSpec((MIX_TILE, GROUP_WIDTH), lambda i: (i, 0))
    return pl.pallas_call(
        _mix_kernel,
        out_shape=jax.ShapeDtypeStruct((t, D_MODEL), BF16),
        grid=(t // MIX_TILE,),
        in_specs=[grp, grp, grp, grp, grp,
                  pl.BlockSpec((MIX_TILE, LANES), lambda i: (i, PF_G // LANES)),
                  grp,
                  pl.BlockSpec((1, D_MODEL), lambda i: (0, 0))],
        out_specs=pl.BlockSpec((MIX_TILE, D_MODEL), lambda i: (i, 0)),
        compiler_params=_params("parallel"),
        name="mix_norm",
    )(o_a, o_b, o_cmp, o_slc, o_win, pf, o_d, g.reshape(1, D_MODEL))


MOE_TILE = 512
MOE_UP_BN = 512
MOE_DOWN_BN = 512
ROUTE_TILE = 256
GATHER_TILE = 256
COMBINE_TILE = 128


def _norm_route_kernel(x_ref, g_ref, r_ref, h_ref, route_ref):
    x = x_ref[...]
    y = x * lax.rsqrt(jnp.mean(x * x, axis=-1, keepdims=True) + NORM_EPS) * g_ref[...]
    h_ref[...] = y
    logits = jnp.dot(y.astype(BF16), r_ref[...], preferred_element_type=F32)
    lane = lax.broadcasted_iota(jnp.int32, logits.shape, 1)
    logits = jnp.where(lane < N_EXPERTS, logits, NEG_INF)
    e = jnp.exp(logits - jnp.max(logits, axis=-1, keepdims=True))
    probs = e / jnp.sum(e, axis=-1, keepdims=True)
    p1 = jnp.max(probs, axis=-1, keepdims=True)
    i1 = jnp.min(jnp.where(probs == p1, lane, LANES), axis=-1, keepdims=True)
    rest = jnp.where(lane == i1, -1.0, probs)
    p2 = jnp.max(rest, axis=-1, keepdims=True)
    i2 = jnp.min(jnp.where(rest == p2, lane, LANES), axis=-1, keepdims=True)
    tot = p1 + p2
    route_ref[...] = jnp.where(lane == 0, p1 / tot,
                               jnp.where(lane == 1, p2 / tot,
                                         jnp.where(lane == 2, i1.astype(F32),
                                                   jnp.where(lane == 3, i2.astype(F32), 0.0))))


def _norm_route(x, g, router):
    m, d = x.shape
    rpad = jnp.zeros((d, LANES), BF16).at[:, :N_EXPERTS].set(router.astype(BF16))
    return pl.pallas_call(
        _norm_route_kernel,
        out_shape=(jax.ShapeDtypeStruct((m, d), F32), jax.ShapeDtypeStruct((m, LANES), F32)),
        grid=(m // ROUTE_TILE,),
        in_specs=[pl.BlockSpec((ROUTE_TILE, d), lambda i: (i, 0)),
                  pl.BlockSpec((1, d), lambda i: (0, 0)),
                  pl.BlockSpec((d, LANES), lambda i: (0, 0))],
        out_specs=(pl.BlockSpec((ROUTE_TILE, d), lambda i: (i, 0)),
                   pl.BlockSpec((ROUTE_TILE, LANES), lambda i: (i, 0))),
        compiler_params=_params("parallel"),
        name="ffn_norm_route",
    )(x, g.reshape(1, d), rpad)


def _row_copy(src_hbm, dst_ref, sem, src_row, dst_row):
    return pltpu.make_async_copy(src_hbm.at[pl.ds(src_row, 1), :], dst_ref.at[pl.ds(dst_row, 1), :], sem)


def _gather_kernel(tok_ref, nu_ref, h_hbm, o_ref, buf_ref, sem):
    i = pl.program_id(0)
    n_live = nu_ref[0] * (MOE_TILE // GATHER_TILE)

    def issue(tile, slot):
        def start(r, c):
            _row_copy(h_hbm, buf_ref.at[slot], sem.at[slot], tok_ref[tile * GATHER_TILE + r], r).start()
            return c
        lax.fori_loop(0, GATHER_TILE, start, 0)

    @pl.when(i == 0)
    def _():
        issue(0, 0)

    @pl.when(i + 1 < n_live)
    def _():
        issue(i + 1, (i + 1) % 2)

    @pl.when(i < n_live)
    def _():
        slot = i % 2

        def wait(r, c):
            _row_copy(h_hbm, buf_ref.at[slot], sem.at[slot], 0, r).wait()
            return c

        lax.fori_loop(0, GATHER_TILE, wait, 0)
        o_ref[...] = buf_ref[slot].astype(o_ref.dtype)

    @pl.when(i >= n_live)
    def _():
        o_ref[...] = jnp.zeros_like(o_ref)


def _gather_rows(h, row_token, n_used):
    rows = row_token.shape[0]
    d = h.shape[1]
    return pl.pallas_call(
        _gather_kernel,
        out_shape=jax.ShapeDtypeStruct((rows, d), BF16),
        grid_spec=pltpu.PrefetchScalarGridSpec(
            num_scalar_prefetch=2,
            grid=(rows // GATHER_TILE,),
            in_specs=[pl.BlockSpec(memory_space=pl.ANY)],
            out_specs=pl.BlockSpec((GATHER_TILE, d), lambda i, tok, nu: (i, 0)),
            scratch_shapes=[pltpu.VMEM((2, GATHER_TILE, d), F32), pltpu.SemaphoreType.DMA((2,))]),
        compiler_params=_params("arbitrary"),
        name="moe_gather",
    )(row_token, n_used, h)


CAST_ROWS = 128


def _swiglu_tile(x, s_ref):
    g = jnp.dot(x, s_ref[0], preferred_element_type=F32)
    u = jnp.dot(x, s_ref[1], preferred_element_type=F32)
    return g * jax.nn.sigmoid(g) * u


def _plain_tile(x, s_ref):
    return jnp.dot(x, s_ref[0], preferred_element_type=F32)


def _grouped_kernel(te_ref, nu_ref, first_ref, run_ref, nruns_ref, rune_ref, x_ref, *rest,
                    n_weights, bn, n_cols, tile_fn):
    w_hbm = rest[:n_weights]
    o_ref, wbuf_ref, s_ref, sem = rest[n_weights:]
    j = pl.program_id(0)
    i = pl.program_id(1)
    nj = pl.num_programs(0)
    k = s_ref.shape[1]
    last_width = n_cols - (pl.cdiv(n_cols, bn) - 1) * bn
    live = i < nu_ref[0]

    def tile_copy(w, expert, jj, slot, width):
        col0 = pl.multiple_of(jj * bn, bn)
        return pltpu.make_async_copy(w_hbm[w].at[expert, :, pl.ds(col0, width)],
                                     wbuf_ref.at[slot, w, :, pl.ds(0, width)], sem.at[slot, w])

    def by_width(jj, fn):
        if last_width == bn:
            fn(bn)
        else:
            pl.when(jj < nj - 1)(lambda: fn(bn))
            pl.when(jj == nj - 1)(lambda: fn(last_width))

    def issue(expert, jj, slot):
        def go(width):
            for w in range(n_weights):
                tile_copy(w, expert, jj, slot, width).start()
        by_width(jj, go)

    def wait_and_cast(expert, jj, slot):
        def go(width):
            for w in range(n_weights):
                tile_copy(w, expert, jj, slot, width).wait()

                def rows(c, carry, w=w):
                    r0 = pl.multiple_of(c * CAST_ROWS, CAST_ROWS)
                    s_ref[w, pl.ds(r0, CAST_ROWS), 0:width] = (
                        wbuf_ref[slot, w, pl.ds(r0, CAST_ROWS), 0:width].astype(BF16))
                    return carry

                lax.fori_loop(0, k // CAST_ROWS, rows, 0)
        by_width(jj, go)

    @pl.when(live & (first_ref[i] == 1))
    def _():
        run = run_ref[i]
        group = j * nruns_ref[0] + run
        slot = group % 2
        expert = te_ref[i]

        @pl.when(group == 0)
        def _():
            issue(expert, j, 0)

        wait_and_cast(expert, j, slot)
        wrap = run + 1 == nruns_ref[0]
        next_run = jnp.where(wrap, 0, run + 1)
        next_j = jnp.where(wrap, j + 1, j)

        @pl.when(next_j < nj)
        def _():
            issue(rune_ref[next_run], next_j, 1 - slot)

    @pl.when(live)
    def _():
        o_ref[...] = tile_fn(x_ref[...], s_ref).astype(o_ref.dtype)

    @pl.when(jnp.logical_not(live))
    def _():
        o_ref[...] = jnp.zeros_like(o_ref)


def _grouped_call(tile_fn, tables, xs, weights, bn, out_dtype, name):
    rows, k = xs.shape
    n = weights[0].shape[2]
    nt = rows // MOE_TILE
    nw = len(weights)
    assert k % CAST_ROWS == 0 and bn % LANES == 0 and n % LANES == 0

    def used(i, nu):
        return jnp.minimum(i, nu[0] - 1)

    body = functools.partial(_grouped_kernel, n_weights=nw, bn=bn, n_cols=n, tile_fn=tile_fn)
    return pl.pallas_call(
        body,
        out_shape=jax.ShapeDtypeStruct((rows, n), out_dtype),
        grid_spec=pltpu.PrefetchScalarGridSpec(
            num_scalar_prefetch=len(tables),
            grid=(pl.cdiv(n, bn), nt),
            in_specs=[pl.BlockSpec((MOE_TILE, k), lambda j, i, te, nu, *_: (used(i, nu), 0))]
            + [pl.BlockSpec(memory_space=pl.ANY) for _ in weights],
            out_specs=pl.BlockSpec((MOE_TILE, bn), lambda j, i, *_: (i, j)),
            scratch_shapes=[pltpu.VMEM((2, nw, k, bn), F32), pltpu.VMEM((nw, k, bn), BF16),
                            pltpu.SemaphoreType.DMA((2, nw))]),
        compiler_params=_params("arbitrary", "arbitrary"),
        name=name,
    )(*tables, xs, *weights)


def _combine_kernel(dest_ref, x_ref, route_ref, y_hbm, *rest, normalize):
    if normalize:
        g_ref, o_ref, y0_ref, y1_ref, sem = rest
    else:
        o_ref, y0_ref, y1_ref, sem = rest
    i = pl.program_id(0)

    def issue(tile, slot):
        def start(r, c):
            tok = tile * COMBINE_TILE + r
            _row_copy(y_hbm, y0_ref.at[slot], sem.at[slot], dest_ref[2 * tok], r).start()
            _row_copy(y_hbm, y1_ref.at[slot], sem.at[slot], dest_ref[2 * tok + 1], r).start()
            return c
        lax.fori_loop(0, COMBINE_TILE, start, 0)

    @pl.when(i == 0)
    def _():
        issue(0, 0)

    @pl.when(i + 1 < pl.num_programs(0))
    def _():
        issue(i + 1, (i + 1) % 2)

    slot = i % 2

    def wait(r, c):
        _row_copy(y_hbm, y0_ref.at[slot], sem.at[slot], 0, r).wait()
        _row_copy(y_hbm, y1_ref.at[slot], sem.at[slot], 0, r).wait()
        return c

    lax.fori_loop(0, COMBINE_TILE, wait, 0)
    route = route_ref[...]
    out = x_ref[...] + route[:, 0:1] * y0_ref[slot] + route[:, 1:2] * y1_ref[slot]
    if normalize:
        out = out * lax.rsqrt(jnp.mean(out * out, axis=-1, keepdims=True) + NORM_EPS) * g_ref[...]
    o_ref[...] = out


def _moe_combine(x2, route, y, dest, out_norm_g=None):
    t, d = x2.shape
    normalize = out_norm_g is not None
    in_specs = [pl.BlockSpec((COMBINE_TILE, d), lambda i, dest: (i, 0)),
                pl.BlockSpec((COMBINE_TILE, LANES), lambda i, dest: (i, 0)),
                pl.BlockSpec(memory_space=pl.ANY)]
    args = [dest, x2, route, y]
    if normalize:
        in_specs.append(pl.BlockSpec((1, d), lambda i, dest: (0, 0)))
        args.append(out_norm_g.reshape(1, d))
    return pl.pallas_call(
        functools.partial(_combine_kernel, normalize=normalize),
        out_shape=jax.ShapeDtypeStruct((t, d), F32),
        grid_spec=pltpu.PrefetchScalarGridSpec(
            num_scalar_prefetch=1,
            grid=(t // COMBINE_TILE,),
            in_specs=in_specs,
            out_specs=pl.BlockSpec((COMBINE_TILE, d), lambda i, dest: (i, 0)),
            scratch_shapes=[pltpu.VMEM((2, COMBINE_TILE, d), F32), pltpu.VMEM((2, COMBINE_TILE, d), F32),
                            pltpu.SemaphoreType.DMA((2,))]),
        compiler_params=_params("arbitrary"),
        name="moe_combine",
    )(*args)


def _routing_tables(top_i):
    t = top_i.shape[0]
    e_flat = top_i.reshape(-1)
    onehot = (e_flat[:, None] == jnp.arange(N_EXPERTS, dtype=jnp.int32)[None, :]).astype(jnp.int32)
    csum = jnp.cumsum(onehot, axis=0)
    counts = csum[-1]
    pos = jnp.take_along_axis(csum, e_flat[:, None], axis=1)[:, 0] - 1
    padded = ((counts + MOE_TILE - 1) // MOE_TILE) * MOE_TILE
    ends = jnp.cumsum(padded)
    dest = ((ends - padded)[e_flat] + pos).astype(jnp.int32)
    rows = TOP_K * t + N_EXPERTS * MOE_TILE
    row_token = jnp.zeros((rows,), jnp.int32).at[dest].set(jnp.arange(TOP_K * t, dtype=jnp.int32) // TOP_K)
    tile_start = jnp.arange(rows // MOE_TILE, dtype=jnp.int32) * MOE_TILE
    tile_expert = jnp.minimum(jnp.sum(tile_start[:, None] >= ends[None, :], axis=1), N_EXPERTS - 1)
    n_used = (ends[-1:] // MOE_TILE).astype(jnp.int32)
    tile_expert = tile_expert.astype(jnp.int32)
    tiles = jnp.arange(rows // MOE_TILE, dtype=jnp.int32)
    prev = jnp.concatenate([jnp.full((1,), -1, jnp.int32), tile_expert[:-1]])
    first = ((tiles < n_used[0]) & (tile_expert != prev)).astype(jnp.int32)
    run = jnp.maximum(jnp.cumsum(first) - 1, 0).astype(jnp.int32)
    n_runs = jnp.sum(first, keepdims=True).astype(jnp.int32)
    run_expert = jnp.zeros((N_EXPERTS,), jnp.int32).at[
        jnp.where(first == 1, run, N_EXPERTS)].set(tile_expert, mode="drop")
    return dest, row_token, n_used, (tile_expert, n_used, first, run, n_runs, run_expert)


def _moe_swiglu(x2, norm_g, router, w_gate, w_up, w_down, out_norm_g=None):
    h, route = _norm_route(x2, norm_g, router)
    top_i = route[:, 2:4].astype(jnp.int32)
    dest, row_token, n_used, tables = _routing_tables(top_i)
    xs = _gather_rows(h, row_token, n_used)
    hid = _grouped_call(_swiglu_tile, tables, xs, [w_gate, w_up], MOE_UP_BN, BF16, "moe_up")
    y = _grouped_call(_plain_tile, tables, hid, [w_down], MOE_DOWN_BN, F32, "moe_down")
    return _moe_combine(x2, route, y, dest, out_norm_g)


def _split_in_weights(w):
    sizes = (1024, 256, 256, 1024, 1024, 1024, 256, 256, 256, 256, 256, 256, 24, 1024, 1024, 1024)
    offs = np.concatenate([[0], np.cumsum(sizes)])
    (a_q, a_k, a_v, b_val, b_gate, c_q, c_kc, c_vc, c_ks, c_vs, c_kw, c_vw, c_g,
     d_q, d_k, d_v) = [w[:, int(offs[i]):int(offs[i + 1])] for i in range(len(sizes))]
    wa = jnp.concatenate([a_q, a_k, a_v, c_q, c_ks, c_vs, c_kw, c_vw, d_q, d_k, d_v], axis=1)
    pad = jnp.zeros((w.shape[0], PF_COLS - PF_G - c_g.shape[1]), w.dtype)
    wf = jnp.concatenate([b_val, b_gate, c_kc, c_vc, c_g, pad], axis=1)
    return wa.astype(BF16), wf.astype(BF16)


def _mixer_layer(x2, batch, seq, layer, p, wbias, cbias):
    h = _rmsnorm(x2, p["attn_norm_g"][layer], out_dtype=BF16, name="attn_norm")
    wa, wf = _split_in_weights(p["w_in"][layer])
    pa = _matmul(h, wa, bm=1024, bn=512, out_dtype=BF16, name="in_proj_attn")
    pf = _matmul(h, wf, bm=1024, bn=PF_COLS // 3, out_dtype=F32, name="in_proj_f32")

    o_a = _window_attention(pa, batch, seq, PA_AQ, PA_AK, PA_AV, wbias, 0,
                            p["swa_sinks"][layer], "swa_attn")
    o_b = _conformer_conv(pf, batch, seq, p["conv_dw"][layer], p["conv_dw_b"][layer],
                          p["conv_ln_g"][layer], p["conv_ln_b"][layer], p["conv_pw"][layer])
    cmp_kv = _compress(pf, batch, seq, p["nsa_cmp_pe"][layer], p["nsa_cmp_w1"][layer],
                       p["nsa_cmp_w2"][layer])
    o_cmp, sel = _cmp_attention(pa, cmp_kv, cbias, batch, seq)
    o_slc = _slc_attention(pa, sel, wbias, p["rel_bias"][:, GROUP_HEADS:], batch, seq)
    o_win = _window_attention(pa, batch, seq, PA_CQ, PA_CKW, PA_CVW, wbias, GROUP_HEADS, None,
                              "nsa_win_attn")
    o_d = _stick_breaking(pa, batch, seq)
    mixed = _mix(o_a, o_b, o_cmp, o_slc, o_win, pf, o_d, p["mix_norm_g"][layer])
    return _matmul(mixed, p["w_out"][layer].astype(BF16), bm=1024, bn=512, out_dtype=F32,
                   residual=x2, name="out_proj")


def kernel(x, attn_norm_g, ffn_norm_g, final_norm_g, w_in, w_out, mix_norm_g, rel_bias,
           swa_sinks, conv_dw, conv_dw_b, conv_ln_g, conv_ln_b, conv_pw, nsa_cmp_pe,
           nsa_cmp_w1, nsa_cmp_w2, ffn_w_gate, ffn_w_up, ffn_w_down, moe_router,
           moe_w_gate, moe_w_up, moe_w_down):
    batch, seq, _ = x.shape
    t = batch * seq
    p = dict(attn_norm_g=attn_norm_g, w_in=w_in, w_out=w_out, mix_norm_g=mix_norm_g,
             rel_bias=rel_bias, swa_sinks=swa_sinks, conv_dw=conv_dw, conv_dw_b=conv_dw_b,
             conv_ln_g=conv_ln_g, conv_ln_b=conv_ln_b, conv_pw=conv_pw, nsa_cmp_pe=nsa_cmp_pe,
             nsa_cmp_w1=nsa_cmp_w1, nsa_cmp_w2=nsa_cmp_w2)
    wbias = _win_bias(rel_bias)
    cbias = _cmp_bias(rel_bias, seq, GROUP_HEADS)
    x2 = x.reshape(t, D_MODEL)
    for layer in range(DEPTH):
        x2 = _mixer_layer(x2, batch, seq, layer, p, wbias, cbias)
        i = layer // 2
        if layer % 2 == 0:
            hf = _rmsnorm(x2, ffn_norm_g[layer], out_dtype=BF16, name="ffn_norm")
            hid = _swiglu_up(hf, ffn_w_gate[i].astype(BF16), ffn_w_up[i].astype(BF16),
                             bm=1024, bn=256, name="ffn_up")
            x2 = _matmul_ksplit_res(hid, ffn_w_down[i].astype(BF16), x2,
                                    bm=1024, bn=512, bk=D_FF // 2, name="ffn_down")
        else:
            last = layer == DEPTH - 1
            x2 = _moe_swiglu(x2, ffn_norm_g[layer], moe_router[i], moe_w_gate[i], moe_w_up[i],
                             moe_w_down[i], out_norm_g=final_norm_g if last else None)
    if DEPTH % 2 == 1:
        x2 = _rmsnorm(x2, final_norm_g, out_dtype=F32, name="final_norm")
    return x2.reshape(batch, seq, D_MODEL)
```
